```python
import jax, jax.numpy as jnp
from jax import lax
import numpy as np

D_MODEL = 1024
BATCH = 2
SEQ = 16384
DEPTH = 1
DEC_BATCH = 32
DEC_SEQ = 32
PAST_LEN = 2048

CHUNK = 64
P_DIM = 256
N_HEADS_A = 8
HEAD_DK = 128
HEAD_DV = 128
KEY_DIM = N_HEADS_A * HEAD_DK
VAL_DIM = N_HEADS_A * HEAD_DV
QKV_DIM = 2 * KEY_DIM + VAL_DIM
CONV_A = 4
WIDTH_B = D_MODEL
CONV_B = 3
ALPHA = (2 * DEPTH) ** 0.25
BETA_INIT = (8 * DEPTH) ** -0.25
LN_EPS = 1e-5
RMS_EPS = 1e-6
L2_EPS = 1e-6

OFF_ZA = QKV_DIM
OFF_BETA = OFF_ZA + VAL_DIM
OFF_DECAY = OFF_BETA + N_HEADS_A
OFF_BB = OFF_DECAY + N_HEADS_A
OFF_CB = OFF_BB + WIDTH_B
OFF_UB = OFF_CB + WIDTH_B
OFF_ZB = OFF_UB + WIDTH_B
OFF_GATE = OFF_ZB + WIDTH_B
IN_DIM = OFF_GATE + 2 * D_MODEL

kernel_name = 'hybrid_gdn_shortconv_stream'


def layer_norm(x, g, b):
    xf = x.astype(jnp.float32)
    mu = jnp.mean(xf, -1, keepdims=True)
    var = jnp.mean(jnp.square(xf - mu), -1, keepdims=True)
    return ((xf - mu) * lax.rsqrt(var + LN_EPS) * g.astype(jnp.float32) + b.astype(jnp.float32)).astype(x.dtype)


def l2norm(x):
    return x * lax.rsqrt(jnp.sum(x * x, -1, keepdims=True) + L2_EPS)


def causal_dwconv(u, buf, w):
    k_w = w.shape[0]
    t = u.shape[1]
    full = jnp.concatenate([buf.astype(u.dtype), u], axis=1)
    y = full[:, 0:t] * w[0]
    for j in range(1, k_w):
        y = y + full[:, j:j + t] * w[j]
    return y, full[:, t:]


def gated_delta_chunked(q, k, v, beta, g, s0, chunk):
    bsz, t, h, dk = q.shape
    dv = v.shape[-1]
    n = t // chunk

    def blk(a):
        a = a.reshape((bsz, n, chunk, h) + a.shape[3:])
        return jnp.moveaxis(a, 3, 1)

    q, k, v, beta, g = blk(q), blk(k), blk(v), blk(beta), blk(g)
    gc = jnp.cumsum(g, axis=-1)
    causal = jnp.tril(jnp.ones((chunk, chunk), bool))
    strict = jnp.tril(jnp.ones((chunk, chunk), bool), -1)
    diff = gc[..., :, None] - gc[..., None, :]
    decay = jnp.where(causal, jnp.exp(jnp.where(causal, diff, 0.0)), 0.0)
    kk = jnp.einsum('bhnid,bhnjd->bhnij', k, k)
    a_mat = jnp.where(strict, beta[..., :, None] * kk * decay, 0.0)
    eye = jnp.eye(chunk, dtype=jnp.float32)
    t_mat = lax.linalg.triangular_solve(eye + a_mat, jnp.broadcast_to(eye, a_mat.shape),
                                        left_side=True, lower=True, unit_diagonal=True)
    u_blk = jnp.einsum('bhnij,bhnje->bhnie', t_mat, v * beta[..., None])
    w_blk = jnp.einsum('bhnij,bhnjd->bhnid', t_mat, k * (beta * jnp.exp(gc))[..., None])
    qk = jnp.where(causal, jnp.einsum('bhnid,bhnjd->bhnij', q, k) * decay, 0.0)
    q_dec = q * jnp.exp(gc)[..., None]
    k_dec = k * jnp.exp(gc[..., -1:] - gc)[..., None]
    g_last = jnp.exp(gc[..., -1])

    def step(s, xs):
        u_c, w_c, qk_c, qd_c, kd_c, gl_c = xs
        v_new = u_c - jnp.einsum('bhid,bhde->bhie', w_c, s)
        o_c = jnp.einsum('bhid,bhde->bhie', qd_c, s) + jnp.einsum('bhij,bhje->bhie', qk_c, v_new)
        s = s * gl_c[..., None, None] + jnp.einsum('bhid,bhie->bhde', kd_c, v_new)
        return s, o_c

    xs = tuple(jnp.moveaxis(a, 2, 0) for a in (u_blk, w_blk, qk, q_dec, k_dec, g_last))
    s_fin, o = lax.scan(step, s0, xs)
    o = jnp.moveaxis(o, 0, 2)
    o = jnp.moveaxis(o, 1, 3).reshape(bsz, t, h, dv)
    return o, s_fin


def trunk_layer(h, p, conv_a_buf, s_gdn, conv_b_buf, chunk, w_in, w_conv_a, a_log, dt_bias,
                norm_a_g, w_conv_b, w_proj_a, w_proj_b, w_out, ln1_g, ln1_b, w_ple, w_ple_gate,
                ln2_g, ln2_b):
    f32 = jnp.float32
    bsz, t = h.shape[0], h.shape[1]
    z = h @ w_in
    qkv, conv_a_new = causal_dwconv(z[..., :QKV_DIM], conv_a_buf, w_conv_a)
    qkv = jax.nn.silu(qkv).astype(f32)
    q = l2norm(qkv[..., :KEY_DIM].reshape(bsz, t, N_HEADS_A, HEAD_DK)) * (HEAD_DK ** -0.5)
    k = l2norm(qkv[..., KEY_DIM:2 * KEY_DIM].reshape(bsz, t, N_HEADS_A, HEAD_DK))
    v = qkv[..., 2 * KEY_DIM:].reshape(bsz, t, N_HEADS_A, HEAD_DV)
    beta = jax.nn.sigmoid(z[..., OFF_BETA:OFF_DECAY].astype(f32))
    g = -jnp.exp(a_log.astype(f32)) * jax.nn.softplus(z[..., OFF_DECAY:OFF_BB].astype(f32) + dt_bias.astype(f32))
    o, s_new = gated_delta_chunked(q, k, v, beta, g, s_gdn.astype(f32), chunk)
    gate_a = z[..., OFF_ZA:OFF_BETA].astype(f32).reshape(bsz, t, N_HEADS_A, HEAD_DV)
    o = o * lax.rsqrt(jnp.mean(o * o, -1, keepdims=True) + RMS_EPS) * norm_a_g.astype(f32) * jax.nn.silu(gate_a)
    y_a = o.reshape(bsz, t, VAL_DIM).astype(h.dtype)
    cu = z[..., OFF_CB:OFF_UB] * z[..., OFF_UB:OFF_ZB]
    conv_b_out, conv_b_new = causal_dwconv(cu, conv_b_buf, w_conv_b)
    y_b = z[..., OFF_BB:OFF_CB] * conv_b_out * jax.nn.silu(z[..., OFF_ZB:OFF_GATE])
    gates = jax.nn.sigmoid(z[..., OFF_GATE:])
    merged = gates[..., :D_MODEL] * (y_a @ w_proj_a) + gates[..., D_MODEL:] * (y_b @ w_proj_b)
    h1 = layer_norm(ALPHA * h + merged @ w_out, ln1_g, ln1_b)
    ple = jax.nn.sigmoid(h1 @ w_ple_gate) * (p @ w_ple)
    h2 = layer_norm(ALPHA * h1 + ple, ln2_g, ln2_b)
    return h2, conv_a_new, s_new.astype(s_gdn.dtype), conv_b_new


def setup_inputs(seed: int = 0) -> dict:
    key = jax.random.key(seed)
    ks = jax.random.split(key, 32)
    nrm = jax.random.normal
    d = D_MODEL
    dt = jnp.exp(jax.random.uniform(ks[10], (DEPTH, N_HEADS_A), minval=np.log(1e-3), maxval=np.log(1e-1)))
    return {
        'x_prompt': nrm(ks[0], (BATCH, SEQ, d), jnp.float32),
        'x_sample': nrm(ks[1], (DEC_BATCH, DEC_SEQ, d), jnp.float32),
        'state_conv_a': nrm(ks[2], (DEPTH, DEC_BATCH, CONV_A - 1, QKV_DIM), jnp.float32),
        'state_gdn': 0.1 * nrm(ks[3], (DEPTH, DEC_BATCH, N_HEADS_A, HEAD_DK, HEAD_DV), jnp.float32),
        'state_conv_b': nrm(ks[4], (DEPTH, DEC_BATCH, CONV_B - 1, WIDTH_B), jnp.float32),
        'p_prompt': nrm(ks[5], (DEPTH, BATCH, SEQ, P_DIM), jnp.float32),
        'p_sample': nrm(ks[6], (DEPTH, DEC_BATCH, DEC_SEQ, P_DIM), jnp.float32),
        'ln_in_g': 1.0 + 0.01 * nrm(ks[7], (d,), jnp.float32),
        'ln_in_b': 0.01 * nrm(ks[8], (d,), jnp.float32),
        'w_in': nrm(ks[9], (DEPTH, d, IN_DIM), jnp.float32) * d ** -0.5,
        'w_conv_a': nrm(ks[11], (DEPTH, CONV_A, QKV_DIM), jnp.float32) * CONV_A ** -0.5,
        'a_log': jnp.log(jax.random.uniform(ks[12], (DEPTH, N_HEADS_A), minval=1.0, maxval=16.0)),
        'dt_bias': dt + jnp.log(-jnp.expm1(-dt)),
        'norm_a_g': 1.0 + 0.01 * nrm(ks[13], (DEPTH, HEAD_DV), jnp.float32),
        'w_conv_b': nrm(ks[14], (DEPTH, CONV_B, WIDTH_B), jnp.float32) * CONV_B ** -0.5,
        'w_proj_a': nrm(ks[15], (DEPTH, VAL_DIM, d), jnp.float32) * VAL_DIM ** -0.5 * BETA_INIT,
        'w_proj_b': nrm(ks[16], (DEPTH, WIDTH_B, d), jnp.float32) * WIDTH_B ** -0.5 * BETA_INIT,
        'w_out': nrm(ks[17], (DEPTH, d, d), jnp.float32) * d ** -0.5 * BETA_INIT,
        'ln1_g': 1.0 + 0.01 * nrm(ks[18], (DEPTH, d), jnp.float32),
        'ln1_b': 0.01 * nrm(ks[19], (DEPTH, d), jnp.float32),
        'w_ple': nrm(ks[20], (DEPTH, P_DIM, d), jnp.float32) * P_DIM ** -0.5 * BETA_INIT,
        'w_ple_gate': nrm(ks[21], (DEPTH, d, d), jnp.float32) * d ** -0.5,
        'ln2_g': 1.0 + 0.01 * nrm(ks[22], (DEPTH, d), jnp.float32),
        'ln2_b': 0.01 * nrm(ks[23], (DEPTH, d), jnp.float32),
    }


def reference(x_prompt, x_sample, state_conv_a, state_gdn, state_conv_b, p_prompt, p_sample,
              ln_in_g, ln_in_b, w_in, w_conv_a, a_log, dt_bias, norm_a_g, w_conv_b, w_proj_a,
              w_proj_b, w_out, ln1_g, ln1_b, w_ple, w_ple_gate, ln2_g, ln2_b):
    def encode(x, p, conv_a, s_gdn, conv_b, chunk):
        h = layer_norm(x, ln_in_g, ln_in_b)
        new_a, new_s, new_b = [], [], []
        for i in range(DEPTH):
            h, ca, s, cb = trunk_layer(h, p[i], conv_a[i], s_gdn[i], conv_b[i], chunk, w_in[i],
                                       w_conv_a[i], a_log[i], dt_bias[i], norm_a_g[i], w_conv_b[i],
                                       w_proj_a[i], w_proj_b[i], w_out[i], ln1_g[i], ln1_b[i],
                                       w_ple[i], w_ple_gate[i], ln2_g[i], ln2_b[i])
            new_a.append(ca)
            new_s.append(s)
            new_b.append(cb)
        return h, jnp.stack(new_a), jnp.stack(new_s), jnp.stack(new_b)

    bp = x_prompt.shape[0]
    zero_a = jnp.zeros((DEPTH, bp, CONV_A - 1, QKV_DIM), x_prompt.dtype)
    zero_s = jnp.zeros((DEPTH, bp, N_HEADS_A, HEAD_DK, HEAD_DV), state_gdn.dtype)
    zero_b = jnp.zeros((DEPTH, bp, CONV_B - 1, WIDTH_B), x_prompt.dtype)
    y_prompt, ca_p, s_p, cb_p = encode(x_prompt, p_prompt, zero_a, zero_s, zero_b, CHUNK)
    y_sample, ca_s, s_s, cb_s = encode(x_sample, p_sample, state_conv_a, state_gdn, state_conv_b,
                                       x_sample.shape[1])
    return (y_prompt, y_sample, ca_p, s_p, cb_p, ca_s, s_s, cb_s)
```

```python
import functools

import jax
import jax.numpy as jnp
from jax import lax
from jax.experimental import pallas as pl
from jax.experimental.pallas import tpu as pltpu

D_MODEL = 1024
N_HEADS = 8
HEAD_D = 128
KEY_DIM = N_HEADS * HEAD_D
QKV_DIM = 3 * KEY_DIM
WIDTH_B = D_MODEL
P_DIM = 256
CONV_A = 4
CONV_B = 3
PROMPT_CHUNK = 64
DEPTH = 1
ALPHA = (2 * DEPTH) ** 0.25
LN_EPS = 1e-5
RMS_EPS = 1e-6
L2_EPS = 1e-6

OFF_ZA = QKV_DIM
OFF_BETA = OFF_ZA + KEY_DIM
OFF_DECAY = OFF_BETA + N_HEADS
OFF_BB = OFF_DECAY + N_HEADS
OFF_CB = OFF_BB + WIDTH_B
OFF_UB = OFF_CB + WIDTH_B
OFF_ZB = OFF_UB + WIDTH_B
OFF_GATE = OFF_ZB + WIDTH_B

SUBLANES = 8
LANES = 128
INV_BLOCK = 16
PROMPT_TILE = 256
VMEM_LIMIT_BYTES = 60 * 1024 * 1024

F32 = jnp.float32
BF16 = jnp.bfloat16


def _sigmoid(x):
    return 1.0 / (1.0 + jnp.exp(-x))


def _silu(x):
    return x * _sigmoid(x)


def _softplus(x):
    return jnp.maximum(x, 0.0) + jnp.log(1.0 + jnp.exp(-jnp.abs(x)))


def _layer_norm(x, g, b):
    mu = jnp.mean(x, axis=-1, keepdims=True)
    xc = x - mu
    var = jnp.mean(xc * xc, axis=-1, keepdims=True)
    return xc * lax.rsqrt(var + LN_EPS) * g + b


def _mm(a, b):
    return jnp.dot(a.astype(BF16), b.astype(BF16), preferred_element_type=F32)


def _causal_conv(seg, hist8, w_ref, ntaps):
    row8 = lax.broadcasted_iota(jnp.int32, (SUBLANES, 1), 0)
    acc = seg * w_ref[ntaps - 1:ntaps, :]
    for s in range(1, ntaps):
        rolled = pltpu.roll(seg, s, 0)
        head = jnp.where(row8 < s, pltpu.roll(hist8, s, 0), rolled[0:SUBLANES])
        shifted = jnp.concatenate([head, rolled[SUBLANES:]], axis=0)
        acc = acc + shifted * w_ref[ntaps - 1 - s:ntaps - s, :]
    return acc


def _inv_unit_lower(a, diag_mask, nblk):
    d = jnp.where(diag_mask, a, 0.0)
    low = a - d
    d2 = _mm(d, d)
    d4 = _mm(d2, d2)
    d8 = _mm(d4, d4)
    x1 = d2 - d - _mm(d, d2)
    x2 = d4 + d8 + _mm(d4, d8)
    xd = x1 + x2 + _mm(x1, x2)
    n = low + _mm(xd, low)
    if nblk == 2:
        xq = -n
    elif nblk == 4:
        n2 = _mm(n, n)
        xq = n2 - n - _mm(n, n2)
    else:
        raise NotImplementedError(nblk)
    return xq + xd + _mm(xq, xd)


def _layer_kernel(cfg,
                  x_ref, p_ref, hista_ref, sin_ref, histb_ref,
                  ln_in_g, ln_in_b, w_qkv, w_za, w_beta, w_dec, w_b, w_gate,
                  w_conv_a, a_log, dt_bias, norm_a_g, w_conv_b,
                  w_proj_a, w_proj_b, w_out, ln1_g, ln1_b, w_ple, w_ple_gate, ln2_g, ln2_b,
                  y_ref, ca_ref, s_ref, cb_ref,
                  hbf_s, q_s, k_s, v_s, beta_s, g_s, egl_s, u_s, w_s, qd_s, kd_s, qk_s, o_s):
    tt, ch, seg, carry = cfg
    nseg = tt // seg
    nch = tt // ch
    nblk = ch // INV_BLOCK
    t_idx = pl.program_id(1)

    def _seed():
        ca_ref[...] = hista_ref[...]
        cb_ref[...] = histb_ref[...]
        s_ref[...] = sin_ref[...]

    if carry:
        pl.when(t_idx == 0)(_seed)
    else:
        _seed()

    def conv_tile(pre, hist_ref, col, w_ref, ntaps):
        outs = []
        for i in range(nseg):
            part = pre[i * seg:(i + 1) * seg]
            outs.append(_causal_conv(part, hist_ref[i, :, col], w_ref.at[:, col], ntaps))
            hist_ref[i, :, col] = part[seg - SUBLANES:]
        return outs[0] if nseg == 1 else jnp.concatenate(outs, axis=0)

    h = _layer_norm(x_ref[0], ln_in_g[...], ln_in_b[...])
    hbf_s[...] = h.astype(BF16)

    for grp, dst in enumerate((q_s, k_s, v_s)):
        col = slice(grp * KEY_DIM, (grp + 1) * KEY_DIM)
        pre = jnp.dot(hbf_s[...], w_qkv[:, col], preferred_element_type=F32)
        act = _silu(conv_tile(pre, ca_ref, col, w_conv_a, CONV_A))
        if grp == 2:
            dst[...] = act
        else:
            scale = HEAD_D ** -0.5 if grp == 0 else 1.0
            for hd in range(N_HEADS):
                hc = slice(hd * HEAD_D, (hd + 1) * HEAD_D)
                xh = act[:, hc]
                ss = jnp.sum(xh * xh, axis=-1, keepdims=True)
                dst[:, hc] = xh * (lax.rsqrt(ss + L2_EPS) * scale)

    beta_s[...] = _sigmoid(jnp.dot(hbf_s[...], w_beta[...], preferred_element_type=F32))
    zdec = jnp.dot(hbf_s[...], w_dec[...], preferred_element_type=F32)
    g_s[...] = -jnp.exp(a_log[...]) * _softplus(zdec + dt_bias[...])

    ri = lax.broadcasted_iota(jnp.int32, (ch, ch), 0)
    ci = lax.broadcasted_iota(jnp.int32, (ch, ch), 1)
    causal = ri >= ci
    strict = ri > ci
    diag_mask = (ri // INV_BLOCK) == (ci // INV_BLOCK)
    ltri = causal.astype(F32)

    def phase1(c, carry_):
        r0 = pl.multiple_of(c * ch, ch)
        rows = pl.ds(r0, ch)
        gc = jnp.dot(ltri, g_s[rows, :], preferred_element_type=F32,
                     precision=lax.Precision.HIGHEST)
        gc_t = gc.T
        beta_c = beta_s[rows, :]
        glast = gc[ch - 1:ch, :]
        egc = jnp.exp(gc)
        ekd = jnp.exp(glast - gc)
        egl_s[pl.ds(c, 1), :] = jnp.exp(glast)
        for hd in range(N_HEADS):
            hc = slice(hd * HEAD_D, (hd + 1) * HEAD_D)
            kh = k_s[rows, hc]
            qh = q_s[rows, hc]
            vh = v_s[rows, hc]
            bcol = beta_c[:, hd:hd + 1]
            ecol = egc[:, hd:hd + 1]
            diff = gc[:, hd:hd + 1] - gc_t[hd:hd + 1, :]
            dmat = jnp.where(causal, jnp.exp(jnp.where(causal, diff, 0.0)), 0.0)
            kb = kh * bcol
            kq = jnp.concatenate([kb, qh], axis=0).astype(BF16)
            kkqk = lax.dot_general(kq, kh.astype(BF16), (((1,), (1,)), ((), ())),
                                   preferred_element_type=F32)
            a_mat = jnp.where(strict, kkqk[:ch] * dmat, 0.0)
            qk_s[hd, rows, :] = (kkqk[ch:] * dmat).astype(BF16)
            t_x = _inv_unit_lower(a_mat, diag_mask, nblk)
            rhs = jnp.concatenate([vh * bcol, kb * ecol], axis=1)
            uw = rhs + _mm(t_x, rhs)
            u_s[rows, hc] = uw[:, :HEAD_D]
            w_s[rows, hc] = uw[:, HEAD_D:].astype(BF16)
            qd_s[rows, hc] = (qh * ecol).astype(BF16)
            kd_s[rows, hc] = kh * ekd[:, hd:hd + 1]
        return carry_

    lax.fori_loop(0, nch, phase1, 0)

    def phase2(c, carry_):
        r0 = pl.multiple_of(c * ch, ch)
        rows = pl.ds(r0, ch)
        si = 0 if carry else c
        egl = egl_s[pl.ds(c, 1), :]
        for hd in range(N_HEADS):
            hc = slice(hd * HEAD_D, (hd + 1) * HEAD_D)
            s_old = s_ref[si, hd]
            wq = jnp.concatenate([w_s[rows, hc], qd_s[rows, hc]], axis=0)
            ws = jnp.dot(wq, s_old.astype(BF16), preferred_element_type=F32)
            v_new = (u_s[rows, hc] - ws[:ch]).astype(BF16)
            o_s[rows, hc] = ws[ch:] + jnp.dot(qk_s[hd, rows, :], v_new, preferred_element_type=F32)
            kd_t = kd_s[rows, hc].T.astype(BF16)
            s_ref[si, hd] = s_old * egl[:, hd:hd + 1] + jnp.dot(kd_t, v_new,
                                                                 preferred_element_type=F32)
        return carry_

    lax.fori_loop(0, nch, phase2, 0)

    za = jnp.dot(hbf_s[...], w_za[...], preferred_element_type=F32)
    ya = []
    for hd in range(N_HEADS):
        hc = slice(hd * HEAD_D, (hd + 1) * HEAD_D)
        oh = o_s[:, hc]
        ms = jnp.mean(oh * oh, axis=-1, keepdims=True)
        ya.append((oh * lax.rsqrt(ms + RMS_EPS) * norm_a_g[...] * _silu(za[:, hc])).astype(BF16))
    pa = jnp.dot(jnp.concatenate(ya, axis=1), w_proj_a[...], preferred_element_type=F32)

    def zb_cols(k):
        return jnp.dot(hbf_s[...], w_b[:, k * WIDTH_B:(k + 1) * WIDTH_B],
                       preferred_element_type=F32)

    cu = zb_cols(1) * zb_cols(2)
    conv_b = conv_tile(cu, cb_ref, slice(0, WIDTH_B), w_conv_b, CONV_B)
    yb = zb_cols(0) * conv_b * _silu(zb_cols(3))
    pb = jnp.dot(yb.astype(BF16), w_proj_b[...], preferred_element_type=F32)

    gate_a = _sigmoid(jnp.dot(hbf_s[...], w_gate[:, :D_MODEL], preferred_element_type=F32))
    gate_b = _sigmoid(jnp.dot(hbf_s[...], w_gate[:, D_MODEL:], preferred_element_type=F32))
    merged = gate_a * pa + gate_b * pb
    h_in = _layer_norm(x_ref[0], ln_in_g[...], ln_in_b[...])
    h1 = _layer_norm(ALPHA * h_in + jnp.dot(merged.astype(BF16), w_out[...],
                                            preferred_element_type=F32),
                     ln1_g[...], ln1_b[...])
    h1_bf = h1.astype(BF16)
    ple = _sigmoid(jnp.dot(h1_bf, w_ple_gate[...], preferred_element_type=F32)) * jnp.dot(
        p_ref[0].astype(BF16), w_ple[...], preferred_element_type=F32)
    y_ref[0] = _layer_norm(ALPHA * h1 + ple, ln2_g[...], ln2_b[...])


def _encode(x, p, hist_a, s0, hist_b, weights, *, tt, ch, seg):
    nb, t_len, _ = x.shape
    nt = t_len // tt
    nseg = tt // seg
    carry = nseg == 1
    assert t_len % tt == 0 and tt % seg == 0 and seg % ch == 0 and (carry or nt == 1)
    assert carry or ch == seg
    nch = tt // ch
    cfg = (tt, ch, seg, carry)

    def tile_spec(width):
        return pl.BlockSpec((1, tt, width), lambda b, t: (b, t, 0))

    def state_spec(shape):
        nd = len(shape)
        return pl.BlockSpec((nseg,) + tuple(shape[1:]), lambda b, t: (b,) + (0,) * (nd - 1),
                            pipeline_mode=pl.Buffered(1))

    def const_spec(arr):
        nd = arr.ndim
        return pl.BlockSpec(arr.shape, lambda b, t: (0,) * nd, pipeline_mode=pl.Buffered(1))

    in_specs = [tile_spec(D_MODEL), tile_spec(P_DIM), state_spec(hist_a.shape),
                state_spec(s0.shape), state_spec(hist_b.shape)]
    in_specs += [const_spec(w) for w in weights]
    out_shape = (jax.ShapeDtypeStruct(x.shape, F32),
                 jax.ShapeDtypeStruct(hist_a.shape, F32),
                 jax.ShapeDtypeStruct(s0.shape, F32),
                 jax.ShapeDtypeStruct(hist_b.shape, F32))
    out_specs = (tile_spec(D_MODEL), state_spec(hist_a.shape), state_spec(s0.shape),
                 state_spec(hist_b.shape))
    scratch = [
        pltpu.VMEM((tt, D_MODEL), BF16),
        pltpu.VMEM((tt, KEY_DIM), F32),
        pltpu.VMEM((tt, KEY_DIM), F32),
        pltpu.VMEM((tt, KEY_DIM), F32),
        pltpu.VMEM((tt, LANES), F32),
        pltpu.VMEM((tt, LANES), F32),
        pltpu.VMEM((max(nch, SUBLANES), LANES), F32),
        pltpu.VMEM((tt, KEY_DIM), F32),
        pltpu.VMEM((tt, KEY_DIM), BF16),
        pltpu.VMEM((tt, KEY_DIM), BF16),
        pltpu.VMEM((tt, KEY_DIM), F32),
        pltpu.VMEM((N_HEADS, tt, ch), BF16),
        pltpu.VMEM((tt, KEY_DIM), F32),
    ]
    return pl.pallas_call(
        functools.partial(_layer_kernel, cfg),
        grid=(nb, nt),
        in_specs=in_specs,
        out_specs=out_specs,
        out_shape=out_shape,
        scratch_shapes=scratch,
        compiler_params=pltpu.CompilerParams(
            dimension_semantics=("arbitrary", "arbitrary"),
            vmem_limit_bytes=VMEM_LIMIT_BYTES),
        name=f"gdn_shortconv_layer_t{tt}_c{ch}",
    )(x, p, hist_a, s0, hist_b, *weights)


def _pad_rows_front(a, rows):
    pad = [(0, 0)] * a.ndim
    pad[-2] = (rows - a.shape[-2], 0)
    return jnp.pad(a, pad)


def _pad_lanes(a):
    pad = [(0, 0)] * a.ndim
    pad[-1] = (0, LANES - a.shape[-1])
    return jnp.pad(a, pad)


def kernel(x_prompt, x_sample, state_conv_a, state_gdn, state_conv_b, p_prompt, p_sample, ln_in_g, ln_in_b, w_in, w_conv_a, a_log, dt_bias, norm_a_g, w_conv_b, w_proj_a, w_proj_b, w_out, ln1_g, ln1_b, w_ple, w_ple_gate, ln2_g, ln2_b):
    assert w_in.shape[0] == DEPTH == 1
    w = w_in[0]
    row = lambda v: v.reshape(1, -1).astype(F32)
    weights = (
        row(ln_in_g), row(ln_in_b),
        w[:, :QKV_DIM].astype(BF16),
        w[:, OFF_ZA:OFF_BETA].astype(BF16),
        _pad_lanes(w[:, OFF_BETA:OFF_DECAY]).astype(BF16),
        _pad_lanes(w[:, OFF_DECAY:OFF_BB]).astype(BF16),
        w[:, OFF_BB:OFF_GATE].astype(BF16),
        w[:, OFF_GATE:].astype(BF16),
        w_conv_a[0].astype(F32),
        _pad_lanes(row(a_log[0])), _pad_lanes(row(dt_bias[0])), row(norm_a_g[0]),
        w_conv_b[0].astype(F32),
        w_proj_a[0].astype(BF16), w_proj_b[0].astype(BF16), w_out[0].astype(BF16),
        row(ln1_g[0]), row(ln1_b[0]),
        w_ple[0].astype(BF16), w_ple_gate[0].astype(BF16),
        row(ln2_g[0]), row(ln2_b[0]),
    )

    bp, seq, _ = x_prompt.shape
    y_p, ca_p, s_p, cb_p = _encode(
        x_prompt, p_prompt[0],
        jnp.zeros((bp, SUBLANES, QKV_DIM), F32),
        jnp.zeros((bp, N_HEADS, HEAD_D, HEAD_D), F32),
        jnp.zeros((bp, SUBLANES, WIDTH_B), F32),
        weights, tt=PROMPT_TILE, ch=PROMPT_CHUNK, seg=PROMPT_TILE)

    bs, ts, _ = x_sample.shape
    per_tile = PROMPT_TILE // ts
    y_s, ca_s, s_s, cb_s = _encode(
        x_sample.reshape(bs // per_tile, per_tile * ts, D_MODEL),
        p_sample[0].reshape(bs // per_tile, per_tile * ts, P_DIM),
        _pad_rows_front(state_conv_a[0], SUBLANES),
        state_gdn[0].astype(F32),
        _pad_rows_front(state_conv_b[0], SUBLANES),
        weights, tt=per_tile * ts, ch=ts, seg=ts)

    na, nb = CONV_A - 1, CONV_B - 1
    return (y_p, y_s.reshape(bs, ts, D_MODEL),
            ca_p[None, :, SUBLANES - na:], s_p[None], cb_p[None, :, SUBLANES - nb:],
            ca_s[None, :, SUBLANES - na:], s_s[None].astype(state_gdn.dtype),
            cb_s[None, :, SUBLANES - nb:])
```

```python
import functools

import jax
import jax.numpy as jnp
from jax import lax
from jax.experimental import pallas as pl
from jax.experimental.pallas import tpu as pltpu

D_MODEL = 1024
N_HEADS = 8
HEAD_D = 128
KEY_DIM = N_HEADS * HEAD_D
QKV_DIM = 3 * KEY_DIM
WIDTH_B = D_MODEL
P_DIM = 256
CONV_A = 4
CONV_B = 3
PROMPT_CHUNK = 64
DEPTH = 1
ALPHA = (2 * DEPTH) ** 0.25
LN_EPS = 1e-5
RMS_EPS = 1e-6
L2_EPS = 1e-6

OFF_ZA = QKV_DIM
OFF_BETA = OFF_ZA + KEY_DIM
OFF_DECAY = OFF_BETA + N_HEADS
OFF_BB = OFF_DECAY + N_HEADS
OFF_CB = OFF_BB + WIDTH_B
OFF_UB = OFF_CB + WIDTH_B
OFF_ZB = OFF_UB + WIDTH_B
OFF_GATE = OFF_ZB + WIDTH_B

SUBLANES = 8
LANES = 128
INV_BLOCK = 16
PROMPT_TILE = 256
P1_CHUNKS = 2
VMEM_LIMIT_BYTES = 60 * 1024 * 1024

F32 = jnp.float32
BF16 = jnp.bfloat16


def _sigmoid(x):
    return 1.0 / (1.0 + jnp.exp(-x))


def _silu(x):
    return x * _sigmoid(x)


def _softplus(x):
    return jnp.maximum(x, 0.0) + jnp.log(1.0 + jnp.exp(-jnp.abs(x)))


def _layer_norm(x, g, b):
    mu = jnp.mean(x, axis=-1, keepdims=True)
    xc = x - mu
    var = jnp.mean(xc * xc, axis=-1, keepdims=True)
    return xc * lax.rsqrt(var + LN_EPS) * g + b


def _mm(a, b):
    return jnp.dot(a.astype(BF16), b.astype(BF16), preferred_element_type=F32)


def _causal_conv(seg, hist8, w_ref, ntaps):
    row8 = lax.broadcasted_iota(jnp.int32, (SUBLANES, 1), 0)
    acc = seg * w_ref[ntaps - 1:ntaps, :]
    for s in range(1, ntaps):
        rolled = pltpu.roll(seg, s, 0)
        head = jnp.where(row8 < s, pltpu.roll(hist8, s, 0), rolled[0:SUBLANES])
        shifted = jnp.concatenate([head, rolled[SUBLANES:]], axis=0)
        acc = acc + shifted * w_ref[ntaps - 1 - s:ntaps - s, :]
    return acc


def _mm_each(xs, ys):
    return [_mm(x, y) for x, y in zip(xs, ys)]


def _inv_unit_lower(a_list, diag_mask, nblk):
    d = [jnp.where(diag_mask, a, 0.0) for a in a_list]
    low = [a - x for a, x in zip(a_list, d)]
    d2 = _mm_each(d, d)
    d4 = _mm_each(d2, d2)
    dd2 = _mm_each(d, d2)
    d8 = _mm_each(d4, d4)
    x1 = [b - a - c for a, b, c in zip(d, d2, dd2)]
    d4d8 = _mm_each(d4, d8)
    x2 = [a + b + c for a, b, c in zip(d4, d8, d4d8)]
    x1x2 = _mm_each(x1, x2)
    xd = [a + b + c for a, b, c in zip(x1, x2, x1x2)]
    xdl = _mm_each(xd, low)
    n = [a + b for a, b in zip(low, xdl)]
    if nblk == 2:
        xq = [-a for a in n]
    elif nblk == 4:
        n2 = _mm_each(n, n)
        nn2 = _mm_each(n, n2)
        xq = [b - a - c for a, b, c in zip(n, n2, nn2)]
    else:
        raise NotImplementedError(nblk)
    xqxd = _mm_each(xq, xd)
    return [a + b + c for a, b, c in zip(xq, xd, xqxd)]


def _layer_kernel(cfg,
                  x_ref, p_ref, hista_ref, sin_ref, histb_ref,
                  ln_in_g, ln_in_b, w_qkv, w_za, w_beta, w_dec, w_b, w_gate,
                  w_conv_a, a_log, dt_bias, norm_a_g, w_conv_b,
                  w_proj_a, w_proj_b, w_out, ln1_g, ln1_b, w_ple, w_ple_gate, ln2_g, ln2_b,
                  y_ref, ca_ref, s_ref, cb_ref,
                  hbf_s, q_s, k_s, v_s, beta_s, g_s, egl_s, u_s, w_s, qd_s, kd_s, qk_s, o_s):
    tt, ch, seg, carry = cfg
    nseg = tt // seg
    nch = tt // ch
    nblk = ch // INV_BLOCK
    t_idx = pl.program_id(1)

    def _seed():
        ca_ref[...] = hista_ref[...]
        cb_ref[...] = histb_ref[...]
        s_ref[...] = sin_ref[...]

    if carry:
        pl.when(t_idx == 0)(_seed)
    else:
        _seed()

    def conv_tile(pre, hist_ref, col, w_ref, ntaps):
        outs = []
        for i in range(nseg):
            part = pre[i * seg:(i + 1) * seg]
            outs.append(_causal_conv(part, hist_ref[i, :, col], w_ref.at[:, col], ntaps))
            hist_ref[i, :, col] = part[seg - SUBLANES:]
        return outs[0] if nseg == 1 else jnp.concatenate(outs, axis=0)

    h = _layer_norm(x_ref[0], ln_in_g[...], ln_in_b[...])
    hbf_s[...] = h.astype(BF16)

    for grp, dst in enumerate((q_s, k_s, v_s)):
        col = slice(grp * KEY_DIM, (grp + 1) * KEY_DIM)
        pre = jnp.dot(hbf_s[...], w_qkv[:, col], preferred_element_type=F32)
        act = _silu(conv_tile(pre, ca_ref, col, w_conv_a, CONV_A))
        if grp == 2:
            dst[...] = act
        else:
            scale = HEAD_D ** -0.5 if grp == 0 else 1.0
            for hd in range(N_HEADS):
                hc = slice(hd * HEAD_D, (hd + 1) * HEAD_D)
                xh = act[:, hc]
                ss = jnp.sum(xh * xh, axis=-1, keepdims=True)
                dst[:, hc] = xh * (lax.rsqrt(ss + L2_EPS) * scale)

    beta_s[...] = _sigmoid(jnp.dot(hbf_s[...], w_beta[...], preferred_element_type=F32))
    zdec = jnp.dot(hbf_s[...], w_dec[...], preferred_element_type=F32)
    g_s[...] = -jnp.exp(a_log[...]) * _softplus(zdec + dt_bias[...])

    ri = lax.broadcasted_iota(jnp.int32, (ch, ch), 0)
    ci = lax.broadcasted_iota(jnp.int32, (ch, ch), 1)
    causal = ri >= ci
    strict = ri > ci
    diag_mask = (ri // INV_BLOCK) == (ci // INV_BLOCK)
    ltri = causal.astype(F32)

    heads = range(N_HEADS)
    hcols = [slice(hd * HEAD_D, (hd + 1) * HEAD_D) for hd in heads]

    def phase1(it, carry_):
        units = []
        pre = {}
        for j in range(P1_CHUNKS):
            c = it * P1_CHUNKS + j
            rows = pl.ds(pl.multiple_of(c * ch, ch), ch)
            g_c = g_s[rows, :]
            beta_c = beta_s[rows, :]
            for hd in heads:
                units.append((j, rows, hd))
                pre[j, hd] = (k_s[rows, hcols[hd]], q_s[rows, hcols[hd]], v_s[rows, hcols[hd]])
            gc = jnp.dot(ltri, g_c, preferred_element_type=F32,
                         precision=lax.Precision.HIGHEST)
            glast = gc[ch - 1:ch, :]
            pre[j] = (c, gc, gc.T, beta_c, jnp.exp(gc), jnp.exp(glast - gc), jnp.exp(glast))

        kb, rhs, dmat, kq, kbf = [], [], [], [], []
        for j, rows, hd in units:
            kh, qh, vh = pre[j, hd]
            _, gc, gc_t, beta_c, egc, _, _ = pre[j]
            bcol = beta_c[:, hd:hd + 1]
            diff = gc[:, hd:hd + 1] - gc_t[hd:hd + 1, :]
            dmat.append(jnp.where(causal, jnp.exp(jnp.where(causal, diff, 0.0)), 0.0))
            kb_u = kh * bcol
            kb.append(kb_u)
            kq.append(jnp.concatenate([kb_u, qh], axis=0).astype(BF16))
            kbf.append(kh.astype(BF16))
            rhs.append(jnp.concatenate([vh * bcol, kb_u * egc[:, hd:hd + 1]], axis=1))
        kkqk = [lax.dot_general(a, b, (((1,), (1,)), ((), ())), preferred_element_type=F32)
                for a, b in zip(kq, kbf)]
        a_mat = [jnp.where(strict, x[:ch] * d, 0.0) for x, d in zip(kkqk, dmat)]
        t_x = _inv_unit_lower(a_mat, diag_mask, nblk)
        uw = [r + x for r, x in zip(rhs, _mm_each(t_x, rhs))]

        for i, (j, rows, hd) in enumerate(units):
            kh, qh, _ = pre[j, hd]
            _, _, _, _, egc, ekd, _ = pre[j]
            qk_s[hd, rows, :] = (kkqk[i][ch:] * dmat[i]).astype(BF16)
            u_s[rows, hcols[hd]] = uw[i][:, :HEAD_D]
            w_s[rows, hcols[hd]] = uw[i][:, HEAD_D:].astype(BF16)
            qd_s[rows, hcols[hd]] = (qh * egc[:, hd:hd + 1]).astype(BF16)
            kd_s[rows, hcols[hd]] = kh * ekd[:, hd:hd + 1]
        for j in range(P1_CHUNKS):
            egl_s[pl.ds(pre[j][0], 1), :] = pre[j][6]
        return carry_

    lax.fori_loop(0, nch // P1_CHUNKS, phase1, 0)

    def phase2(c, carry_):
        rows = pl.ds(pl.multiple_of(c * ch, ch), ch)
        si = 0 if carry else c
        egl = egl_s[pl.ds(c, 1), :]
        s_old = [s_ref[si, hd] for hd in heads]
        wq = [jnp.concatenate([w_s[rows, hcols[hd]], qd_s[rows, hcols[hd]]], axis=0)
              for hd in heads]
        u = [u_s[rows, hcols[hd]] for hd in heads]
        qk = [qk_s[hd, rows, :] for hd in heads]
        kd_t = [kd_s[rows, hcols[hd]].T.astype(BF16) for hd in heads]
        ws = [jnp.dot(a, s.astype(BF16), preferred_element_type=F32) for a, s in zip(wq, s_old)]
        v_new = [(a - b[:ch]).astype(BF16) for a, b in zip(u, ws)]
        o = [b[ch:] + jnp.dot(a, v, preferred_element_type=F32)
             for a, b, v in zip(qk, ws, v_new)]
        s_new = [s * egl[:, hd:hd + 1] + jnp.dot(a, v, preferred_element_type=F32)
                 for hd, (s, a, v) in enumerate(zip(s_old, kd_t, v_new))]
        for hd in heads:
            o_s[rows, hcols[hd]] = o[hd]
            s_ref[si, hd] = s_new[hd]
        return carry_

    lax.fori_loop(0, nch, phase2, 0)

    za = jnp.dot(hbf_s[...], w_za[...], preferred_element_type=F32)
    ya = []
    for hd in range(N_HEADS):
        hc = slice(hd * HEAD_D, (hd + 1) * HEAD_D)
        oh = o_s[:, hc]
        ms = jnp.mean(oh * oh, axis=-1, keepdims=True)
        ya.append((oh * lax.rsqrt(ms + RMS_EPS) * norm_a_g[...] * _silu(za[:, hc])).astype(BF16))
    pa = jnp.dot(jnp.concatenate(ya, axis=1), w_proj_a[...], preferred_element_type=F32)

    def zb_cols(k):
        return jnp.dot(hbf_s[...], w_b[:, k * WIDTH_B:(k + 1) * WIDTH_B],
                       preferred_element_type=F32)

    cu = zb_cols(1) * zb_cols(2)
    conv_b = conv_tile(cu, cb_ref, slice(0, WIDTH_B), w_conv_b, CONV_B)
    yb = zb_cols(0) * conv_b * _silu(zb_cols(3))
    pb = jnp.dot(yb.astype(BF16), w_proj_b[...], preferred_element_type=F32)

    gate_a = _sigmoid(jnp.dot(hbf_s[...], w_gate[:, :D_MODEL], preferred_element_type=F32))
    gate_b = _sigmoid(jnp.dot(hbf_s[...], w_gate[:, D_MODEL:], preferred_element_type=F32))
    merged = gate_a * pa + gate_b * pb
    h_in = _layer_norm(x_ref[0], ln_in_g[...], ln_in_b[...])
    h1 = _layer_norm(ALPHA * h_in + jnp.dot(merged.astype(BF16), w_out[...],
                                            preferred_element_type=F32),
                     ln1_g[...], ln1_b[...])
    h1_bf = h1.astype(BF16)
    ple = _sigmoid(jnp.dot(h1_bf, w_ple_gate[...], preferred_element_type=F32)) * jnp.dot(
        p_ref[0].astype(BF16), w_ple[...], preferred_element_type=F32)
    y_ref[0] = _layer_norm(ALPHA * h1 + ple, ln2_g[...], ln2_b[...])


def _encode(x, p, hist_a, s0, hist_b, weights, *, tt, ch, seg):
    nb, t_len, _ = x.shape
    nt = t_len // tt
    nseg = tt // seg
    carry = nseg == 1
    assert t_len % tt == 0 and tt % seg == 0 and seg % ch == 0 and (carry or nt == 1)
    assert carry or ch == seg
    nch = tt // ch
    cfg = (tt, ch, seg, carry)

    def tile_spec(width):
        return pl.BlockSpec((1, tt, width), lambda b, t: (b, t, 0))

    def state_spec(shape):
        nd = len(shape)
        return pl.BlockSpec((nseg,) + tuple(shape[1:]), lambda b, t: (b,) + (0,) * (nd - 1),
                            pipeline_mode=pl.Buffered(1))

    def const_spec(arr):
        nd = arr.ndim
        return pl.BlockSpec(arr.shape, lambda b, t: (0,) * nd, pipeline_mode=pl.Buffered(1))

    in_specs = [tile_spec(D_MODEL), tile_spec(P_DIM), state_spec(hist_a.shape),
                state_spec(s0.shape), state_spec(hist_b.shape)]
    in_specs += [const_spec(w) for w in weights]
    out_shape = (jax.ShapeDtypeStruct(x.shape, F32),
                 jax.ShapeDtypeStruct(hist_a.shape, F32),
                 jax.ShapeDtypeStruct(s0.shape, F32),
                 jax.ShapeDtypeStruct(hist_b.shape, F32))
    out_specs = (tile_spec(D_MODEL), state_spec(hist_a.shape), state_spec(s0.shape),
                 state_spec(hist_b.shape))
    scratch = [
        pltpu.VMEM((tt, D_MODEL), BF16),
        pltpu.VMEM((tt, KEY_DIM), F32),
        pltpu.VMEM((tt, KEY_DIM), F32),
        pltpu.VMEM((tt, KEY_DIM), F32),
        pltpu.VMEM((tt, LANES), F32),
        pltpu.VMEM((tt, LANES), F32),
        pltpu.VMEM((max(nch, SUBLANES), LANES), F32),
        pltpu.VMEM((tt, KEY_DIM), F32),
        pltpu.VMEM((tt, KEY_DIM), BF16),
        pltpu.VMEM((tt, KEY_DIM), BF16),
        pltpu.VMEM((tt, KEY_DIM), F32),
        pltpu.VMEM((N_HEADS, tt, ch), BF16),
        pltpu.VMEM((tt, KEY_DIM), F32),
    ]
    return pl.pallas_call(
        functools.partial(_layer_kernel, cfg),
        grid=(nb, nt),
        in_specs=in_specs,
        out_specs=out_specs,
        out_shape=out_shape,
        scratch_shapes=scratch,
        compiler_params=pltpu.CompilerParams(
            dimension_semantics=("arbitrary", "arbitrary"),
            vmem_limit_bytes=VMEM_LIMIT_BYTES),
        name=f"gdn_shortconv_layer_t{tt}_c{ch}",
    )(x, p, hist_a, s0, hist_b, *weights)


def _pad_rows_front(a, rows):
    pad = [(0, 0)] * a.ndim
    pad[-2] = (rows - a.shape[-2], 0)
    return jnp.pad(a, pad)


def _pad_lanes(a):
    pad = [(0, 0)] * a.ndim
    pad[-1] = (0, LANES - a.shape[-1])
    return jnp.pad(a, pad)


def kernel(x_prompt, x_sample, state_conv_a, state_gdn, state_conv_b, p_prompt, p_sample, ln_in_g, ln_in_b, w_in, w_conv_a, a_log, dt_bias, norm_a_g, w_conv_b, w_proj_a, w_proj_b, w_out, ln1_g, ln1_b, w_ple, w_ple_gate, ln2_g, ln2_b):
    assert w_in.shape[0] == DEPTH == 1
    w = w_in[0]
    row = lambda v: v.reshape(1, -1).astype(F32)
    weights = (
        row(ln_in_g), row(ln_in_b),
        w[:, :QKV_DIM].astype(BF16),
        w[:, OFF_ZA:OFF_BETA].astype(BF16),
        _pad_lanes(w[:, OFF_BETA:OFF_DECAY]).astype(BF16),
        _pad_lanes(w[:, OFF_DECAY:OFF_BB]).astype(BF16),
        w[:, OFF_BB:OFF_GATE].astype(BF16),
        w[:, OFF_GATE:].astype(BF16),
        w_conv_a[0].astype(F32),
        _pad_lanes(row(a_log[0])), _pad_lanes(row(dt_bias[0])), row(norm_a_g[0]),
        w_conv_b[0].astype(F32),
        w_proj_a[0].astype(BF16), w_proj_b[0].astype(BF16), w_out[0].astype(BF16),
        row(ln1_g[0]), row(ln1_b[0]),
        w_ple[0].astype(BF16), w_ple_gate[0].astype(BF16),
        row(ln2_g[0]), row(ln2_b[0]),
    )

    bp, seq, _ = x_prompt.shape
    y_p, ca_p, s_p, cb_p = _encode(
        x_prompt, p_prompt[0],
        jnp.zeros((bp, SUBLANES, QKV_DIM), F32),
        jnp.zeros((bp, N_HEADS, HEAD_D, HEAD_D), F32),
        jnp.zeros((bp, SUBLANES, WIDTH_B), F32),
        weights, tt=PROMPT_TILE, ch=PROMPT_CHUNK, seg=PROMPT_TILE)

    bs, ts, _ = x_sample.shape
    per_tile = PROMPT_TILE // ts
    y_s, ca_s, s_s, cb_s = _encode(
        x_sample.reshape(bs // per_tile, per_tile * ts, D_MODEL),
        p_sample[0].reshape(bs // per_tile, per_tile * ts, P_DIM),
        _pad_rows_front(state_conv_a[0], SUBLANES),
        state_gdn[0].astype(F32),
        _pad_rows_front(state_conv_b[0], SUBLANES),
        weights, tt=per_tile * ts, ch=ts, seg=ts)

    na, nb = CONV_A - 1, CONV_B - 1
    return (y_p, y_s.reshape(bs, ts, D_MODEL),
            ca_p[None, :, SUBLANES - na:], s_p[None], cb_p[None, :, SUBLANES - nb:],
            ca_s[None, :, SUBLANES - na:], s_s[None].astype(state_gdn.dtype),
            cb_s[None, :, SUBLANES - nb:])
```

```python
import functools

import jax
import jax.numpy as jnp
from jax import lax
from jax.experimental import pallas as pl
from jax.experimental.pallas import tpu as pltpu

D_MODEL = 1024
N_HEADS = 8
HEAD_D = 128
KEY_DIM = N_HEADS * HEAD_D
QKV_DIM = 3 * KEY_DIM
WIDTH_B = D_MODEL
P_DIM = 256
CONV_A = 4
CONV_B = 3
PROMPT_CHUNK = 64
DEPTH = 1
ALPHA = (2 * DEPTH) ** 0.25
LN_EPS = 1e-5
RMS_EPS = 1e-6
L2_EPS = 1e-6

OFF_ZA = QKV_DIM
OFF_BETA = OFF_ZA + KEY_DIM
OFF_DECAY = OFF_BETA + N_HEADS
OFF_BB = OFF_DECAY + N_HEADS
OFF_CB = OFF_BB + WIDTH_B
OFF_UB = OFF_CB + WIDTH_B
OFF_ZB = OFF_UB + WIDTH_B
OFF_GATE = OFF_ZB + WIDTH_B

SUBLANES = 8
LANES = 128
INV_BLOCK = 16
PROMPT_TILE = 512
SAMPLE_TILE_SEQS = 8
P1_CHUNKS = 2
VMEM_LIMIT_BYTES = 63 * 1024 * 1024

F32 = jnp.float32
BF16 = jnp.bfloat16


def _sigmoid(x):
    return 1.0 / (1.0 + jnp.exp(-x))


def _silu(x):
    return x * _sigmoid(x)


def _softplus(x):
    return jnp.maximum(x, 0.0) + jnp.log(1.0 + jnp.exp(-jnp.abs(x)))


def _layer_norm(x, g, b):
    mu = jnp.mean(x, axis=-1, keepdims=True)
    xc = x - mu
    var = jnp.mean(xc * xc, axis=-1, keepdims=True)
    return xc * lax.rsqrt(var + LN_EPS) * g + b


def _mm(a, b):
    return jnp.dot(a.astype(BF16), b.astype(BF16), preferred_element_type=F32)


def _causal_conv(seg, hist8, w_ref, ntaps, buf):
    n = seg.shape[0]
    buf[0:SUBLANES, :] = hist8
    buf[SUBLANES:SUBLANES + n, :] = seg
    acc = seg * w_ref[ntaps - 1:ntaps, :]
    for s in range(1, ntaps):
        acc = acc + buf[SUBLANES - s:SUBLANES - s + n, :] * w_ref[ntaps - 1 - s:ntaps - s, :]
    return acc


def _mm_each(xs, ys):
    return [_mm(x, y) for x, y in zip(xs, ys)]


def _inv_unit_lower(a_list, diag_mask, nblk):
    d = [jnp.where(diag_mask, a, 0.0) for a in a_list]
    low = [a - x for a, x in zip(a_list, d)]
    d2 = _mm_each(d, d)
    d4 = _mm_each(d2, d2)
    dd2 = _mm_each(d, d2)
    d8 = _mm_each(d4, d4)
    x1 = [b - a - c for a, b, c in zip(d, d2, dd2)]
    d4d8 = _mm_each(d4, d8)
    x2 = [a + b + c for a, b, c in zip(d4, d8, d4d8)]
    x1x2 = _mm_each(x1, x2)
    xd = [a + b + c for a, b, c in zip(x1, x2, x1x2)]
    xdl = _mm_each(xd, low)
    n = [a + b for a, b in zip(low, xdl)]
    if nblk == 2:
        xq = [-a for a in n]
    elif nblk == 4:
        n2 = _mm_each(n, n)
        nn2 = _mm_each(n, n2)
        xq = [b - a - c for a, b, c in zip(n, n2, nn2)]
    else:
        raise NotImplementedError(nblk)
    xqxd = _mm_each(xq, xd)
    return [a + b + c for a, b, c in zip(xq, xd, xqxd)]


def _layer_kernel(cfg,
                  x_ref, p_ref, hista_ref, sin_ref, histb_ref,
                  ln_in_g, ln_in_b, w_qkv, w_za, w_beta, w_dec, w_b, w_gate,
                  w_conv_a, a_log, dt_bias, norm_a_g, w_conv_b,
                  w_proj_a, w_proj_b, w_out, ln1_g, ln1_b, w_ple, w_ple_gate, ln2_g, ln2_b,
                  y_ref, ca_ref, s_ref, cb_ref,
                  hbf_s, q_s, k_s, v_s, beta_s, g_s, egl_s, w_s, qd_s, qk_s, cbuf_s):
    u_s, kd_s, o_s = v_s, k_s, q_s
    tt, ch, seg, carry = cfg
    nseg = tt // seg
    nch = tt // ch
    nblk = ch // INV_BLOCK
    t_idx = pl.program_id(1)

    def _seed():
        ca_ref[...] = hista_ref[...]
        cb_ref[...] = histb_ref[...]
        s_ref[...] = sin_ref[...]

    if carry:
        pl.when(t_idx == 0)(_seed)
    else:
        _seed()

    def conv_tile(pre, hist_ref, col, w_ref, ntaps):
        outs = []
        for i in range(nseg):
            part = pre[i * seg:(i + 1) * seg]
            outs.append(_causal_conv(part, hist_ref[i, :, col], w_ref.at[:, col], ntaps,
                                     cbuf_s))
            hist_ref[i, :, col] = part[seg - SUBLANES:]
        return outs[0] if nseg == 1 else jnp.concatenate(outs, axis=0)

    h = _layer_norm(x_ref[0], ln_in_g[...], ln_in_b[...])
    y_ref[0] = h
    hbf_s[...] = h.astype(BF16)

    for grp, dst in enumerate((q_s, k_s, v_s)):
        col = slice(grp * KEY_DIM, (grp + 1) * KEY_DIM)
        pre = jnp.dot(hbf_s[...], w_qkv[:, col], preferred_element_type=F32)
        act = _silu(conv_tile(pre, ca_ref, col, w_conv_a, CONV_A))
        if grp == 2:
            dst[...] = act
        else:
            scale = HEAD_D ** -0.5 if grp == 0 else 1.0
            for hd in range(N_HEADS):
                hc = slice(hd * HEAD_D, (hd + 1) * HEAD_D)
                xh = act[:, hc]
                ss = jnp.sum(xh * xh, axis=-1, keepdims=True)
                dst[:, hc] = xh * (lax.rsqrt(ss + L2_EPS) * scale)

    beta_s[...] = _sigmoid(jnp.dot(hbf_s[...], w_beta[...], preferred_element_type=F32))
    zdec = jnp.dot(hbf_s[...], w_dec[...], preferred_element_type=F32)
    g_s[...] = -jnp.exp(a_log[...]) * _softplus(zdec + dt_bias[...])

    ri = lax.broadcasted_iota(jnp.int32, (ch, ch), 0)
    ci = lax.broadcasted_iota(jnp.int32, (ch, ch), 1)
    causal = ri >= ci
    strict = ri > ci
    diag_mask = (ri // INV_BLOCK) == (ci // INV_BLOCK)
    ltri = causal.astype(F32)

    heads = range(N_HEADS)
    hcols = [slice(hd * HEAD_D, (hd + 1) * HEAD_D) for hd in heads]

    def phase1(it, carry_):
        units = []
        pre = {}
        for j in range(P1_CHUNKS):
            c = it * P1_CHUNKS + j
            rows = pl.ds(pl.multiple_of(c * ch, ch), ch)
            g_c = g_s[rows, :]
            beta_c = beta_s[rows, :]
            for hd in heads:
                units.append((j, rows, hd))
                pre[j, hd] = (k_s[rows, hcols[hd]], q_s[rows, hcols[hd]], v_s[rows, hcols[hd]])
            gc = jnp.dot(ltri, g_c, preferred_element_type=F32,
                         precision=lax.Precision.HIGHEST)
            glast = gc[ch - 1:ch, :]
            pre[j] = (c, gc, gc.T, beta_c, jnp.exp(gc), jnp.exp(glast - gc), jnp.exp(glast))

        kb, rhs, dmat, kq, kbf = [], [], [], [], []
        for j, rows, hd in units:
            kh, qh, vh = pre[j, hd]
            _, gc, gc_t, beta_c, egc, _, _ = pre[j]
            bcol = beta_c[:, hd:hd + 1]
            diff = gc[:, hd:hd + 1] - gc_t[hd:hd + 1, :]
            dmat.append(jnp.where(causal, jnp.exp(jnp.where(causal, diff, 0.0)), 0.0))
            kb_u = kh * bcol
            kb.append(kb_u)
            kq.append(jnp.concatenate([kb_u, qh], axis=0).astype(BF16))
            kbf.append(kh.astype(BF16))
            rhs.append(jnp.concatenate([vh * bcol, kb_u * egc[:, hd:hd + 1]], axis=1))
        kkqk = [lax.dot_general(a, b, (((1,), (1,)), ((), ())), preferred_element_type=F32)
                for a, b in zip(kq, kbf)]
        a_mat = [jnp.where(strict, x[:ch] * d, 0.0) for x, d in zip(kkqk, dmat)]
        t_x = _inv_unit_lower(a_mat, diag_mask, nblk)
        uw = [r + x for r, x in zip(rhs, _mm_each(t_x, rhs))]

        for i, (j, rows, hd) in enumerate(units):
            kh, qh, _ = pre[j, hd]
            _, _, _, _, egc, ekd, _ = pre[j]
            qk_s[hd, rows, :] = (kkqk[i][ch:] * dmat[i]).astype(BF16)
            u_s[rows, hcols[hd]] = uw[i][:, :HEAD_D]
            w_s[rows, hcols[hd]] = uw[i][:, HEAD_D:].astype(BF16)
            qd_s[rows, hcols[hd]] = (qh * egc[:, hd:hd + 1]).astype(BF16)
            kd_s[rows, hcols[hd]] = kh * ekd[:, hd:hd + 1]
        for j in range(P1_CHUNKS):
            egl_s[pl.ds(pre[j][0], 1), :] = pre[j][6]
        return carry_

    lax.fori_loop(0, nch // P1_CHUNKS, phase1, 0)

    def phase2(c, carry_):
        rows = pl.ds(pl.multiple_of(c * ch, ch), ch)
        si = 0 if carry else c
        egl = egl_s[pl.ds(c, 1), :]
        s_old = [s_ref[si, hd] for hd in heads]
        wq = [jnp.concatenate([w_s[rows, hcols[hd]], qd_s[rows, hcols[hd]]], axis=0)
              for hd in heads]
        u = [u_s[rows, hcols[hd]] for hd in heads]
        qk = [qk_s[hd, rows, :] for hd in heads]
        kd_t = [kd_s[rows, hcols[hd]].T.astype(BF16) for hd in heads]
        ws = [jnp.dot(a, s.astype(BF16), preferred_element_type=F32) for a, s in zip(wq, s_old)]
        v_new = [(a - b[:ch]).astype(BF16) for a, b in zip(u, ws)]
        o = [b[ch:] + jnp.dot(a, v, preferred_element_type=F32)
             for a, b, v in zip(qk, ws, v_new)]
        s_new = [s * egl[:, hd:hd + 1] + jnp.dot(a, v, preferred_element_type=F32)
                 for hd, (s, a, v) in enumerate(zip(s_old, kd_t, v_new))]
        for hd in heads:
            o_s[rows, hcols[hd]] = o[hd]
            s_ref[si, hd] = s_new[hd]
        return carry_

    lax.fori_loop(0, nch, phase2, 0)

    za = jnp.dot(hbf_s[...], w_za[...], preferred_element_type=F32)
    ya = []
    for hd in range(N_HEADS):
        hc = slice(hd * HEAD_D, (hd + 1) * HEAD_D)
        oh = o_s[:, hc]
        ms = jnp.mean(oh * oh, axis=-1, keepdims=True)
        ya.append((oh * lax.rsqrt(ms + RMS_EPS) * norm_a_g[...] * _silu(za[:, hc])).astype(BF16))
    pa = jnp.dot(jnp.concatenate(ya, axis=1), w_proj_a[...], preferred_element_type=F32)

    def zb_cols(k):
        return jnp.dot(hbf_s[...], w_b[:, k * WIDTH_B:(k + 1) * WIDTH_B],
                       preferred_element_type=F32)

    cu = zb_cols(1) * zb_cols(2)
    conv_b = conv_tile(cu, cb_ref, slice(0, WIDTH_B), w_conv_b, CONV_B)
    yb = zb_cols(0) * conv_b * _silu(zb_cols(3))
    pb = jnp.dot(yb.astype(BF16), w_proj_b[...], preferred_element_type=F32)

    gate_a = _sigmoid(jnp.dot(hbf_s[...], w_gate[:, :D_MODEL], preferred_element_type=F32))
    gate_b = _sigmoid(jnp.dot(hbf_s[...], w_gate[:, D_MODEL:], preferred_element_type=F32))
    merged = gate_a * pa + gate_b * pb
    h1 = _layer_norm(ALPHA * y_ref[0] + jnp.dot(merged.astype(BF16), w_out[...],
                                            preferred_element_type=F32),
                     ln1_g[...], ln1_b[...])
    h1_bf = h1.astype(BF16)
    ple = _sigmoid(jnp.dot(h1_bf, w_ple_gate[...], preferred_element_type=F32)) * jnp.dot(
        p_ref[0].astype(BF16), w_ple[...], preferred_element_type=F32)
    y_ref[0] = _layer_norm(ALPHA * h1 + ple, ln2_g[...], ln2_b[...])


def _encode(x, p, hist_a, s0, hist_b, weights, *, tt, ch, seg):
    nb, t_len, _ = x.shape
    nt = t_len // tt
    nseg = tt // seg
    carry = nseg == 1
    assert t_len % tt == 0 and tt % seg == 0 and seg % ch == 0 and (carry or nt == 1)
    assert carry or ch == seg
    nch = tt // ch
    cfg = (tt, ch, seg, carry)

    def tile_spec(width):
        return pl.BlockSpec((1, tt, width), lambda b, t: (b, t, 0))

    def state_spec(shape):
        nd = len(shape)
        return pl.BlockSpec((nseg,) + tuple(shape[1:]), lambda b, t: (b,) + (0,) * (nd - 1),
                            pipeline_mode=pl.Buffered(1))

    def const_spec(arr):
        nd = arr.ndim
        return pl.BlockSpec(arr.shape, lambda b, t: (0,) * nd, pipeline_mode=pl.Buffered(1))

    in_specs = [tile_spec(D_MODEL), tile_spec(P_DIM), state_spec(hist_a.shape),
                state_spec(s0.shape), state_spec(hist_b.shape)]
    in_specs += [const_spec(w) for w in weights]
    out_shape = (jax.ShapeDtypeStruct(x.shape, F32),
                 jax.ShapeDtypeStruct(hist_a.shape, F32),
                 jax.ShapeDtypeStruct(s0.shape, F32),
                 jax.ShapeDtypeStruct(hist_b.shape, F32))
    out_specs = (tile_spec(D_MODEL), state_spec(hist_a.shape), state_spec(s0.shape),
                 state_spec(hist_b.shape))
    scratch = [
        pltpu.VMEM((tt, D_MODEL), BF16),
        pltpu.VMEM((tt, KEY_DIM), F32),
        pltpu.VMEM((tt, KEY_DIM), F32),
        pltpu.VMEM((tt, KEY_DIM), F32),
        pltpu.VMEM((tt, LANES), F32),
        pltpu.VMEM((tt, LANES), F32),
        pltpu.VMEM((max(nch, SUBLANES), LANES), F32),
        pltpu.VMEM((tt, KEY_DIM), BF16),
        pltpu.VMEM((tt, KEY_DIM), BF16),
        pltpu.VMEM((N_HEADS, tt, ch), BF16),
        pltpu.VMEM((SUBLANES + seg, KEY_DIM), F32),
    ]
    return pl.pallas_call(
        functools.partial(_layer_kernel, cfg),
        grid=(nb, nt),
        in_specs=in_specs,
        out_specs=out_specs,
        out_shape=out_shape,
        scratch_shapes=scratch,
        compiler_params=pltpu.CompilerParams(
            dimension_semantics=("arbitrary", "arbitrary"),
            vmem_limit_bytes=VMEM_LIMIT_BYTES),
        name=f"gdn_shortconv_layer_t{tt}_c{ch}",
    )(x, p, hist_a, s0, hist_b, *weights)


def _pad_rows_front(a, rows):
    pad = [(0, 0)] * a.ndim
    pad[-2] = (rows - a.shape[-2], 0)
    return jnp.pad(a, pad)


def _pad_lanes(a):
    pad = [(0, 0)] * a.ndim
    pad[-1] = (0, LANES - a.shape[-1])
    return jnp.pad(a, pad)


def kernel(x_prompt, x_sample, state_conv_a, state_gdn, state_conv_b, p_prompt, p_sample, ln_in_g, ln_in_b, w_in, w_conv_a, a_log, dt_bias, norm_a_g, w_conv_b, w_proj_a, w_proj_b, w_out, ln1_g, ln1_b, w_ple, w_ple_gate, ln2_g, ln2_b):
    assert w_in.shape[0] == DEPTH == 1
    w = w_in[0]
    row = lambda v: v.reshape(1, -1).astype(F32)
    weights = (
        row(ln_in_g), row(ln_in_b),
        w[:, :QKV_DIM].astype(BF16),
        w[:, OFF_ZA:OFF_BETA].astype(BF16),
        _pad_lanes(w[:, OFF_BETA:OFF_DECAY]).astype(BF16),
        _pad_lanes(w[:, OFF_DECAY:OFF_BB]).astype(BF16),
        w[:, OFF_BB:OFF_GATE].astype(BF16),
        w[:, OFF_GATE:].astype(BF16),
        w_conv_a[0].astype(F32),
        _pad_lanes(row(a_log[0])), _pad_lanes(row(dt_bias[0])), row(norm_a_g[0]),
        w_conv_b[0].astype(F32),
        w_proj_a[0].astype(BF16), w_proj_b[0].astype(BF16), w_out[0].astype(BF16),
        row(ln1_g[0]), row(ln1_b[0]),
        w_ple[0].astype(BF16), w_ple_gate[0].astype(BF16),
        row(ln2_g[0]), row(ln2_b[0]),
    )

    bp, seq, _ = x_prompt.shape
    y_p, ca_p, s_p, cb_p = _encode(
        x_prompt, p_prompt[0],
        jnp.zeros((bp, SUBLANES, QKV_DIM), F32),
        jnp.zeros((bp, N_HEADS, HEAD_D, HEAD_D), F32),
        jnp.zeros((bp, SUBLANES, WIDTH_B), F32),
        weights, tt=PROMPT_TILE, ch=PROMPT_CHUNK, seg=PROMPT_TILE)

    bs, ts, _ = x_sample.shape
    per_tile = SAMPLE_TILE_SEQS
    y_s, ca_s, s_s, cb_s = _encode(
        x_sample.reshape(bs // per_tile, per_tile * ts, D_MODEL),
        p_sample[0].reshape(bs // per_tile, per_tile * ts, P_DIM),
        _pad_rows_front(state_conv_a[0], SUBLANES),
        state_gdn[0].astype(F32),
        _pad_rows_front(state_conv_b[0], SUBLANES),
        weights, tt=per_tile * ts, ch=ts, seg=ts)

    na, nb = CONV_A - 1, CONV_B - 1
    return (y_p, y_s.reshape(bs, ts, D_MODEL),
            ca_p[None, :, SUBLANES - na:], s_p[None], cb_p[None, :, SUBLANES - nb:],
            ca_s[None, :, SUBLANES - na:], s_s[None].astype(state_gdn.dtype),
            cb_s[None, :, SUBLANES - nb:])
```

```python
import functools

import jax
import jax.numpy as jnp
from jax import lax
from jax.experimental import pallas as pl
from jax.experimental.pallas import tpu as pltpu

D_MODEL = 1024
N_HEADS = 8
HEAD_D = 128
KEY_DIM = N_HEADS * HEAD_D
QKV_DIM = 3 * KEY_DIM
WIDTH_B = D_MODEL
P_DIM = 256
CONV_A = 4
CONV_B = 3
PROMPT_CHUNK = 64
DEPTH = 1
ALPHA = (2 * DEPTH) ** 0.25
LN_EPS = 1e-5
RMS_EPS = 1e-6
L2_EPS = 1e-6

OFF_ZA = QKV_DIM
OFF_BETA = OFF_ZA + KEY_DIM
OFF_DECAY = OFF_BETA + N_HEADS
OFF_BB = OFF_DECAY + N_HEADS
OFF_CB = OFF_BB + WIDTH_B
OFF_UB = OFF_CB + WIDTH_B
OFF_ZB = OFF_UB + WIDTH_B
OFF_GATE = OFF_ZB + WIDTH_B

SUBLANES = 8
LANES = 128
INV_BLOCK = 16
PROMPT_TILE = 512
SAMPLE_TILE_SEQS = 8
P1_CHUNKS = 4
VMEM_LIMIT_BYTES = 127 * 512 * 1024

F32 = jnp.float32
NEG_LOG2_E = -1.4426950408889634
BF16 = jnp.bfloat16


def _sigmoid(x):
    return 1.0 / (1.0 + jnp.exp2(x * NEG_LOG2_E))


def _silu(x):
    return x * _sigmoid(x)


def _softplus(x):
    return jnp.maximum(x, 0.0) + jnp.log(1.0 + jnp.exp(-jnp.abs(x)))


def _layer_norm(x, g, b):
    mu = jnp.mean(x, axis=-1, keepdims=True)
    xc = x - mu
    var = jnp.mean(xc * xc, axis=-1, keepdims=True)
    return xc * lax.rsqrt(var + LN_EPS) * g + b


def _mm(a, b):
    return jnp.dot(a.astype(BF16), b.astype(BF16), preferred_element_type=F32)


def _causal_conv(seg, hist8, w_ref, ntaps, buf):
    n = seg.shape[0]
    buf[0:SUBLANES, :] = hist8
    buf[SUBLANES:SUBLANES + n, :] = seg
    acc = seg * w_ref[ntaps - 1:ntaps, :]
    for s in range(1, ntaps):
        acc = acc + buf[SUBLANES - s:SUBLANES - s + n, :] * w_ref[ntaps - 1 - s:ntaps - s, :]
    return acc


def _block_diag2(y1, y2):
    z = jnp.zeros_like(y1)
    return jnp.concatenate([jnp.concatenate([y1, z], axis=1), jnp.concatenate([z, y2], axis=1)],
                           axis=0)


def _inv_unit_lower(a_list, diag_mask, nblk, mm_each):
    d = [jnp.where(diag_mask, a, 0.0) for a in a_list]
    low = [a - x for a, x in zip(a_list, d)]
    d2 = mm_each(d, d)
    d4 = mm_each(d2, d2)
    dd2 = mm_each(d, d2)
    d8 = mm_each(d4, d4)
    x1 = [b - a - c for a, b, c in zip(d, d2, dd2)]
    d4d8 = mm_each(d4, d8)
    x2 = [a + b + c for a, b, c in zip(d4, d8, d4d8)]
    x1x2 = mm_each(x1, x2)
    xd = [a + b + c for a, b, c in zip(x1, x2, x1x2)]
    xdl = mm_each(xd, low)
    n = [a + b for a, b in zip(low, xdl)]
    if nblk == 2:
        xq = [-a for a in n]
    elif nblk == 4:
        n2 = mm_each(n, n)
        nn2 = mm_each(n, n2)
        xq = [b - a - c for a, b, c in zip(n, n2, nn2)]
    else:
        raise NotImplementedError(nblk)
    xqxd = mm_each(xq, xd)
    return [a + b + c for a, b, c in zip(xq, xd, xqxd)]


def _layer_kernel(cfg,
                  x_ref, p_ref, hista_ref, sin_ref, histb_ref,
                  ln_in_g, ln_in_b, w_qkv, w_za, w_beta, w_dec, w_b, w_gate,
                  w_conv_a, a_log, dt_bias, norm_a_g, w_conv_b,
                  w_proj_a, w_proj_b, w_out, ln1_g, ln1_b, w_ple, w_ple_gate, ln2_g, ln2_b,
                  y_ref, ca_ref, s_ref, cb_ref,
                  hbf_s, q_s, k_s, v_s, beta_s, g_s, egl_s, w_s, qd_s, qk_s, cbuf_s):
    u_s, kd_s, o_s = v_s, k_s, q_s
    tt, ch, seg, carry = cfg
    nseg = tt // seg
    nch = tt // ch
    nblk = ch // INV_BLOCK
    p1_chunks = min(P1_CHUNKS, nch)
    t_idx = pl.program_id(1)

    def _seed():
        ca_ref[...] = hista_ref[...]
        cb_ref[...] = histb_ref[...]
        s_ref[...] = sin_ref[...]

    if carry:
        pl.when(t_idx == 0)(_seed)
    else:
        _seed()

    def conv_tile(pre, hist_ref, col, w_ref, ntaps):
        outs = []
        for i in range(nseg):
            part = pre[i * seg:(i + 1) * seg]
            outs.append(_causal_conv(part, hist_ref[i, :, col], w_ref.at[:, col], ntaps,
                                     cbuf_s))
            hist_ref[i, :, col] = part[seg - SUBLANES:]
        return outs[0] if nseg == 1 else jnp.concatenate(outs, axis=0)

    h = _layer_norm(x_ref[0], ln_in_g[...], ln_in_b[...])
    y_ref[0] = h
    hbf_s[...] = h.astype(BF16)

    for grp, dst in enumerate((q_s, k_s, v_s)):
        col = slice(grp * KEY_DIM, (grp + 1) * KEY_DIM)
        pre = jnp.dot(hbf_s[...], w_qkv[:, col], preferred_element_type=F32)
        act = _silu(conv_tile(pre, ca_ref, col, w_conv_a, CONV_A))
        if grp == 2:
            dst[...] = act
        else:
            scale = HEAD_D ** -0.5 if grp == 0 else 1.0
            for hd in range(N_HEADS):
                hc = slice(hd * HEAD_D, (hd + 1) * HEAD_D)
                xh = act[:, hc]
                ss = jnp.sum(xh * xh, axis=-1, keepdims=True)
                dst[:, hc] = xh * (lax.rsqrt(ss + L2_EPS) * scale)

    beta_s[...] = _sigmoid(jnp.dot(hbf_s[...], w_beta[...], preferred_element_type=F32))
    zdec = jnp.dot(hbf_s[...], w_dec[...], preferred_element_type=F32)
    g_s[...] = -jnp.exp(a_log[...]) * _softplus(zdec + dt_bias[...])

    pw = 2 * ch
    ri = lax.broadcasted_iota(jnp.int32, (ch, pw), 0)
    cn = lax.broadcasted_iota(jnp.int32, (ch, pw), 1)
    cj = cn & (ch - 1)
    hi = cn >= ch
    hi_row = lax.broadcasted_iota(jnp.int32, (1, pw), 1) >= ch
    causal = ri >= cj
    strict = ri > cj
    diag_mask = (ri // INV_BLOCK) == (cj // INV_BLOCK)
    rt = lax.broadcasted_iota(jnp.int32, (ch, ch), 0)
    ct = lax.broadcasted_iota(jnp.int32, (ch, ch), 1)
    ltri = (rt >= ct).astype(F32)

    pairs = range(N_HEADS // 2)
    pcols = [slice(p * 2 * HEAD_D, (p + 1) * 2 * HEAD_D) for p in pairs]
    hcols = [slice(hd * HEAD_D, (hd + 1) * HEAD_D) for hd in range(N_HEADS)]

    def bd_pair(y):
        zero = jnp.zeros_like(y)
        return jnp.concatenate([jnp.where(hi, zero, y), jnp.where(hi, y, zero)], axis=0)

    def bd_wide(y):
        return _block_diag2(y[:, :HEAD_D], y[:, HEAD_D:])

    def mm_pairs(xs, ys):
        return [jnp.dot(x.astype(BF16), bd_pair(y.astype(BF16)), preferred_element_type=F32)
                for x, y in zip(xs, ys)]

    def lanes_of_pair(cols, p):
        return jnp.concatenate([jnp.broadcast_to(cols[:, 2 * p:2 * p + 1], (ch, HEAD_D)),
                                jnp.broadcast_to(cols[:, 2 * p + 1:2 * p + 2], (ch, HEAD_D))],
                               axis=1)

    def phase1(it, carry_):
        units = []
        pre = {}
        for j in range(p1_chunks):
            c = it * p1_chunks + j
            rows = pl.ds(pl.multiple_of(c * ch, ch), ch)
            g_c = g_s[rows, :]
            beta_c = beta_s[rows, :]
            for p in pairs:
                units.append((j, rows, p))
                pre[j, p] = (k_s[rows, pcols[p]], q_s[rows, pcols[p]], v_s[rows, pcols[p]])
            gc = jnp.dot(ltri, g_c, preferred_element_type=F32,
                         precision=lax.Precision.HIGHEST)
            gc_rows = jnp.concatenate([gc, gc], axis=0).T
            glast = gc[ch - 1:ch, :]
            pre[j] = (c, gc, gc_rows, beta_c, jnp.exp(gc), jnp.exp(glast - gc), jnp.exp(glast))

        dmat, lhs, rk, rhs = [], [], [], []
        for j, rows, p in units:
            kp, qp, vp = pre[j, p]
            _, gc, gc_rows, beta_c, egc, _, _ = pre[j]
            g_col = jnp.where(hi, gc[:, 2 * p + 1:2 * p + 2], gc[:, 2 * p:2 * p + 1])
            g_row = jnp.where(hi_row, gc_rows[2 * p + 1:2 * p + 2, :], gc_rows[2 * p:2 * p + 1, :])
            dmat.append(jnp.where(causal, jnp.exp(jnp.where(causal, g_col - g_row, 0.0)), 0.0))
            kb = kp * lanes_of_pair(beta_c, p)
            lhs.append(jnp.concatenate([kb, qp], axis=0).astype(BF16))
            rk.append(bd_wide(kp.astype(BF16)))
            rhs.append((vp * lanes_of_pair(beta_c, p), kb * lanes_of_pair(egc, p)))
        kkqk = [lax.dot_general(a, b, (((1,), (1,)), ((), ())), preferred_element_type=F32)
                for a, b in zip(lhs, rk)]
        a_mat = [jnp.where(strict, x[:ch] * d, 0.0) for x, d in zip(kkqk, dmat)]
        t_x = _inv_unit_lower(a_mat, diag_mask, nblk, mm_pairs)
        duw = [jnp.dot(t.astype(BF16),
                       jnp.concatenate([bd_wide(ru.astype(BF16)), bd_wide(rw.astype(BF16))], axis=1),
                       preferred_element_type=F32)
               for t, (ru, rw) in zip(t_x, rhs)]

        for i, (j, rows, p) in enumerate(units):
            kp, qp = k_s[rows, pcols[p]], q_s[rows, pcols[p]]
            _, _, _, _, egc, ekd, _ = pre[j]
            qk_s[p, rows, :] = (kkqk[i][ch:] * dmat[i]).astype(BF16)
            u_s[rows, pcols[p]] = rhs[i][0] + duw[i][:, :2 * HEAD_D]
            w_s[rows, pcols[p]] = (rhs[i][1] + duw[i][:, 2 * HEAD_D:]).astype(BF16)
            qd_s[rows, pcols[p]] = (qp * lanes_of_pair(egc, p)).astype(BF16)
            kd_s[rows, pcols[p]] = kp * lanes_of_pair(ekd, p)
        for j in range(p1_chunks):
            egl_s[pl.ds(pre[j][0], 1), :] = pre[j][6]
        return carry_

    lax.fori_loop(0, nch // p1_chunks, phase1, 0)

    def phase2(c, carry_):
        rows = pl.ds(pl.multiple_of(c * ch, ch), ch)
        si = 0 if carry else c
        egl = egl_s[pl.ds(c, 1), :]
        s_old = [s_ref[si, hd] for hd in range(N_HEADS)]
        wq = [jnp.concatenate([w_s[rows, pcols[p]], qd_s[rows, pcols[p]]], axis=0)
              for p in pairs]
        u = [u_s[rows, pcols[p]] for p in pairs]
        qk = [qk_s[p, rows, :] for p in pairs]
        kd_t = [kd_s[rows, hcols[hd]].T.astype(BF16) for hd in range(N_HEADS)]
        s_bd = [_block_diag2(s_old[2 * p].astype(BF16), s_old[2 * p + 1].astype(BF16))
                for p in pairs]
        ws = [jnp.dot(a, s, preferred_element_type=F32) for a, s in zip(wq, s_bd)]
        v_new = [(a - b[:ch]).astype(BF16) for a, b in zip(u, ws)]
        o = [b[ch:] + jnp.dot(a, bd_wide(v), preferred_element_type=F32)
             for a, b, v in zip(qk, ws, v_new)]
        s_new = [s_old[hd] * egl[:, hd:hd + 1]
                 + jnp.dot(kd_t[hd], v_new[hd // 2][:, (hd % 2) * HEAD_D:(hd % 2 + 1) * HEAD_D],
                           preferred_element_type=F32)
                 for hd in range(N_HEADS)]
        for p in pairs:
            o_s[rows, pcols[p]] = o[p]
        for hd in range(N_HEADS):
            s_ref[si, hd] = s_new[hd]
        return carry_

    lax.fori_loop(0, nch, phase2, 0)

    za = jnp.dot(hbf_s[...], w_za[...], preferred_element_type=F32)
    ya = []
    for hd in range(N_HEADS):
        hc = slice(hd * HEAD_D, (hd + 1) * HEAD_D)
        oh = o_s[:, hc]
        ms = jnp.mean(oh * oh, axis=-1, keepdims=True)
        ya.append((oh * lax.rsqrt(ms + RMS_EPS) * norm_a_g[...] * _silu(za[:, hc])).astype(BF16))
    pa = jnp.dot(jnp.concatenate(ya, axis=1), w_proj_a[...], preferred_element_type=F32)

    def zb_cols(k):
        return jnp.dot(hbf_s[...], w_b[:, k * WIDTH_B:(k + 1) * WIDTH_B],
                       preferred_element_type=F32)

    cu = zb_cols(1) * zb_cols(2)
    conv_b = conv_tile(cu, cb_ref, slice(0, WIDTH_B), w_conv_b, CONV_B)
    yb = zb_cols(0) * conv_b * _silu(zb_cols(3))
    pb = jnp.dot(yb.astype(BF16), w_proj_b[...], preferred_element_type=F32)

    gate_a = _sigmoid(jnp.dot(hbf_s[...], w_gate[:, :D_MODEL], preferred_element_type=F32))
    gate_b = _sigmoid(jnp.dot(hbf_s[...], w_gate[:, D_MODEL:], preferred_element_type=F32))
    merged = gate_a * pa + gate_b * pb
    h1 = _layer_norm(ALPHA * y_ref[0] + jnp.dot(merged.astype(BF16), w_out[...],
                                            preferred_element_type=F32),
                     ln1_g[...], ln1_b[...])
    h1_bf = h1.astype(BF16)
    ple = _sigmoid(jnp.dot(h1_bf, w_ple_gate[...], preferred_element_type=F32)) * jnp.dot(
        p_ref[0, 0].astype(BF16), w_ple[...], preferred_element_type=F32)
    y_ref[0] = _layer_norm(ALPHA * h1 + ple, ln2_g[...], ln2_b[...])


def _encode(x, p, hist_a, s0, hist_b, weights, *, tt, ch, seg):
    nb, t_len, _ = x.shape
    nt = t_len // tt
    nseg = tt // seg
    carry = nseg == 1
    assert t_len % tt == 0 and tt % seg == 0 and seg % ch == 0 and (carry or nt == 1)
    assert carry or ch == seg
    assert ch & (ch - 1) == 0 and ch % INV_BLOCK == 0
    nch = tt // ch
    cfg = (tt, ch, seg, carry)

    def tile_spec(width):
        return pl.BlockSpec((1, tt, width), lambda b, t: (b, t, 0))

    def state_spec(shape):
        nd = len(shape)
        return pl.BlockSpec((nseg,) + tuple(shape[1:]), lambda b, t: (b,) + (0,) * (nd - 1),
                            pipeline_mode=pl.Buffered(1))

    def const_spec(arr):
        nd = arr.ndim
        return pl.BlockSpec(arr.shape, lambda b, t: (0,) * nd, pipeline_mode=pl.Buffered(1))

    p_spec = pl.BlockSpec((1, 1, tt, P_DIM), lambda b, t: (0, b, t, 0))
    in_specs = [tile_spec(D_MODEL), p_spec, state_spec(hist_a.shape),
                state_spec(s0.shape), state_spec(hist_b.shape)]
    in_specs += [const_spec(w) for w in weights]
    out_shape = (jax.ShapeDtypeStruct(x.shape, F32),
                 jax.ShapeDtypeStruct(hist_a.shape, F32),
                 jax.ShapeDtypeStruct(s0.shape, F32),
                 jax.ShapeDtypeStruct(hist_b.shape, F32))
    out_specs = (tile_spec(D_MODEL), state_spec(hist_a.shape), state_spec(s0.shape),
                 state_spec(hist_b.shape))
    scratch = [
        pltpu.VMEM((tt, D_MODEL), BF16),
        pltpu.VMEM((tt, KEY_DIM), F32),
        pltpu.VMEM((tt, KEY_DIM), F32),
        pltpu.VMEM((tt, KEY_DIM), F32),
        pltpu.VMEM((tt, LANES), F32),
        pltpu.VMEM((tt, LANES), F32),
        pltpu.VMEM((max(nch, SUBLANES), LANES), F32),
        pltpu.VMEM((tt, KEY_DIM), BF16),
        pltpu.VMEM((tt, KEY_DIM), BF16),
        pltpu.VMEM((N_HEADS // 2, tt, 2 * ch), BF16),
        pltpu.VMEM((SUBLANES + seg, KEY_DIM), F32),
    ]
    return pl.pallas_call(
        functools.partial(_layer_kernel, cfg),
        grid=(nb, nt),
        in_specs=in_specs,
        out_specs=out_specs,
        out_shape=out_shape,
        scratch_shapes=scratch,
        compiler_params=pltpu.CompilerParams(
            dimension_semantics=("arbitrary", "arbitrary"),
            vmem_limit_bytes=VMEM_LIMIT_BYTES),
        name=f"gdn_shortconv_layer_t{tt}_c{ch}",
    )(x, p, hist_a, s0, hist_b, *weights)


def _pad_rows_front(a, rows):
    pad = [(0, 0)] * a.ndim
    pad[-2] = (rows - a.shape[-2], 0)
    return jnp.pad(a, pad)


def _pad_lanes(a):
    pad = [(0, 0)] * a.ndim
    pad[-1] = (0, LANES - a.shape[-1])
    return jnp.pad(a, pad)


def kernel(x_prompt, x_sample, state_conv_a, state_gdn, state_conv_b, p_prompt, p_sample, ln_in_g, ln_in_b, w_in, w_conv_a, a_log, dt_bias, norm_a_g, w_conv_b, w_proj_a, w_proj_b, w_out, ln1_g, ln1_b, w_ple, w_ple_gate, ln2_g, ln2_b):
    assert w_in.shape[0] == DEPTH == 1
    w = w_in[0]
    row = lambda v: v.reshape(1, -1).astype(F32)
    weights = (
        row(ln_in_g), row(ln_in_b),
        w[:, :QKV_DIM].astype(BF16),
        w[:, OFF_ZA:OFF_BETA].astype(BF16),
        _pad_lanes(w[:, OFF_BETA:OFF_DECAY]).astype(BF16),
        _pad_lanes(w[:, OFF_DECAY:OFF_BB]).astype(BF16),
        w[:, OFF_BB:OFF_GATE].astype(BF16),
        w[:, OFF_GATE:].astype(BF16),
        w_conv_a[0].astype(F32),
        _pad_lanes(row(a_log[0])), _pad_lanes(row(dt_bias[0])), row(norm_a_g[0]),
        w_conv_b[0].astype(F32),
        w_proj_a[0].astype(BF16), w_proj_b[0].astype(BF16), w_out[0].astype(BF16),
        row(ln1_g[0]), row(ln1_b[0]),
        w_ple[0].astype(BF16), w_ple_gate[0].astype(BF16),
        row(ln2_g[0]), row(ln2_b[0]),
    )

    bp, seq, _ = x_prompt.shape
    y_p, ca_p, s_p, cb_p = _encode(
        x_prompt, p_prompt,
        jnp.zeros((bp, SUBLANES, QKV_DIM), F32),
        jnp.zeros((bp, N_HEADS, HEAD_D, HEAD_D), F32),
        jnp.zeros((bp, SUBLANES, WIDTH_B), F32),
        weights, tt=PROMPT_TILE, ch=PROMPT_CHUNK, seg=PROMPT_TILE)

    bs, ts, _ = x_sample.shape
    per_tile = SAMPLE_TILE_SEQS
    y_s, ca_s, s_s, cb_s = _encode(
        x_sample.reshape(bs // per_tile, per_tile * ts, D_MODEL),
        p_sample.reshape(DEPTH, bs // per_tile, per_tile * ts, P_DIM),
        _pad_rows_front(state_conv_a[0], SUBLANES),
        state_gdn[0].astype(F32),
        _pad_rows_front(state_conv_b[0], SUBLANES),
        weights, tt=per_tile * ts, ch=ts, seg=ts)

    na, nb = CONV_A - 1, CONV_B - 1
    return (y_p, y_s.reshape(bs, ts, D_MODEL),
            ca_p[None, :, SUBLANES - na:], s_p[None], cb_p[None, :, SUBLANES - nb:],
            ca_s[None, :, SUBLANES - na:], s_s[None].astype(state_gdn.dtype),
            cb_s[None, :, SUBLANES - nb:])
```

```python
import functools

import jax
import jax.numpy as jnp
from jax import lax
from jax.experimental import pallas as pl
from jax.experimental.pallas import tpu as pltpu

D_MODEL = 1024
N_HEADS = 8
HEAD_D = 128
KEY_DIM = N_HEADS * HEAD_D
QKV_DIM = 3 * KEY_DIM
WIDTH_B = D_MODEL
P_DIM = 256
CONV_A = 4
CONV_B = 3
PROMPT_CHUNK = 64
DEPTH = 1
ALPHA = (2 * DEPTH) ** 0.25
LN_EPS = 1e-5
RMS_EPS = 1e-6
L2_EPS = 1e-6

OFF_ZA = QKV_DIM
OFF_BETA = OFF_ZA + KEY_DIM
OFF_DECAY = OFF_BETA + N_HEADS
OFF_BB = OFF_DECAY + N_HEADS
OFF_CB = OFF_BB + WIDTH_B
OFF_UB = OFF_CB + WIDTH_B
OFF_ZB = OFF_UB + WIDTH_B
OFF_GATE = OFF_ZB + WIDTH_B

SUBLANES = 8
LANES = 128
INV_BLOCK = 16
PROMPT_TILE = 256
SAMPLE_TILE_SEQS = 8
P1_CHUNKS = 4
P2_SEGS = 2
VMEM_LIMIT_BYTES = 127 * 512 * 1024

PACKED_COLS = ((QKV_DIM, 0), (KEY_DIM, OFF_ZA), (LANES, OFF_BETA + 4 * WIDTH_B + 2 * D_MODEL),
               (LANES, OFF_BETA + 4 * WIDTH_B + 2 * D_MODEL + LANES), (4 * WIDTH_B, OFF_BETA),
               (2 * D_MODEL, OFF_BETA + 4 * WIDTH_B))

F32 = jnp.float32
NEG_LOG2_E = -1.4426950408889634
BF16 = jnp.bfloat16


def _sigmoid(x):
    return 1.0 / (1.0 + jnp.exp2(x * NEG_LOG2_E))


def _silu(x):
    return x * _sigmoid(x)


def _softplus(x):
    return jnp.maximum(x, 0.0) + jnp.log(1.0 + jnp.exp(-jnp.abs(x)))


def _layer_norm(x, g, b):
    mu = jnp.mean(x, axis=-1, keepdims=True)
    xc = x - mu
    var = jnp.mean(xc * xc, axis=-1, keepdims=True)
    return xc * lax.rsqrt(var + LN_EPS) * g + b


def _mm(a, b):
    return jnp.dot(a.astype(BF16), b.astype(BF16), preferred_element_type=F32)


def _causal_conv(seg, hist8, w_ref, ntaps, buf):
    n = seg.shape[0]
    buf[0:SUBLANES, :] = hist8
    buf[SUBLANES:SUBLANES + n, :] = seg
    acc = seg * w_ref[ntaps - 1:ntaps, :]
    for s in range(1, ntaps):
        acc = acc + buf[SUBLANES - s:SUBLANES - s + n, :] * w_ref[ntaps - 1 - s:ntaps - s, :]
    return acc


def _block_diag2(y1, y2):
    z = jnp.zeros_like(y1)
    return jnp.concatenate([jnp.concatenate([y1, z], axis=1), jnp.concatenate([z, y2], axis=1)],
                           axis=0)


def _inv_unit_lower(a_list, diag_mask, nblk, mm_each):
    d = [jnp.where(diag_mask, a, 0.0) for a in a_list]
    low = [a - x for a, x in zip(a_list, d)]
    d2 = mm_each(d, d)
    d4 = mm_each(d2, d2)
    dd2 = mm_each(d, d2)
    d8 = mm_each(d4, d4)
    x1 = [b - a - c for a, b, c in zip(d, d2, dd2)]
    d4d8 = mm_each(d4, d8)
    x2 = [a + b + c for a, b, c in zip(d4, d8, d4d8)]
    x1x2 = mm_each(x1, x2)
    xd = [a + b + c for a, b, c in zip(x1, x2, x1x2)]
    xdl = mm_each(xd, low)
    n = [a + b for a, b in zip(low, xdl)]
    if nblk == 2:
        xq = [-a for a in n]
    elif nblk == 4:
        n2 = mm_each(n, n)
        nn2 = mm_each(n, n2)
        xq = [b - a - c for a, b, c in zip(n, n2, nn2)]
    else:
        raise NotImplementedError(nblk)
    xqxd = mm_each(xq, xd)
    return [a + b + c for a, b, c in zip(xq, xd, xqxd)]


def _layer_kernel(cfg,
                  x_ref, p_ref, hista_ref, sin_ref, histb_ref,
                  ln_in_g, ln_in_b, w_qkv, w_za, w_beta, w_dec, w_b, w_gate,
                  w_conv_a, a_log, dt_bias, norm_a_g, w_conv_b,
                  w_proj_a, w_proj_b, w_out, ln1_g, ln1_b, w_ple, w_ple_gate, ln2_g, ln2_b,
                  y_ref, ca_ref, s_ref, cb_ref,
                  hbf_s, q_s, k_s, v_s, beta_s, g_s, egl_s, w_s, qd_s, qk_s, cbuf_s):
    u_s, kd_s, o_s = v_s, k_s, q_s
    nseg, seg, ch, carry = cfg
    tt = nseg * seg
    nch = tt // ch
    nch_seg = seg // ch
    nblk = ch // INV_BLOCK
    p1_chunks = min(P1_CHUNKS, nch)
    t_idx = pl.program_id(1)

    def _seed():
        ca_ref[...] = hista_ref[...]
        cb_ref[...] = histb_ref[...]
        s_ref[...] = sin_ref[...]

    if carry:
        pl.when(t_idx == 0)(_seed)
    else:
        _seed()

    def conv_tile(pre, hist_ref, col, w_ref, ntaps):
        outs = []
        for i in range(nseg):
            part = pre[i * seg:(i + 1) * seg]
            outs.append(_causal_conv(part, hist_ref[i, :, col], w_ref.at[:, col], ntaps,
                                     cbuf_s))
            hist_ref[i, :, col] = part[seg - SUBLANES:]
        return outs[0] if nseg == 1 else jnp.concatenate(outs, axis=0)

    h = _layer_norm(x_ref[...].reshape(tt, D_MODEL), ln_in_g[...], ln_in_b[...])
    y_ref[...] = h.reshape(nseg, seg, D_MODEL)
    hbf_s[...] = h.astype(BF16)

    for grp, dst in enumerate((q_s, k_s, v_s)):
        col = slice(grp * KEY_DIM, (grp + 1) * KEY_DIM)
        pre = jnp.dot(hbf_s[...], w_qkv[:, col], preferred_element_type=F32)
        act = _silu(conv_tile(pre, ca_ref, col, w_conv_a, CONV_A))
        if grp == 2:
            dst[...] = act
        else:
            scale = HEAD_D ** -0.5 if grp == 0 else 1.0
            for hd in range(N_HEADS):
                hc = slice(hd * HEAD_D, (hd + 1) * HEAD_D)
                xh = act[:, hc]
                ss = jnp.sum(xh * xh, axis=-1, keepdims=True)
                dst[:, hc] = xh * (lax.rsqrt(ss + L2_EPS) * scale)

    beta_s[...] = _sigmoid(jnp.dot(hbf_s[...], w_beta[...], preferred_element_type=F32))
    zdec = jnp.dot(hbf_s[...], w_dec[...], preferred_element_type=F32)
    g_s[...] = -jnp.exp(a_log[...]) * _softplus(zdec + dt_bias[...])

    pw = 2 * ch
    ri = lax.broadcasted_iota(jnp.int32, (ch, pw), 0)
    cn = lax.broadcasted_iota(jnp.int32, (ch, pw), 1)
    cj = cn & (ch - 1)
    hi = cn >= ch
    hi_row = lax.broadcasted_iota(jnp.int32, (1, pw), 1) >= ch
    causal = ri >= cj
    strict = ri > cj
    diag_mask = (ri // INV_BLOCK) == (cj // INV_BLOCK)
    rt = lax.broadcasted_iota(jnp.int32, (ch, ch), 0)
    ct = lax.broadcasted_iota(jnp.int32, (ch, ch), 1)
    ltri = (rt >= ct).astype(F32)

    pairs = range(N_HEADS // 2)
    pcols = [slice(p * 2 * HEAD_D, (p + 1) * 2 * HEAD_D) for p in pairs]
    hcols = [slice(hd * HEAD_D, (hd + 1) * HEAD_D) for hd in range(N_HEADS)]

    def bd_pair(y):
        zero = jnp.zeros_like(y)
        return jnp.concatenate([jnp.where(hi, zero, y), jnp.where(hi, y, zero)], axis=0)

    def bd_wide(y):
        return _block_diag2(y[:, :HEAD_D], y[:, HEAD_D:])

    def mm_pairs(xs, ys):
        return [jnp.dot(x.astype(BF16), bd_pair(y.astype(BF16)), preferred_element_type=F32)
                for x, y in zip(xs, ys)]

    def lanes_of_pair(cols, p):
        return jnp.concatenate([jnp.broadcast_to(cols[:, 2 * p:2 * p + 1], (ch, HEAD_D)),
                                jnp.broadcast_to(cols[:, 2 * p + 1:2 * p + 2], (ch, HEAD_D))],
                               axis=1)

    def phase1(it, carry_):
        units = []
        pre = {}
        for j in range(p1_chunks):
            c = it * p1_chunks + j
            rows = pl.ds(pl.multiple_of(c * ch, ch), ch)
            g_c = g_s[rows, :]
            beta_c = beta_s[rows, :]
            for p in pairs:
                units.append((j, rows, p))
                pre[j, p] = (k_s[rows, pcols[p]], q_s[rows, pcols[p]], v_s[rows, pcols[p]])
            gc = jnp.dot(ltri, g_c, preferred_element_type=F32,
                         precision=lax.Precision.HIGHEST)
            gc_rows = jnp.concatenate([gc, gc], axis=0).T
            glast = gc[ch - 1:ch, :]
            pre[j] = (c, gc, gc_rows, beta_c, jnp.exp(gc), jnp.exp(glast - gc), jnp.exp(glast))

        dmat, lhs, rk, rhs = [], [], [], []
        for j, rows, p in units:
            kp, qp, vp = pre[j, p]
            _, gc, gc_rows, beta_c, egc, _, _ = pre[j]
            g_col = jnp.where(hi, gc[:, 2 * p + 1:2 * p + 2], gc[:, 2 * p:2 * p + 1])
            g_row = jnp.where(hi_row, gc_rows[2 * p + 1:2 * p + 2, :], gc_rows[2 * p:2 * p + 1, :])
            dmat.append(jnp.where(causal, jnp.exp(jnp.where(causal, g_col - g_row, 0.0)), 0.0))
            kb = kp * lanes_of_pair(beta_c, p)
            lhs.append(jnp.concatenate([kb, qp], axis=0).astype(BF16))
            rk.append(bd_wide(kp.astype(BF16)))
            rhs.append((vp * lanes_of_pair(beta_c, p), kb * lanes_of_pair(egc, p)))
        kkqk = [lax.dot_general(a, b, (((1,), (1,)), ((), ())), preferred_element_type=F32)
                for a, b in zip(lhs, rk)]
        a_mat = [jnp.where(strict, x[:ch] * d, 0.0) for x, d in zip(kkqk, dmat)]
        t_x = _inv_unit_lower(a_mat, diag_mask, nblk, mm_pairs)
        duw = [jnp.dot(t.astype(BF16),
                       jnp.concatenate([bd_wide(ru.astype(BF16)), bd_wide(rw.astype(BF16))], axis=1),
                       preferred_element_type=F32)
               for t, (ru, rw) in zip(t_x, rhs)]

        for i, (j, rows, p) in enumerate(units):
            kp, qp = k_s[rows, pcols[p]], q_s[rows, pcols[p]]
            _, _, _, _, egc, ekd, _ = pre[j]
            qk_s[p, rows, :] = (kkqk[i][ch:] * dmat[i]).astype(BF16)
            u_s[rows, pcols[p]] = rhs[i][0] + duw[i][:, :2 * HEAD_D]
            w_s[rows, pcols[p]] = (rhs[i][1] + duw[i][:, 2 * HEAD_D:]).astype(BF16)
            qd_s[rows, pcols[p]] = (qp * lanes_of_pair(egc, p)).astype(BF16)
            kd_s[rows, pcols[p]] = kp * lanes_of_pair(ekd, p)
        for j in range(p1_chunks):
            egl_s[pl.ds(pre[j][0], 1), :] = pre[j][6]
        return carry_

    lax.fori_loop(0, nch // p1_chunks, phase1, 0)

    ngroups = nseg // P2_SEGS
    assert ngroups == 1 or nch_seg == 1

    def phase2(it, carry_):
        c, sg = (it, 0) if ngroups == 1 else (0, it)
        units = []
        egl = {}
        for i in range(P2_SEGS):
            sq = sg * P2_SEGS + i
            blk = sq * nch_seg + c
            rows = pl.ds(pl.multiple_of(blk * ch, ch), ch)
            egl[i] = egl_s[pl.ds(blk, 1), :]
            units += [(i, sq, rows, p) for p in pairs]
        s_old = {(i, hd): s_ref[sq, hd] for i, sq, _, p in units for hd in (2 * p, 2 * p + 1)}
        wq = [jnp.concatenate([w_s[rows, pcols[p]], qd_s[rows, pcols[p]]], axis=0)
              for _, _, rows, p in units]
        u = [u_s[rows, pcols[p]] for _, _, rows, p in units]
        qk = [qk_s[p, rows, :] for _, _, rows, p in units]
        kd_t = {(i, hd): kd_s[rows, hcols[hd]].T.astype(BF16)
                for i, _, rows, p in units for hd in (2 * p, 2 * p + 1)}
        s_bd = [_block_diag2(s_old[i, 2 * p].astype(BF16), s_old[i, 2 * p + 1].astype(BF16))
                for i, _, _, p in units]
        ws = [jnp.dot(a, sb, preferred_element_type=F32) for a, sb in zip(wq, s_bd)]
        v_new = [(a - b[:ch]).astype(BF16) for a, b in zip(u, ws)]
        o = [b[ch:] + jnp.dot(a, bd_wide(v), preferred_element_type=F32)
             for a, b, v in zip(qk, ws, v_new)]
        s_new = {}
        for n, (i, _, _, p) in enumerate(units):
            for half, hd in enumerate((2 * p, 2 * p + 1)):
                s_new[i, hd] = s_old[i, hd] * egl[i][:, hd:hd + 1] + jnp.dot(
                    kd_t[i, hd], v_new[n][:, half * HEAD_D:(half + 1) * HEAD_D],
                    preferred_element_type=F32)
        for n, (i, sq, rows, p) in enumerate(units):
            o_s[rows, pcols[p]] = o[n]
            s_ref[sq, 2 * p] = s_new[i, 2 * p]
            s_ref[sq, 2 * p + 1] = s_new[i, 2 * p + 1]
        return carry_

    lax.fori_loop(0, nch_seg * ngroups, phase2, 0)

    za = jnp.dot(hbf_s[...], w_za[...], preferred_element_type=F32)
    ya = []
    for hd in range(N_HEADS):
        hc = slice(hd * HEAD_D, (hd + 1) * HEAD_D)
        oh = o_s[:, hc]
        ms = jnp.mean(oh * oh, axis=-1, keepdims=True)
        ya.append((oh * lax.rsqrt(ms + RMS_EPS) * norm_a_g[...] * _silu(za[:, hc])).astype(BF16))
    pa = jnp.dot(jnp.concatenate(ya, axis=1), w_proj_a[...], preferred_element_type=F32)

    def zb_cols(k):
        return jnp.dot(hbf_s[...], w_b[:, k * WIDTH_B:(k + 1) * WIDTH_B],
                       preferred_element_type=F32)

    cu = zb_cols(1) * zb_cols(2)
    conv_b = conv_tile(cu, cb_ref, slice(0, WIDTH_B), w_conv_b, CONV_B)
    yb = zb_cols(0) * conv_b * _silu(zb_cols(3))
    pb = jnp.dot(yb.astype(BF16), w_proj_b[...], preferred_element_type=F32)

    gate_a = _sigmoid(jnp.dot(hbf_s[...], w_gate[:, :D_MODEL], preferred_element_type=F32))
    gate_b = _sigmoid(jnp.dot(hbf_s[...], w_gate[:, D_MODEL:], preferred_element_type=F32))
    merged = gate_a * pa + gate_b * pb
    h1 = _layer_norm(ALPHA * y_ref[...].reshape(tt, D_MODEL) + jnp.dot(merged.astype(BF16), w_out[...],
                                            preferred_element_type=F32),
                     ln1_g[...], ln1_b[...])
    h1_bf = h1.astype(BF16)
    ple = _sigmoid(jnp.dot(h1_bf, w_ple_gate[...], preferred_element_type=F32)) * jnp.dot(
        p_ref[0].reshape(tt, P_DIM).astype(BF16), w_ple[...], preferred_element_type=F32)
    y_ref[...] = _layer_norm(ALPHA * h1 + ple, ln2_g[...], ln2_b[...]).reshape(nseg, seg, D_MODEL)


def _encode(x, p, hist_a, s0, hist_b, weights, *, nseg, seg, ch):
    nseq, t_len, _ = x.shape
    nb, nt = nseq // nseg, t_len // seg
    carry = nt > 1
    assert nseq % nseg == 0 and t_len % seg == 0 and seg % ch == 0 and seg % SUBLANES == 0
    assert ch & (ch - 1) == 0 and ch % INV_BLOCK == 0 and nseg % P2_SEGS == 0
    tt = nseg * seg
    nch = tt // ch
    cfg = (nseg, seg, ch, carry)

    def tile_spec(width):
        return pl.BlockSpec((nseg, seg, width), lambda b, t: (b, t, 0))

    def state_spec(shape):
        nd = len(shape)
        return pl.BlockSpec((nseg,) + tuple(shape[1:]), lambda b, t: (b,) + (0,) * (nd - 1),
                            pipeline_mode=pl.Buffered(1))

    def const_spec(arr):
        nd = arr.ndim
        return pl.BlockSpec(arr.shape, lambda b, t: (0,) * nd, pipeline_mode=pl.Buffered(1))

    def cols_spec(arr, width, offset):
        assert offset % width == 0
        return pl.BlockSpec((arr.shape[0], width), lambda b, t: (0, offset // width),
                            pipeline_mode=pl.Buffered(1))

    w_in_packed, others = weights
    p_spec = pl.BlockSpec((1, nseg, seg, P_DIM), lambda b, t: (0, b, t, 0))
    in_specs = [tile_spec(D_MODEL), p_spec, state_spec(hist_a.shape),
                state_spec(s0.shape), state_spec(hist_b.shape)]
    operands = [x, p, hist_a, s0, hist_b]
    for name, arr in others:
        if name == "w_in":
            for width, offset in PACKED_COLS:
                in_specs.append(cols_spec(w_in_packed, width, offset))
                operands.append(w_in_packed)
        else:
            in_specs.append(const_spec(arr))
            operands.append(arr)
    out_shape = (jax.ShapeDtypeStruct(x.shape, F32),
                 jax.ShapeDtypeStruct(hist_a.shape, F32),
                 jax.ShapeDtypeStruct(s0.shape, F32),
                 jax.ShapeDtypeStruct(hist_b.shape, F32))
    out_specs = (tile_spec(D_MODEL), state_spec(hist_a.shape), state_spec(s0.shape),
                 state_spec(hist_b.shape))
    scratch = [
        pltpu.VMEM((tt, D_MODEL), BF16),
        pltpu.VMEM((tt, KEY_DIM), F32),
        pltpu.VMEM((tt, KEY_DIM), F32),
        pltpu.VMEM((tt, KEY_DIM), F32),
        pltpu.VMEM((tt, LANES), F32),
        pltpu.VMEM((tt, LANES), F32),
        pltpu.VMEM((max(nch, SUBLANES), LANES), F32),
        pltpu.VMEM((tt, KEY_DIM), BF16),
        pltpu.VMEM((tt, KEY_DIM), BF16),
        pltpu.VMEM((N_HEADS // 2, tt, 2 * ch), BF16),
        pltpu.VMEM((SUBLANES + seg, KEY_DIM), F32),
    ]
    return pl.pallas_call(
        functools.partial(_layer_kernel, cfg),
        grid=(nb, nt),
        in_specs=in_specs,
        out_specs=out_specs,
        out_shape=out_shape,
        scratch_shapes=scratch,
        compiler_params=pltpu.CompilerParams(
            dimension_semantics=("arbitrary", "arbitrary"),
            vmem_limit_bytes=VMEM_LIMIT_BYTES),
        name=f"gdn_shortconv_layer_n{nseg}_t{seg}_c{ch}",
    )(*operands)


def _pad_rows_front(a, rows):
    pad = [(0, 0)] * a.ndim
    pad[-2] = (rows - a.shape[-2], 0)
    return jnp.pad(a, pad)


def _pad_lanes(a):
    pad = [(0, 0)] * a.ndim
    pad[-1] = (0, LANES - a.shape[-1])
    return jnp.pad(a, pad)


def kernel(x_prompt, x_sample, state_conv_a, state_gdn, state_conv_b, p_prompt, p_sample, ln_in_g, ln_in_b, w_in, w_conv_a, a_log, dt_bias, norm_a_g, w_conv_b, w_proj_a, w_proj_b, w_out, ln1_g, ln1_b, w_ple, w_ple_gate, ln2_g, ln2_b):
    assert w_in.shape[0] == DEPTH == 1
    w = w_in[0]
    row = lambda v: v.reshape(1, -1).astype(F32)
    w_in_packed = jnp.concatenate(
        [w[:, :OFF_BETA], w[:, OFF_BB:], _pad_lanes(w[:, OFF_BETA:OFF_DECAY]),
         _pad_lanes(w[:, OFF_DECAY:OFF_BB])], axis=1).astype(BF16)
    others = (
        ("ln_in_g", row(ln_in_g)), ("ln_in_b", row(ln_in_b)),
        ("w_in", None),
        ("w_conv_a", w_conv_a[0].astype(F32)),
        ("a_log", _pad_lanes(row(a_log[0]))), ("dt_bias", _pad_lanes(row(dt_bias[0]))),
        ("norm_a_g", row(norm_a_g[0])),
        ("w_conv_b", w_conv_b[0].astype(F32)),
        ("w_proj_a", w_proj_a[0].astype(BF16)), ("w_proj_b", w_proj_b[0].astype(BF16)),
        ("w_out", w_out[0].astype(BF16)),
        ("ln1_g", row(ln1_g[0])), ("ln1_b", row(ln1_b[0])),
        ("w_ple", w_ple[0].astype(BF16)), ("w_ple_gate", w_ple_gate[0].astype(BF16)),
        ("ln2_g", row(ln2_g[0])), ("ln2_b", row(ln2_b[0])),
    )
    weights = (w_in_packed, others)

    bp, seq, _ = x_prompt.shape
    y_p, ca_p, s_p, cb_p = _encode(
        x_prompt, p_prompt,
        jnp.zeros((bp, SUBLANES, QKV_DIM), F32),
        jnp.zeros((bp, N_HEADS, HEAD_D, HEAD_D), F32),
        jnp.zeros((bp, SUBLANES, WIDTH_B), F32),
        weights, nseg=bp, seg=PROMPT_TILE, ch=PROMPT_CHUNK)

    bs, ts, _ = x_sample.shape
    y_s, ca_s, s_s, cb_s = _encode(
        x_sample, p_sample,
        _pad_rows_front(state_conv_a[0], SUBLANES),
        state_gdn[0].astype(F32),
        _pad_rows_front(state_conv_b[0], SUBLANES),
        weights, nseg=SAMPLE_TILE_SEQS, seg=ts, ch=ts)

    na, nb = CONV_A - 1, CONV_B - 1
    return (y_p, y_s,
            ca_p[None, :, SUBLANES - na:], s_p[None], cb_p[None, :, SUBLANES - nb:],
            ca_s[None, :, SUBLANES - na:], s_s[None].astype(state_gdn.dtype),
            cb_s[None, :, SUBLANES - nb:])
```

```python
import functools

import jax
import jax.numpy as jnp
from jax import lax
from jax.experimental import pallas as pl
from jax.experimental.pallas import tpu as pltpu

D_MODEL = 1024
N_HEADS = 8
HEAD_D = 128
KEY_DIM = N_HEADS * HEAD_D
QKV_DIM = 3 * KEY_DIM
WIDTH_B = D_MODEL
P_DIM = 256
CONV_A = 4
CONV_B = 3
PROMPT_CHUNK = 64
DEPTH = 1
ALPHA = (2 * DEPTH) ** 0.25
LN_EPS = 1e-5
RMS_EPS = 1e-6
L2_EPS = 1e-6

OFF_ZA = QKV_DIM
OFF_BETA = OFF_ZA + KEY_DIM
OFF_DECAY = OFF_BETA + N_HEADS
OFF_BB = OFF_DECAY + N_HEADS
OFF_CB = OFF_BB + WIDTH_B
OFF_UB = OFF_CB + WIDTH_B
OFF_ZB = OFF_UB + WIDTH_B
OFF_GATE = OFF_ZB + WIDTH_B

SUBLANES = 8
LANES = 128
INV_BLOCK = 16
PROMPT_TILE = 256
SAMPLE_TILE_SEQS = 8
P1_CHUNKS = 4
QKV_CHUNK = 256
P2_SEGS = 2
VMEM_LIMIT_BYTES = 127 * 512 * 1024

PACKED_COLS = ((QKV_DIM, 0), (KEY_DIM, OFF_ZA), (LANES, OFF_BETA + 4 * WIDTH_B + 2 * D_MODEL),
               (LANES, OFF_BETA + 4 * WIDTH_B + 2 * D_MODEL + LANES), (4 * WIDTH_B, OFF_BETA),
               (2 * D_MODEL, OFF_BETA + 4 * WIDTH_B))

F32 = jnp.float32
NEG_LOG2_E = -1.4426950408889634
BF16 = jnp.bfloat16


def _sigmoid(x):
    return 1.0 / (1.0 + jnp.exp2(x * NEG_LOG2_E))


def _silu(x):
    return x * _sigmoid(x)


def _softplus(x):
    return jnp.maximum(x, 0.0) + jnp.log(1.0 + jnp.exp(-jnp.abs(x)))


def _layer_norm(x, g, b):
    mu = jnp.mean(x, axis=-1, keepdims=True)
    xc = x - mu
    var = jnp.mean(xc * xc, axis=-1, keepdims=True)
    return xc * lax.rsqrt(var + LN_EPS) * g + b


def _mm(a, b):
    return jnp.dot(a.astype(BF16), b.astype(BF16), preferred_element_type=F32)


def _causal_conv(seg, hist8, w_ref, ntaps, buf):
    n = seg.shape[0]
    buf[0:SUBLANES, :] = hist8
    buf[SUBLANES:SUBLANES + n, :] = seg
    acc = seg * w_ref[ntaps - 1:ntaps, :]
    for s in range(1, ntaps):
        acc = acc + buf[SUBLANES - s:SUBLANES - s + n, :] * w_ref[ntaps - 1 - s:ntaps - s, :]
    return acc


def _block_diag2(y1, y2):
    z = jnp.zeros_like(y1)
    return jnp.concatenate([jnp.concatenate([y1, z], axis=1), jnp.concatenate([z, y2], axis=1)],
                           axis=0)


def _inv_unit_lower(a_list, diag_mask, nblk, mm_each):
    d = [jnp.where(diag_mask, a, 0.0) for a in a_list]
    low = [a - x for a, x in zip(a_list, d)]
    d2 = mm_each(d, d)
    d4 = mm_each(d2, d2)
    dd2 = mm_each(d, d2)
    d8 = mm_each(d4, d4)
    x1 = [b - a - c for a, b, c in zip(d, d2, dd2)]
    d4d8 = mm_each(d4, d8)
    x2 = [a + b + c for a, b, c in zip(d4, d8, d4d8)]
    x1x2 = mm_each(x1, x2)
    xd = [a + b + c for a, b, c in zip(x1, x2, x1x2)]
    xdl = mm_each(xd, low)
    n = [a + b for a, b in zip(low, xdl)]
    if nblk == 2:
        xq = [-a for a in n]
    elif nblk == 4:
        n2 = mm_each(n, n)
        nn2 = mm_each(n, n2)
        xq = [b - a - c for a, b, c in zip(n, n2, nn2)]
    else:
        raise NotImplementedError(nblk)
    xqxd = mm_each(xq, xd)
    return [a + b + c for a, b, c in zip(xq, xd, xqxd)]


def _layer_kernel(cfg,
                  x_ref, p_ref, hista_ref, sin_ref, histb_ref,
                  ln_in_g, ln_in_b, w_qkv, w_za, w_beta, w_dec, w_b, w_gate,
                  w_conv_a, a_log, dt_bias, norm_a_g, w_conv_b,
                  w_proj_a, w_proj_b, w_out, ln1_g, ln1_b, w_ple, w_ple_gate, ln2_g, ln2_b,
                  y_ref, ca_ref, s_ref, cb_ref,
                  hbf_s, q_s, k_s, v_s, beta_s, g_s, egl_s, w_s, qd_s, qk_s, cbuf_s):
    u_s, kd_s, o_s = v_s, k_s, q_s
    nseg, seg, ch, carry = cfg
    tt = nseg * seg
    nch = tt // ch
    nch_seg = seg // ch
    nblk = ch // INV_BLOCK
    p1_chunks = min(P1_CHUNKS, nch)
    t_idx = pl.program_id(1)

    def _seed():
        ca_ref[...] = hista_ref[...]
        cb_ref[...] = histb_ref[...]
        s_ref[...] = sin_ref[...]

    if carry:
        pl.when(t_idx == 0)(_seed)
    else:
        _seed()

    def conv_tile(pre, hist_ref, col, w_ref, ntaps, buf):
        outs = []
        for i in range(nseg):
            part = pre[i * seg:(i + 1) * seg]
            outs.append(_causal_conv(part, hist_ref[i, :, col], w_ref.at[:, col], ntaps,
                                     buf))
            hist_ref[i, :, col] = part[seg - SUBLANES:]
        return outs[0] if nseg == 1 else jnp.concatenate(outs, axis=0)

    h = _layer_norm(x_ref[...].reshape(tt, D_MODEL), ln_in_g[...], ln_in_b[...])
    y_ref[...] = h.reshape(nseg, seg, D_MODEL)
    hbf_s[...] = h.astype(BF16)

    for j in range(QKV_DIM // QKV_CHUNK):
        col = slice(j * QKV_CHUNK, (j + 1) * QKV_CHUNK)
        grp, sub = divmod(j, KEY_DIM // QKV_CHUNK)
        dst = (q_s, k_s, v_s)[grp]
        pre = jnp.dot(hbf_s[...], w_qkv[:, col], preferred_element_type=F32)
        act = _silu(conv_tile(pre, ca_ref, col, w_conv_a, CONV_A,
                              cbuf_s.at[:, sub * QKV_CHUNK:(sub + 1) * QKV_CHUNK]))
        for hh in range(QKV_CHUNK // HEAD_D):
            hc = slice(sub * QKV_CHUNK + hh * HEAD_D, sub * QKV_CHUNK + (hh + 1) * HEAD_D)
            xh = act[:, hh * HEAD_D:(hh + 1) * HEAD_D]
            if grp == 2:
                dst[:, hc] = xh
            else:
                scale = HEAD_D ** -0.5 if grp == 0 else 1.0
                ss = jnp.sum(xh * xh, axis=-1, keepdims=True)
                dst[:, hc] = xh * (lax.rsqrt(ss + L2_EPS) * scale)

    beta_s[...] = _sigmoid(jnp.dot(hbf_s[...], w_beta[...], preferred_element_type=F32))
    zdec = jnp.dot(hbf_s[...], w_dec[...], preferred_element_type=F32)
    g_s[...] = -jnp.exp(a_log[...]) * _softplus(zdec + dt_bias[...])

    pw = 2 * ch
    ri = lax.broadcasted_iota(jnp.int32, (ch, pw), 0)
    cn = lax.broadcasted_iota(jnp.int32, (ch, pw), 1)
    cj = cn & (ch - 1)
    hi = cn >= ch
    hi_row = lax.broadcasted_iota(jnp.int32, (1, pw), 1) >= ch
    causal = ri >= cj
    strict = ri > cj
    diag_mask = (ri // INV_BLOCK) == (cj // INV_BLOCK)
    rt = lax.broadcasted_iota(jnp.int32, (ch, ch), 0)
    ct = lax.broadcasted_iota(jnp.int32, (ch, ch), 1)
    ltri = (rt >= ct).astype(F32)

    pairs = range(N_HEADS // 2)
    pcols = [slice(p * 2 * HEAD_D, (p + 1) * 2 * HEAD_D) for p in pairs]
    hcols = [slice(hd * HEAD_D, (hd + 1) * HEAD_D) for hd in range(N_HEADS)]

    def bd_pair(y):
        zero = jnp.zeros_like(y)
        return jnp.concatenate([jnp.where(hi, zero, y), jnp.where(hi, y, zero)], axis=0)

    def bd_wide(y):
        return _block_diag2(y[:, :HEAD_D], y[:, HEAD_D:])

    def mm_pairs(xs, ys):
        return [jnp.dot(x.astype(BF16), bd_pair(y.astype(BF16)), preferred_element_type=F32)
                for x, y in zip(xs, ys)]

    def lanes_of_pair(cols, p):
        return jnp.concatenate([jnp.broadcast_to(cols[:, 2 * p:2 * p + 1], (ch, HEAD_D)),
                                jnp.broadcast_to(cols[:, 2 * p + 1:2 * p + 2], (ch, HEAD_D))],
                               axis=1)

    def block_rows(blk):
        start = blk * ch
        return pl.ds(start if isinstance(start, int) else pl.multiple_of(start, ch), ch)

    per_seq = max(p1_chunks // nseg, 1) if nch_seg > 1 else 0

    def phase1(it, carry_):
        units = []
        pre = {}
        for j in range(p1_chunks):
            if nch_seg == 1:
                c = it * p1_chunks + j
            else:
                c = (j // per_seq) * nch_seg + it * per_seq + j % per_seq
            rows = block_rows(c)
            g_c = g_s[rows, :]
            beta_c = beta_s[rows, :]
            for p in pairs:
                units.append((j, rows, p))
                pre[j, p] = (k_s[rows, pcols[p]], q_s[rows, pcols[p]], v_s[rows, pcols[p]])
            gc = jnp.dot(ltri, g_c, preferred_element_type=F32,
                         precision=lax.Precision.HIGHEST)
            gc_rows = jnp.concatenate([gc, gc], axis=0).T
            glast = gc[ch - 1:ch, :]
            pre[j] = (c, gc, gc_rows, beta_c, jnp.exp(gc), jnp.exp(glast - gc), jnp.exp(glast))

        dmat, lhs, rk, rhs = [], [], [], []
        for j, rows, p in units:
            kp, qp, vp = pre[j, p]
            _, gc, gc_rows, beta_c, egc, _, _ = pre[j]
            g_col = jnp.where(hi, gc[:, 2 * p + 1:2 * p + 2], gc[:, 2 * p:2 * p + 1])
            g_row = jnp.where(hi_row, gc_rows[2 * p + 1:2 * p + 2, :], gc_rows[2 * p:2 * p + 1, :])
            dmat.append(jnp.where(causal, jnp.exp(jnp.where(causal, g_col - g_row, 0.0)), 0.0))
            kb = kp * lanes_of_pair(beta_c, p)
            lhs.append(jnp.concatenate([kb, qp], axis=0).astype(BF16))
            rk.append(bd_wide(kp.astype(BF16)))
            rhs.append((vp * lanes_of_pair(beta_c, p), kb * lanes_of_pair(egc, p)))
        kkqk = [lax.dot_general(a, b, (((1,), (1,)), ((), ())), preferred_element_type=F32)
                for a, b in zip(lhs, rk)]
        a_mat = [jnp.where(strict, x[:ch] * d, 0.0) for x, d in zip(kkqk, dmat)]
        t_x = _inv_unit_lower(a_mat, diag_mask, nblk, mm_pairs)
        duw = [jnp.dot(t.astype(BF16),
                       jnp.concatenate([bd_wide(ru.astype(BF16)), bd_wide(rw.astype(BF16))], axis=1),
                       preferred_element_type=F32)
               for t, (ru, rw) in zip(t_x, rhs)]

        for i, (j, rows, p) in enumerate(units):
            kp, qp = k_s[rows, pcols[p]], q_s[rows, pcols[p]]
            _, _, _, _, egc, ekd, _ = pre[j]
            qk_s[p, rows, :] = (kkqk[i][ch:] * dmat[i]).astype(BF16)
            u_s[rows, pcols[p]] = rhs[i][0] + duw[i][:, :2 * HEAD_D]
            w_s[rows, pcols[p]] = (rhs[i][1] + duw[i][:, 2 * HEAD_D:]).astype(BF16)
            qd_s[rows, pcols[p]] = (qp * lanes_of_pair(egc, p)).astype(BF16)
            kd_s[rows, pcols[p]] = kp * lanes_of_pair(ekd, p)
        for j in range(p1_chunks):
            egl_s[pl.ds(pre[j][0], 1), :] = pre[j][6]
        return carry_


    ngroups = nseg // P2_SEGS
    assert ngroups == 1 or nch_seg == 1

    def phase2(it, carry_):
        c, sg = (it, 0) if ngroups == 1 else (0, it)
        units = []
        egl = {}
        for i in range(P2_SEGS):
            sq = sg * P2_SEGS + i
            blk = sq * nch_seg + c
            rows = block_rows(blk)
            egl[i] = egl_s[pl.ds(blk, 1), :]
            units += [(i, sq, rows, p) for p in pairs]
        s_old = {(i, hd): s_ref[sq, hd] for i, sq, _, p in units for hd in (2 * p, 2 * p + 1)}
        wq = [jnp.concatenate([w_s[rows, pcols[p]], qd_s[rows, pcols[p]]], axis=0)
              for _, _, rows, p in units]
        u = [u_s[rows, pcols[p]] for _, _, rows, p in units]
        qk = [qk_s[p, rows, :] for _, _, rows, p in units]
        kd_t = {(i, hd): kd_s[rows, hcols[hd]].T.astype(BF16)
                for i, _, rows, p in units for hd in (2 * p, 2 * p + 1)}
        s_bd = [_block_diag2(s_old[i, 2 * p].astype(BF16), s_old[i, 2 * p + 1].astype(BF16))
                for i, _, _, p in units]
        ws = [jnp.dot(a, sb, preferred_element_type=F32) for a, sb in zip(wq, s_bd)]
        v_new = [(a - b[:ch]).astype(BF16) for a, b in zip(u, ws)]
        o = [b[ch:] + jnp.dot(a, bd_wide(v), preferred_element_type=F32)
             for a, b, v in zip(qk, ws, v_new)]
        s_new = {}
        for n, (i, _, _, p) in enumerate(units):
            for half, hd in enumerate((2 * p, 2 * p + 1)):
                s_new[i, hd] = s_old[i, hd] * egl[i][:, hd:hd + 1] + jnp.dot(
                    kd_t[i, hd], v_new[n][:, half * HEAD_D:(half + 1) * HEAD_D],
                    preferred_element_type=F32)
        for n, (i, sq, rows, p) in enumerate(units):
            o_s[rows, pcols[p]] = o[n]
            s_ref[sq, 2 * p] = s_new[i, 2 * p]
            s_ref[sq, 2 * p + 1] = s_new[i, 2 * p + 1]
        return carry_

    n_p1, n_p2 = nch // p1_chunks, nch_seg * ngroups
    phase1(0, 0)
    done2 = 0
    for it1 in range(1, n_p1 + 1):
        if it1 < n_p1:
            phase1(it1, 0)
        ready2 = n_p2 * it1 // n_p1
        for it2 in range(done2, ready2):
            phase2(it2, 0)
        done2 = ready2

    za = jnp.dot(hbf_s[...], w_za[...], preferred_element_type=F32)
    ya = []
    for hd in range(N_HEADS):
        hc = slice(hd * HEAD_D, (hd + 1) * HEAD_D)
        oh = o_s[:, hc]
        ms = jnp.mean(oh * oh, axis=-1, keepdims=True)
        ya.append((oh * lax.rsqrt(ms + RMS_EPS) * norm_a_g[...] * _silu(za[:, hc])).astype(BF16))
    pa = jnp.dot(jnp.concatenate(ya, axis=1), w_proj_a[...], preferred_element_type=F32)

    def zb_cols(k):
        return jnp.dot(hbf_s[...], w_b[:, k * WIDTH_B:(k + 1) * WIDTH_B],
                       preferred_element_type=F32)

    cu = zb_cols(1) * zb_cols(2)
    conv_b = conv_tile(cu, cb_ref, slice(0, WIDTH_B), w_conv_b, CONV_B, cbuf_s)
    yb = zb_cols(0) * conv_b * _silu(zb_cols(3))
    pb = jnp.dot(yb.astype(BF16), w_proj_b[...], preferred_element_type=F32)

    gate_a = _sigmoid(jnp.dot(hbf_s[...], w_gate[:, :D_MODEL], preferred_element_type=F32))
    gate_b = _sigmoid(jnp.dot(hbf_s[...], w_gate[:, D_MODEL:], preferred_element_type=F32))
    merged = gate_a * pa + gate_b * pb
    h1 = _layer_norm(ALPHA * y_ref[...].reshape(tt, D_MODEL) + jnp.dot(merged.astype(BF16), w_out[...],
                                            preferred_element_type=F32),
                     ln1_g[...], ln1_b[...])
    h1_bf = h1.astype(BF16)
    ple = _sigmoid(jnp.dot(h1_bf, w_ple_gate[...], preferred_element_type=F32)) * jnp.dot(
        p_ref[0].reshape(tt, P_DIM).astype(BF16), w_ple[...], preferred_element_type=F32)
    y_ref[...] = _layer_norm(ALPHA * h1 + ple, ln2_g[...], ln2_b[...]).reshape(nseg, seg, D_MODEL)


def _encode(x, p, hist_a, s0, hist_b, weights, *, nseg, seg, ch):
    nseq, t_len, _ = x.shape
    nb, nt = nseq // nseg, t_len // seg
    carry = nt > 1
    assert nseq % nseg == 0 and t_len % seg == 0 and seg % ch == 0 and seg % SUBLANES == 0
    assert ch & (ch - 1) == 0 and ch % INV_BLOCK == 0 and nseg % P2_SEGS == 0
    tt = nseg * seg
    nch = tt // ch
    cfg = (nseg, seg, ch, carry)

    def tile_spec(width):
        return pl.BlockSpec((nseg, seg, width), lambda b, t: (b, t, 0))

    def state_spec(shape):
        nd = len(shape)
        return pl.BlockSpec((nseg,) + tuple(shape[1:]), lambda b, t: (b,) + (0,) * (nd - 1),
                            pipeline_mode=pl.Buffered(1))

    def const_spec(arr):
        nd = arr.ndim
        return pl.BlockSpec(arr.shape, lambda b, t: (0,) * nd, pipeline_mode=pl.Buffered(1))

    def cols_spec(arr, width, offset):
        assert offset % width == 0
        return pl.BlockSpec((arr.shape[0], width), lambda b, t: (0, offset // width),
                            pipeline_mode=pl.Buffered(1))

    w_in_packed, others = weights
    p_spec = pl.BlockSpec((1, nseg, seg, P_DIM), lambda b, t: (0, b, t, 0))
    in_specs = [tile_spec(D_MODEL), p_spec, state_spec(hist_a.shape),
                state_spec(s0.shape), state_spec(hist_b.shape)]
    operands = [x, p, hist_a, s0, hist_b]
    for name, arr in others:
        if name == "w_in":
            for width, offset in PACKED_COLS:
                in_specs.append(cols_spec(w_in_packed, width, offset))
                operands.append(w_in_packed)
        else:
            in_specs.append(const_spec(arr))
            operands.append(arr)
    out_shape = (jax.ShapeDtypeStruct(x.shape, F32),
                 jax.ShapeDtypeStruct(hist_a.shape, F32),
                 jax.ShapeDtypeStruct(s0.shape, F32),
                 jax.ShapeDtypeStruct(hist_b.shape, F32))
    out_specs = (tile_spec(D_MODEL), state_spec(hist_a.shape), state_spec(s0.shape),
                 state_spec(hist_b.shape))
    scratch = [
        pltpu.VMEM((tt, D_MODEL), BF16),
        pltpu.VMEM((tt, KEY_DIM), F32),
        pltpu.VMEM((tt, KEY_DIM), F32),
        pltpu.VMEM((tt, KEY_DIM), F32),
        pltpu.VMEM((tt, LANES), F32),
        pltpu.VMEM((tt, LANES), F32),
        pltpu.VMEM((max(nch, SUBLANES), LANES), F32),
        pltpu.VMEM((tt, KEY_DIM), BF16),
        pltpu.VMEM((tt, KEY_DIM), BF16),
        pltpu.VMEM((N_HEADS // 2, tt, 2 * ch), BF16),
        pltpu.VMEM((SUBLANES + seg, KEY_DIM), F32),
    ]
    return pl.pallas_call(
        functools.partial(_layer_kernel, cfg),
        grid=(nb, nt),
        in_specs=in_specs,
        out_specs=out_specs,
        out_shape=out_shape,
        scratch_shapes=scratch,
        compiler_params=pltpu.CompilerParams(
            dimension_semantics=("arbitrary", "arbitrary"),
            vmem_limit_bytes=VMEM_LIMIT_BYTES),
        name=f"gdn_shortconv_layer_n{nseg}_t{seg}_c{ch}",
    )(*operands)


def _pad_rows_front(a, rows):
    pad = [(0, 0)] * a.ndim
    pad[-2] = (rows - a.shape[-2], 0)
    return jnp.pad(a, pad)


def _pad_lanes(a):
    pad = [(0, 0)] * a.ndim
    pad[-1] = (0, LANES - a.shape[-1])
    return jnp.pad(a, pad)


def kernel(x_prompt, x_sample, state_conv_a, state_gdn, state_conv_b, p_prompt, p_sample, ln_in_g, ln_in_b, w_in, w_conv_a, a_log, dt_bias, norm_a_g, w_conv_b, w_proj_a, w_proj_b, w_out, ln1_g, ln1_b, w_ple, w_ple_gate, ln2_g, ln2_b):
    assert w_in.shape[0] == DEPTH == 1
    w = w_in[0]
    row = lambda v: v.reshape(1, -1).astype(F32)
    w_in_packed = jnp.concatenate(
        [w[:, :OFF_BETA], w[:, OFF_BB:], _pad_lanes(w[:, OFF_BETA:OFF_DECAY]),
         _pad_lanes(w[:, OFF_DECAY:OFF_BB])], axis=1).astype(BF16)
    others = (
        ("ln_in_g", row(ln_in_g)), ("ln_in_b", row(ln_in_b)),
        ("w_in", None),
        ("w_conv_a", w_conv_a[0].astype(F32)),
        ("a_log", _pad_lanes(row(a_log[0]))), ("dt_bias", _pad_lanes(row(dt_bias[0]))),
        ("norm_a_g", row(norm_a_g[0])),
        ("w_conv_b", w_conv_b[0].astype(F32)),
        ("w_proj_a", w_proj_a[0].astype(BF16)), ("w_proj_b", w_proj_b[0].astype(BF16)),
        ("w_out", w_out[0].astype(BF16)),
        ("ln1_g", row(ln1_g[0])), ("ln1_b", row(ln1_b[0])),
        ("w_ple", w_ple[0].astype(BF16)), ("w_ple_gate", w_ple_gate[0].astype(BF16)),
        ("ln2_g", row(ln2_g[0])), ("ln2_b", row(ln2_b[0])),
    )
    weights = (w_in_packed, others)

    bp, seq, _ = x_prompt.shape
    y_p, ca_p, s_p, cb_p = _encode(
        x_prompt, p_prompt,
        jnp.zeros((bp, SUBLANES, QKV_DIM), F32),
        jnp.zeros((bp, N_HEADS, HEAD_D, HEAD_D), F32),
        jnp.zeros((bp, SUBLANES, WIDTH_B), F32),
        weights, nseg=bp, seg=PROMPT_TILE, ch=PROMPT_CHUNK)

    bs, ts, _ = x_sample.shape
    y_s, ca_s, s_s, cb_s = _encode(
        x_sample, p_sample,
        _pad_rows_front(state_conv_a[0], SUBLANES),
        state_gdn[0].astype(F32),
        _pad_rows_front(state_conv_b[0], SUBLANES),
        weights, nseg=SAMPLE_TILE_SEQS, seg=ts, ch=ts)

    na, nb = CONV_A - 1, CONV_B - 1
    return (y_p, y_s,
            ca_p[None, :, SUBLANES - na:], s_p[None], cb_p[None, :, SUBLANES - nb:],
            ca_s[None, :, SUBLANES - na:], s_s[None].astype(state_gdn.dtype),
            cb_s[None, :, SUBLANES - nb:])
```

```python
import functools

import jax
import jax.numpy as jnp
from jax import lax
from jax.experimental import pallas as pl
from jax.experimental.pallas import tpu as pltpu

D_MODEL = 1024
N_HEADS = 8
HEAD_D = 128
KEY_DIM = N_HEADS * HEAD_D
QKV_DIM = 3 * KEY_DIM
WIDTH_B = D_MODEL
P_DIM = 256
CONV_A = 4
CONV_B = 3
PROMPT_CHUNK = 64
DEPTH = 1
ALPHA = (2 * DEPTH) ** 0.25
LN_EPS = 1e-5
RMS_EPS = 1e-6
L2_EPS = 1e-6

OFF_ZA = QKV_DIM
OFF_BETA = OFF_ZA + KEY_DIM
OFF_DECAY = OFF_BETA + N_HEADS
OFF_BB = OFF_DECAY + N_HEADS
OFF_CB = OFF_BB + WIDTH_B
OFF_UB = OFF_CB + WIDTH_B
OFF_ZB = OFF_UB + WIDTH_B
OFF_GATE = OFF_ZB + WIDTH_B

SUBLANES = 8
LANES = 128
INV_BLOCK = 16
PROMPT_TILE = 256
SAMPLE_TILE_SEQS = 8
P1_CHUNKS = 4
QKV_CHUNK = 256
P2_SEGS = 2
VMEM_LIMIT_BYTES = 127 * 512 * 1024

PACKED_COLS = ((QKV_DIM, 0), (KEY_DIM, OFF_ZA), (LANES, OFF_BETA + 4 * WIDTH_B + 2 * D_MODEL),
               (LANES, OFF_BETA + 4 * WIDTH_B + 2 * D_MODEL + LANES), (4 * WIDTH_B, OFF_BETA),
               (2 * D_MODEL, OFF_BETA + 4 * WIDTH_B))

F32 = jnp.float32
NEG_LOG2_E = -1.4426950408889634
BF16 = jnp.bfloat16


def _sigmoid(x):
    return 1.0 / (1.0 + jnp.exp2(x * NEG_LOG2_E))


def _silu(x):
    return x * _sigmoid(x)


def _softplus(x):
    return jnp.maximum(x, 0.0) + jnp.log(1.0 + jnp.exp(-jnp.abs(x)))


def _layer_norm(x, g, b):
    mu = jnp.mean(x, axis=-1, keepdims=True)
    xc = x - mu
    var = jnp.mean(xc * xc, axis=-1, keepdims=True)
    return xc * lax.rsqrt(var + LN_EPS) * g + b


def _mm(a, b):
    return jnp.dot(a.astype(BF16), b.astype(BF16), preferred_element_type=F32)


def _causal_conv(seg, hist8, w_ref, ntaps, buf):
    n = seg.shape[0]
    buf[0:SUBLANES, :] = hist8
    buf[SUBLANES:SUBLANES + n, :] = seg
    acc = seg * w_ref[ntaps - 1:ntaps, :]
    for s in range(1, ntaps):
        acc = acc + buf[SUBLANES - s:SUBLANES - s + n, :] * w_ref[ntaps - 1 - s:ntaps - s, :]
    return acc


def _block_diag2(y1, y2):
    z = jnp.zeros_like(y1)
    return jnp.concatenate([jnp.concatenate([y1, z], axis=1), jnp.concatenate([z, y2], axis=1)],
                           axis=0)


def _inv_unit_lower(a_list, diag_mask, nblk, mm_each):
    d = [jnp.where(diag_mask, a, 0.0) for a in a_list]
    low = [a - x for a, x in zip(a_list, d)]
    d2 = mm_each(d, d)
    d4 = mm_each(d2, d2)
    dd2 = mm_each(d, d2)
    d8 = mm_each(d4, d4)
    x1 = [b - a - c for a, b, c in zip(d, d2, dd2)]
    d4d8 = mm_each(d4, d8)
    x2 = [a + b + c for a, b, c in zip(d4, d8, d4d8)]
    x1x2 = mm_each(x1, x2)
    xd = [a + b + c for a, b, c in zip(x1, x2, x1x2)]
    xdl = mm_each(xd, low)
    n = [a + b for a, b in zip(low, xdl)]
    if nblk == 2:
        xq = [-a for a in n]
    elif nblk == 4:
        n2 = mm_each(n, n)
        nn2 = mm_each(n, n2)
        xq = [b - a - c for a, b, c in zip(n, n2, nn2)]
    else:
        raise NotImplementedError(nblk)
    xqxd = mm_each(xq, xd)
    return [a + b + c for a, b, c in zip(xq, xd, xqxd)]


def _layer_kernel(cfg,
                  x_ref, p_ref, hista_ref, sin_ref, histb_ref,
                  ln_in_g, ln_in_b, w_qkv, w_za, w_beta, w_dec, w_b, w_gate,
                  w_conv_a, a_log, dt_bias, norm_a_g, w_conv_b,
                  w_proj_a, w_proj_b, w_out, ln1_g, ln1_b, w_ple, w_ple_gate, ln2_g, ln2_b,
                  y_ref, ca_ref, s_ref, cb_ref,
                  hbf_s, q_s, k_s, v_s, beta_s, g_s, egl_s, w_s, qd_s, qk_s, cbuf_s, pbg_s, ybf_s):
    u_s, kd_s, o_s = v_s, k_s, q_s
    nseg, seg, ch, carry = cfg
    tt = nseg * seg
    nch = tt // ch
    nch_seg = seg // ch
    nblk = ch // INV_BLOCK
    p1_chunks = min(P1_CHUNKS, nch)
    t_idx = pl.program_id(1)

    def _seed():
        ca_ref[...] = hista_ref[...]
        cb_ref[...] = histb_ref[...]
        s_ref[...] = sin_ref[...]

    if carry:
        pl.when(t_idx == 0)(_seed)
    else:
        _seed()

    def conv_tile(pre, hist_ref, col, w_ref, ntaps):
        outs = []
        for i in range(nseg):
            part = pre[i * seg:(i + 1) * seg]
            outs.append(_causal_conv(part, hist_ref[i, :, col], w_ref.at[:, col], ntaps,
                                     cbuf_s))
            hist_ref[i, :, col] = part[seg - SUBLANES:]
        return outs[0] if nseg == 1 else jnp.concatenate(outs, axis=0)

    h = _layer_norm(x_ref[...].reshape(tt, D_MODEL), ln_in_g[...], ln_in_b[...])
    y_ref[...] = h.reshape(nseg, seg, D_MODEL)
    hbf_s[...] = h.astype(BF16)

    for j in range(QKV_DIM // QKV_CHUNK):
        col = slice(j * QKV_CHUNK, (j + 1) * QKV_CHUNK)
        grp, sub = divmod(j, KEY_DIM // QKV_CHUNK)
        dst = (q_s, k_s, v_s)[grp]
        pre = jnp.dot(hbf_s[...], w_qkv[:, col], preferred_element_type=F32)
        act = _silu(conv_tile(pre, ca_ref, col, w_conv_a, CONV_A))
        for hh in range(QKV_CHUNK // HEAD_D):
            hc = slice(sub * QKV_CHUNK + hh * HEAD_D, sub * QKV_CHUNK + (hh + 1) * HEAD_D)
            xh = act[:, hh * HEAD_D:(hh + 1) * HEAD_D]
            if grp == 2:
                dst[:, hc] = xh
            else:
                scale = HEAD_D ** -0.5 if grp == 0 else 1.0
                ss = jnp.sum(xh * xh, axis=-1, keepdims=True)
                dst[:, hc] = xh * (lax.rsqrt(ss + L2_EPS) * scale)

    beta_s[...] = _sigmoid(jnp.dot(hbf_s[...], w_beta[...], preferred_element_type=F32))
    zdec = jnp.dot(hbf_s[...], w_dec[...], preferred_element_type=F32)
    g_s[...] = -jnp.exp(a_log[...]) * _softplus(zdec + dt_bias[...])

    fill_queue = []
    held = {}

    def fill():
        if fill_queue:
            fill_queue.pop(0)()

    def b_proj(k, cj):
        return jnp.dot(hbf_s[...], w_b[:, k * WIDTH_B + cj.start:k * WIDTH_B + cj.stop],
                       preferred_element_type=F32)

    def b_tile_steps(cj):
        def step_c():
            held["cb"] = b_proj(1, cj)

        def step_conv():
            cu = held.pop("cb") * b_proj(2, cj)
            pbg_s[:, cj] = conv_tile(cu, cb_ref, cj, w_conv_b, CONV_B)

        def step_b():
            pbg_s[:, cj] = b_proj(0, cj) * pbg_s[:, cj]

        def step_gate():
            ybf_s[:, cj] = (pbg_s[:, cj] * _silu(b_proj(3, cj))).astype(BF16)

        return [step_c, step_conv, step_b, step_gate]

    def b_out_steps(cj):
        def step_proj():
            pbg_s[:, cj] = jnp.dot(ybf_s[...], w_proj_b[:, cj], preferred_element_type=F32)

        def step_merge_gate():
            gate_b = _sigmoid(jnp.dot(hbf_s[...], w_gate[:, D_MODEL + cj.start:D_MODEL + cj.stop],
                                      preferred_element_type=F32))
            pbg_s[:, cj] = gate_b * pbg_s[:, cj]

        return [step_proj, step_merge_gate]

    b_tiles = [slice(j * QKV_CHUNK, (j + 1) * QKV_CHUNK) for j in range(WIDTH_B // QKV_CHUNK)]
    for cj in b_tiles:
        fill_queue += b_tile_steps(cj)
    for cj in b_tiles:
        fill_queue += b_out_steps(cj)

    pw = 2 * ch
    ri = lax.broadcasted_iota(jnp.int32, (ch, pw), 0)
    cn = lax.broadcasted_iota(jnp.int32, (ch, pw), 1)
    cj = cn & (ch - 1)
    hi = cn >= ch
    hi_row = lax.broadcasted_iota(jnp.int32, (1, pw), 1) >= ch
    causal = ri >= cj
    strict = ri > cj
    diag_mask = (ri // INV_BLOCK) == (cj // INV_BLOCK)
    rt = lax.broadcasted_iota(jnp.int32, (ch, ch), 0)
    ct = lax.broadcasted_iota(jnp.int32, (ch, ch), 1)
    ltri = (rt >= ct).astype(F32)

    pairs = range(N_HEADS // 2)
    pcols = [slice(p * 2 * HEAD_D, (p + 1) * 2 * HEAD_D) for p in pairs]
    hcols = [slice(hd * HEAD_D, (hd + 1) * HEAD_D) for hd in range(N_HEADS)]

    def bd_pair(y):
        zero = jnp.zeros_like(y)
        return jnp.concatenate([jnp.where(hi, zero, y), jnp.where(hi, y, zero)], axis=0)

    def bd_wide(y):
        return _block_diag2(y[:, :HEAD_D], y[:, HEAD_D:])

    def mm_pairs(xs, ys):
        out = [jnp.dot(x.astype(BF16), bd_pair(y.astype(BF16)), preferred_element_type=F32)
               for x, y in zip(xs, ys)]
        fill()
        return out

    def lanes_of_pair(cols, p):
        return jnp.concatenate([jnp.broadcast_to(cols[:, 2 * p:2 * p + 1], (ch, HEAD_D)),
                                jnp.broadcast_to(cols[:, 2 * p + 1:2 * p + 2], (ch, HEAD_D))],
                               axis=1)

    def block_rows(blk):
        start = blk * ch
        return pl.ds(start if isinstance(start, int) else pl.multiple_of(start, ch), ch)

    per_seq = max(p1_chunks // nseg, 1) if nch_seg > 1 else 0

    def phase1(it, carry_):
        units = []
        pre = {}
        for j in range(p1_chunks):
            if nch_seg == 1:
                c = it * p1_chunks + j
            else:
                c = (j // per_seq) * nch_seg + it * per_seq + j % per_seq
            rows = block_rows(c)
            g_c = g_s[rows, :]
            beta_c = beta_s[rows, :]
            for p in pairs:
                units.append((j, rows, p))
                pre[j, p] = (k_s[rows, pcols[p]], q_s[rows, pcols[p]], v_s[rows, pcols[p]])
            gc = jnp.dot(ltri, g_c, preferred_element_type=F32,
                         precision=lax.Precision.HIGHEST)
            gc_rows = jnp.concatenate([gc, gc], axis=0).T
            glast = gc[ch - 1:ch, :]
            pre[j] = (c, gc, gc_rows, beta_c, jnp.exp(gc), jnp.exp(glast - gc), jnp.exp(glast))

        dmat, lhs, rk, rhs = [], [], [], []
        for j, rows, p in units:
            kp, qp, vp = pre[j, p]
            _, gc, gc_rows, beta_c, egc, _, _ = pre[j]
            g_col = jnp.where(hi, gc[:, 2 * p + 1:2 * p + 2], gc[:, 2 * p:2 * p + 1])
            g_row = jnp.where(hi_row, gc_rows[2 * p + 1:2 * p + 2, :], gc_rows[2 * p:2 * p + 1, :])
            dmat.append(jnp.where(causal, jnp.exp(jnp.where(causal, g_col - g_row, 0.0)), 0.0))
            kb = kp * lanes_of_pair(beta_c, p)
            lhs.append(jnp.concatenate([kb, qp], axis=0).astype(BF16))
            rk.append(bd_wide(kp.astype(BF16)))
            rhs.append((vp * lanes_of_pair(beta_c, p), kb * lanes_of_pair(egc, p)))
        kkqk = [lax.dot_general(a, b, (((1,), (1,)), ((), ())), preferred_element_type=F32)
                for a, b in zip(lhs, rk)]
        fill()
        a_mat = [jnp.where(strict, x[:ch] * d, 0.0) for x, d in zip(kkqk, dmat)]
        t_x = _inv_unit_lower(a_mat, diag_mask, nblk, mm_pairs)
        duw = [jnp.dot(t.astype(BF16),
                       jnp.concatenate([bd_wide(ru.astype(BF16)), bd_wide(rw.astype(BF16))], axis=1),
                       preferred_element_type=F32)
               for t, (ru, rw) in zip(t_x, rhs)]

        fill()
        for i, (j, rows, p) in enumerate(units):
            kp, qp = k_s[rows, pcols[p]], q_s[rows, pcols[p]]
            _, _, _, _, egc, ekd, _ = pre[j]
            qk_s[p, rows, :] = (kkqk[i][ch:] * dmat[i]).astype(BF16)
            u_s[rows, pcols[p]] = rhs[i][0] + duw[i][:, :2 * HEAD_D]
            w_s[rows, pcols[p]] = (rhs[i][1] + duw[i][:, 2 * HEAD_D:]).astype(BF16)
            qd_s[rows, pcols[p]] = (qp * lanes_of_pair(egc, p)).astype(BF16)
            kd_s[rows, pcols[p]] = kp * lanes_of_pair(ekd, p)
        for j in range(p1_chunks):
            egl_s[pl.ds(pre[j][0], 1), :] = pre[j][6]
        return carry_


    ngroups = nseg // P2_SEGS
    assert ngroups == 1 or nch_seg == 1

    def phase2(it, carry_):
        c, sg = (it, 0) if ngroups == 1 else (0, it)
        units = []
        egl = {}
        for i in range(P2_SEGS):
            sq = sg * P2_SEGS + i
            blk = sq * nch_seg + c
            rows = block_rows(blk)
            egl[i] = egl_s[pl.ds(blk, 1), :]
            units += [(i, sq, rows, p) for p in pairs]
        s_old = {(i, hd): s_ref[sq, hd] for i, sq, _, p in units for hd in (2 * p, 2 * p + 1)}
        wq = [jnp.concatenate([w_s[rows, pcols[p]], qd_s[rows, pcols[p]]], axis=0)
              for _, _, rows, p in units]
        u = [u_s[rows, pcols[p]] for _, _, rows, p in units]
        qk = [qk_s[p, rows, :] for _, _, rows, p in units]
        kd_t = {(i, hd): kd_s[rows, hcols[hd]].T.astype(BF16)
                for i, _, rows, p in units for hd in (2 * p, 2 * p + 1)}
        s_bd = [_block_diag2(s_old[i, 2 * p].astype(BF16), s_old[i, 2 * p + 1].astype(BF16))
                for i, _, _, p in units]
        ws = [jnp.dot(a, sb, preferred_element_type=F32) for a, sb in zip(wq, s_bd)]
        v_new = [(a - b[:ch]).astype(BF16) for a, b in zip(u, ws)]
        o = [b[ch:] + jnp.dot(a, bd_wide(v), preferred_element_type=F32)
             for a, b, v in zip(qk, ws, v_new)]
        s_new = {}
        for n, (i, _, _, p) in enumerate(units):
            for half, hd in enumerate((2 * p, 2 * p + 1)):
                s_new[i, hd] = s_old[i, hd] * egl[i][:, hd:hd + 1] + jnp.dot(
                    kd_t[i, hd], v_new[n][:, half * HEAD_D:(half + 1) * HEAD_D],
                    preferred_element_type=F32)
        for n, (i, sq, rows, p) in enumerate(units):
            o_s[rows, pcols[p]] = o[n]
            s_ref[sq, 2 * p] = s_new[i, 2 * p]
            s_ref[sq, 2 * p + 1] = s_new[i, 2 * p + 1]
        return carry_

    n_p1, n_p2 = nch // p1_chunks, nch_seg * ngroups
    phase1(0, 0)
    done2 = 0
    for it1 in range(1, n_p1 + 1):
        if it1 < n_p1:
            phase1(it1, 0)
        ready2 = n_p2 * it1 // n_p1
        for it2 in range(done2, ready2):
            phase2(it2, 0)
        done2 = ready2
    while fill_queue:
        fill()

    za = jnp.dot(hbf_s[...], w_za[...], preferred_element_type=F32)
    ya = []
    for hd in range(N_HEADS):
        hc = slice(hd * HEAD_D, (hd + 1) * HEAD_D)
        oh = o_s[:, hc]
        ms = jnp.mean(oh * oh, axis=-1, keepdims=True)
        ya.append((oh * lax.rsqrt(ms + RMS_EPS) * norm_a_g[...] * _silu(za[:, hc])).astype(BF16))
    pa = jnp.dot(jnp.concatenate(ya, axis=1), w_proj_a[...], preferred_element_type=F32)

    gate_a = _sigmoid(jnp.dot(hbf_s[...], w_gate[:, :D_MODEL], preferred_element_type=F32))
    merged = gate_a * pa + pbg_s[...]
    h1 = _layer_norm(ALPHA * y_ref[...].reshape(tt, D_MODEL) + jnp.dot(merged.astype(BF16), w_out[...],
                                            preferred_element_type=F32),
                     ln1_g[...], ln1_b[...])
    h1_bf = h1.astype(BF16)
    ple = _sigmoid(jnp.dot(h1_bf, w_ple_gate[...], preferred_element_type=F32)) * jnp.dot(
        p_ref[0].reshape(tt, P_DIM).astype(BF16), w_ple[...], preferred_element_type=F32)
    y_ref[...] = _layer_norm(ALPHA * h1 + ple, ln2_g[...], ln2_b[...]).reshape(nseg, seg, D_MODEL)


def _encode(x, p, hist_a, s0, hist_b, weights, *, nseg, seg, ch):
    nseq, t_len, _ = x.shape
    nb, nt = nseq // nseg, t_len // seg
    carry = nt > 1
    assert nseq % nseg == 0 and t_len % seg == 0 and seg % ch == 0 and seg % SUBLANES == 0
    assert ch & (ch - 1) == 0 and ch % INV_BLOCK == 0 and nseg % P2_SEGS == 0
    tt = nseg * seg
    nch = tt // ch
    cfg = (nseg, seg, ch, carry)

    def tile_spec(width):
        return pl.BlockSpec((nseg, seg, width), lambda b, t: (b, t, 0))

    def state_spec(shape):
        nd = len(shape)
        return pl.BlockSpec((nseg,) + tuple(shape[1:]), lambda b, t: (b,) + (0,) * (nd - 1),
                            pipeline_mode=pl.Buffered(1))

    def const_spec(arr):
        nd = arr.ndim
        return pl.BlockSpec(arr.shape, lambda b, t: (0,) * nd, pipeline_mode=pl.Buffered(1))

    def cols_spec(arr, width, offset):
        assert offset % width == 0
        return pl.BlockSpec((arr.shape[0], width), lambda b, t: (0, offset // width),
                            pipeline_mode=pl.Buffered(1))

    w_in_packed, others = weights
    p_spec = pl.BlockSpec((1, nseg, seg, P_DIM), lambda b, t: (0, b, t, 0))
    in_specs = [tile_spec(D_MODEL), p_spec, state_spec(hist_a.shape),
                state_spec(s0.shape), state_spec(hist_b.shape)]
    operands = [x, p, hist_a, s0, hist_b]
    for name, arr in others:
        if name == "w_in":
            for width, offset in PACKED_COLS:
                in_specs.append(cols_spec(w_in_packed, width, offset))
                operands.append(w_in_packed)
        else:
            in_specs.append(const_spec(arr))
            operands.append(arr)
    out_shape = (jax.ShapeDtypeStruct(x.shape, F32),
                 jax.ShapeDtypeStruct(hist_a.shape, F32),
                 jax.ShapeDtypeStruct(s0.shape, F32),
                 jax.ShapeDtypeStruct(hist_b.shape, F32))
    out_specs = (tile_spec(D_MODEL), state_spec(hist_a.shape), state_spec(s0.shape),
                 state_spec(hist_b.shape))
    scratch = [
        pltpu.VMEM((tt, D_MODEL), BF16),
        pltpu.VMEM((tt, KEY_DIM), F32),
        pltpu.VMEM((tt, KEY_DIM), F32),
        pltpu.VMEM((tt, KEY_DIM), F32),
        pltpu.VMEM((tt, LANES), F32),
        pltpu.VMEM((tt, LANES), F32),
        pltpu.VMEM((max(nch, SUBLANES), LANES), F32),
        pltpu.VMEM((tt, KEY_DIM), BF16),
        pltpu.VMEM((tt, KEY_DIM), BF16),
        pltpu.VMEM((N_HEADS // 2, tt, 2 * ch), BF16),
        pltpu.VMEM((SUBLANES + seg, QKV_CHUNK), F32),
        pltpu.VMEM((tt, D_MODEL), F32),
        pltpu.VMEM((tt, WIDTH_B), BF16),
    ]
    return pl.pallas_call(
        functools.partial(_layer_kernel, cfg),
        grid=(nb, nt),
        in_specs=in_specs,
        out_specs=out_specs,
        out_shape=out_shape,
        scratch_shapes=scratch,
        compiler_params=pltpu.CompilerParams(
            dimension_semantics=("arbitrary", "arbitrary"),
            vmem_limit_bytes=VMEM_LIMIT_BYTES),
        name=f"gdn_shortconv_layer_n{nseg}_t{seg}_c{ch}",
    )(*operands)


def _pad_rows_front(a, rows):
    pad = [(0, 0)] * a.ndim
    pad[-2] = (rows - a.shape[-2], 0)
    return jnp.pad(a, pad)


def _pad_lanes(a):
    pad = [(0, 0)] * a.ndim
    pad[-1] = (0, LANES - a.shape[-1])
    return jnp.pad(a, pad)


def kernel(x_prompt, x_sample, state_conv_a, state_gdn, state_conv_b, p_prompt, p_sample, ln_in_g, ln_in_b, w_in, w_conv_a, a_log, dt_bias, norm_a_g, w_conv_b, w_proj_a, w_proj_b, w_out, ln1_g, ln1_b, w_ple, w_ple_gate, ln2_g, ln2_b):
    assert w_in.shape[0] == DEPTH == 1
    w = w_in[0]
    row = lambda v: v.reshape(1, -1).astype(F32)
    w_in_packed = jnp.concatenate(
        [w[:, :OFF_BETA], w[:, OFF_BB:], _pad_lanes(w[:, OFF_BETA:OFF_DECAY]),
         _pad_lanes(w[:, OFF_DECAY:OFF_BB])], axis=1).astype(BF16)
    others = (
        ("ln_in_g", row(ln_in_g)), ("ln_in_b", row(ln_in_b)),
        ("w_in", None),
        ("w_conv_a", w_conv_a[0].astype(F32)),
        ("a_log", _pad_lanes(row(a_log[0]))), ("dt_bias", _pad_lanes(row(dt_bias[0]))),
        ("norm_a_g", row(norm_a_g[0])),
        ("w_conv_b", w_conv_b[0].astype(F32)),
        ("w_proj_a", w_proj_a[0].astype(BF16)), ("w_proj_b", w_proj_b[0].astype(BF16)),
        ("w_out", w_out[0].astype(BF16)),
        ("ln1_g", row(ln1_g[0])), ("ln1_b", row(ln1_b[0])),
        ("w_ple", w_ple[0].astype(BF16)), ("w_ple_gate", w_ple_gate[0].astype(BF16)),
        ("ln2_g", row(ln2_g[0])), ("ln2_b", row(ln2_b[0])),
    )
    weights = (w_in_packed, others)

    bp, seq, _ = x_prompt.shape
    y_p, ca_p, s_p, cb_p = _encode(
        x_prompt, p_prompt,
        jnp.zeros((bp, SUBLANES, QKV_DIM), F32),
        jnp.zeros((bp, N_HEADS, HEAD_D, HEAD_D), F32),
        jnp.zeros((bp, SUBLANES, WIDTH_B), F32),
        weights, nseg=bp, seg=PROMPT_TILE, ch=PROMPT_CHUNK)

    bs, ts, _ = x_sample.shape
    y_s, ca_s, s_s, cb_s = _encode(
        x_sample, p_sample,
        _pad_rows_front(state_conv_a[0], SUBLANES),
        state_gdn[0].astype(F32),
        _pad_rows_front(state_conv_b[0], SUBLANES),
        weights, nseg=SAMPLE_TILE_SEQS, seg=ts, ch=ts)

    na, nb = CONV_A - 1, CONV_B - 1
    return (y_p, y_s,
            ca_p[None, :, SUBLANES - na:], s_p[None], cb_p[None, :, SUBLANES - nb:],
            ca_s[None, :, SUBLANES - na:], s_s[None].astype(state_gdn.dtype),
            cb_s[None, :, SUBLANES - nb:])
```

```python
import functools

import jax
import jax.numpy as jnp
from jax import lax
from jax.experimental import pallas as pl
from jax.experimental.pallas import tpu as pltpu

D_MODEL = 1024
N_HEADS = 8
HEAD_D = 128
KEY_DIM = N_HEADS * HEAD_D
QKV_DIM = 3 * KEY_DIM
WIDTH_B = D_MODEL
P_DIM = 256
CONV_A = 4
CONV_B = 3
PROMPT_CHUNK = 64
DEPTH = 1
ALPHA = (2 * DEPTH) ** 0.25
LN_EPS = 1e-5
RMS_EPS = 1e-6
L2_EPS = 1e-6

OFF_ZA = QKV_DIM
OFF_BETA = OFF_ZA + KEY_DIM
OFF_DECAY = OFF_BETA + N_HEADS
OFF_BB = OFF_DECAY + N_HEADS
OFF_CB = OFF_BB + WIDTH_B
OFF_UB = OFF_CB + WIDTH_B
OFF_ZB = OFF_UB + WIDTH_B
OFF_GATE = OFF_ZB + WIDTH_B
IN_DIM = OFF_GATE + 2 * D_MODEL

SUBLANES = 8
LANES = 128
INV_BLOCK = 16
PROMPT_TILE = 256
SAMPLE_TILE_SEQS = 8
P1_CHUNKS = 4
QKV_CHUNK = 256
P2_SEGS = 2
VMEM_LIMIT_BYTES = 127 * 512 * 1024

PACK_W = 512
PACKED_DIM = IN_DIM - (OFF_BB - OFF_BETA) + 2 * LANES
PACKED_COLS = ((QKV_DIM, 0), (KEY_DIM, OFF_ZA), (LANES, OFF_BETA + 4 * WIDTH_B + 2 * D_MODEL),
               (LANES, OFF_BETA + 4 * WIDTH_B + 2 * D_MODEL + LANES), (4 * WIDTH_B, OFF_BETA),
               (2 * D_MODEL, OFF_BETA + 4 * WIDTH_B))

F32 = jnp.float32
NEG_LOG2_E = -1.4426950408889634
BF16 = jnp.bfloat16


def _sigmoid(x):
    return 1.0 / (1.0 + jnp.exp2(x * NEG_LOG2_E))


def _silu(x):
    return x * _sigmoid(x)


def _softplus(x):
    return jnp.maximum(x, 0.0) + jnp.log(1.0 + jnp.exp(-jnp.abs(x)))


def _layer_norm(x, g, b):
    mu = jnp.mean(x, axis=-1, keepdims=True)
    xc = x - mu
    var = jnp.mean(xc * xc, axis=-1, keepdims=True)
    return xc * lax.rsqrt(var + LN_EPS) * g + b


def _mm(a, b):
    return jnp.dot(a.astype(BF16), b.astype(BF16), preferred_element_type=F32)


def _causal_conv(seg, hist8, w_ref, ntaps, buf):
    n = seg.shape[0]
    buf[0:SUBLANES, :] = hist8
    buf[SUBLANES:SUBLANES + n, :] = seg
    acc = seg * w_ref[ntaps - 1:ntaps, :]
    for s in range(1, ntaps):
        acc = acc + buf[SUBLANES - s:SUBLANES - s + n, :] * w_ref[ntaps - 1 - s:ntaps - s, :]
    return acc


def _block_diag2(y1, y2):
    z = jnp.zeros_like(y1)
    return jnp.concatenate([jnp.concatenate([y1, z], axis=1), jnp.concatenate([z, y2], axis=1)],
                           axis=0)


def _inv_unit_lower(a_list, diag_mask, nblk, mm_each):
    d = [jnp.where(diag_mask, a, 0.0) for a in a_list]
    low = [a - x for a, x in zip(a_list, d)]
    d2 = mm_each(d, d)
    d4 = mm_each(d2, d2)
    dd2 = mm_each(d, d2)
    d8 = mm_each(d4, d4)
    x1 = [b - a - c for a, b, c in zip(d, d2, dd2)]
    d4d8 = mm_each(d4, d8)
    x2 = [a + b + c for a, b, c in zip(d4, d8, d4d8)]
    x1x2 = mm_each(x1, x2)
    xd = [a + b + c for a, b, c in zip(x1, x2, x1x2)]
    xdl = mm_each(xd, low)
    n = [a + b for a, b in zip(low, xdl)]
    if nblk == 2:
        xq = [-a for a in n]
    elif nblk == 4:
        n2 = mm_each(n, n)
        nn2 = mm_each(n, n2)
        xq = [b - a - c for a, b, c in zip(n, n2, nn2)]
    else:
        raise NotImplementedError(nblk)
    xqxd = mm_each(xq, xd)
    return [a + b + c for a, b, c in zip(xq, xd, xqxd)]


def _layer_kernel(cfg, x_ref, p_ref, *refs):
    init_refs, refs = (None, refs) if cfg[4] else (refs[:3], refs[3:])
    _layer_body(cfg, x_ref, p_ref, init_refs, *refs)


def _layer_body(cfg,
                  x_ref, p_ref, init_refs,
                  ln_in_g, ln_in_b, w_qkv, w_za, w_beta, w_dec, w_b, w_gate,
                  w_conv_a, a_log, dt_bias, norm_a_g, w_conv_b,
                  w_proj_a, w_proj_b, w_out, ln1_g, ln1_b, w_ple, w_ple_gate, ln2_g, ln2_b,
                  y_ref, ca_ref, s_ref, cb_ref,
                  hbf_s, q_s, k_s, v_s, beta_s, g_s, egl_s, w_s, qd_s, qk_s, cbuf_s, pbg_s, ybf_s):
    u_s, kd_s, o_s = v_s, k_s, q_s
    nseg, seg, ch, carry, zero_init = cfg
    tt = nseg * seg
    nch = tt // ch
    nch_seg = seg // ch
    nblk = ch // INV_BLOCK
    p1_chunks = min(P1_CHUNKS, nch)
    t_idx = pl.program_id(1)

    def _seed():
        if zero_init:
            ca_ref[...] = jnp.zeros(ca_ref.shape, F32)
            cb_ref[...] = jnp.zeros(cb_ref.shape, F32)
            s_ref[...] = jnp.zeros(s_ref.shape, F32)
        else:
            hista_ref, sin_ref, histb_ref = init_refs
            ca_ref[...] = hista_ref[...]
            cb_ref[...] = histb_ref[...]
            s_ref[...] = sin_ref[...]

    if carry:
        pl.when(t_idx == 0)(_seed)
    else:
        _seed()

    def conv_tile(pre, hist_ref, col, w_ref, ntaps):
        outs = []
        for i in range(nseg):
            part = pre[i * seg:(i + 1) * seg]
            outs.append(_causal_conv(part, hist_ref[i, :, col], w_ref.at[:, col], ntaps,
                                     cbuf_s))
            hist_ref[i, :, col] = part[seg - SUBLANES:]
        return outs[0] if nseg == 1 else jnp.concatenate(outs, axis=0)

    h = _layer_norm(x_ref[...].reshape(tt, D_MODEL), ln_in_g[...], ln_in_b[...])
    y_ref[...] = h.reshape(nseg, seg, D_MODEL)
    hbf_s[...] = h.astype(BF16)

    for j in range(QKV_DIM // QKV_CHUNK):
        col = slice(j * QKV_CHUNK, (j + 1) * QKV_CHUNK)
        grp, sub = divmod(j, KEY_DIM // QKV_CHUNK)
        dst = (q_s, k_s, v_s)[grp]
        pre = jnp.dot(hbf_s[...], w_qkv[:, col], preferred_element_type=F32)
        act = _silu(conv_tile(pre, ca_ref, col, w_conv_a, CONV_A))
        for hh in range(QKV_CHUNK // HEAD_D):
            hc = slice(sub * QKV_CHUNK + hh * HEAD_D, sub * QKV_CHUNK + (hh + 1) * HEAD_D)
            xh = act[:, hh * HEAD_D:(hh + 1) * HEAD_D]
            if grp == 2:
                dst[:, hc] = xh
            else:
                scale = HEAD_D ** -0.5 if grp == 0 else 1.0
                ss = jnp.sum(xh * xh, axis=-1, keepdims=True)
                dst[:, hc] = xh * (lax.rsqrt(ss + L2_EPS) * scale)

    beta_s[...] = _sigmoid(jnp.dot(hbf_s[...], w_beta[...], preferred_element_type=F32))
    zdec = jnp.dot(hbf_s[...], w_dec[...], preferred_element_type=F32)
    g_s[...] = -jnp.exp(a_log[...]) * _softplus(zdec + dt_bias[...])

    fill_queue = []
    held = {}

    def fill():
        if fill_queue:
            fill_queue.pop(0)()

    def b_proj(k, cj):
        return jnp.dot(hbf_s[...], w_b[:, k * WIDTH_B + cj.start:k * WIDTH_B + cj.stop],
                       preferred_element_type=F32)

    def b_tile_steps(cj):
        def step_c():
            held["cb"] = b_proj(1, cj)

        def step_conv():
            cu = held.pop("cb") * b_proj(2, cj)
            pbg_s[:, cj] = conv_tile(cu, cb_ref, cj, w_conv_b, CONV_B)

        def step_b():
            pbg_s[:, cj] = b_proj(0, cj) * pbg_s[:, cj]

        def step_gate():
            ybf_s[:, cj] = (pbg_s[:, cj] * _silu(b_proj(3, cj))).astype(BF16)

        return [step_c, step_conv, step_b, step_gate]

    def b_out_steps(cj):
        def step_proj():
            pbg_s[:, cj] = jnp.dot(ybf_s[...], w_proj_b[:, cj], preferred_element_type=F32)

        def step_merge_gate():
            gate_b = _sigmoid(jnp.dot(hbf_s[...], w_gate[:, D_MODEL + cj.start:D_MODEL + cj.stop],
                                      preferred_element_type=F32))
            pbg_s[:, cj] = gate_b * pbg_s[:, cj]

        return [step_proj, step_merge_gate]

    b_tiles = [slice(j * QKV_CHUNK, (j + 1) * QKV_CHUNK) for j in range(WIDTH_B // QKV_CHUNK)]
    for cj in b_tiles:
        fill_queue += b_tile_steps(cj)
    for cj in b_tiles:
        fill_queue += b_out_steps(cj)

    pw = 2 * ch
    ri = lax.broadcasted_iota(jnp.int32, (ch, pw), 0)
    cn = lax.broadcasted_iota(jnp.int32, (ch, pw), 1)
    cj = cn & (ch - 1)
    hi = cn >= ch
    hi_row = lax.broadcasted_iota(jnp.int32, (1, pw), 1) >= ch
    causal = ri >= cj
    strict = ri > cj
    diag_mask = (ri // INV_BLOCK) == (cj // INV_BLOCK)
    rt = lax.broadcasted_iota(jnp.int32, (ch, ch), 0)
    ct = lax.broadcasted_iota(jnp.int32, (ch, ch), 1)
    ltri = (rt >= ct).astype(F32)

    pairs = range(N_HEADS // 2)
    pcols = [slice(p * 2 * HEAD_D, (p + 1) * 2 * HEAD_D) for p in pairs]
    hcols = [slice(hd * HEAD_D, (hd + 1) * HEAD_D) for hd in range(N_HEADS)]

    def bd_pair(y):
        zero = jnp.zeros_like(y)
        return jnp.concatenate([jnp.where(hi, zero, y), jnp.where(hi, y, zero)], axis=0)

    def bd_wide(y):
        return _block_diag2(y[:, :HEAD_D], y[:, HEAD_D:])

    def mm_pairs(xs, ys):
        out = [jnp.dot(x.astype(BF16), bd_pair(y.astype(BF16)), preferred_element_type=F32)
               for x, y in zip(xs, ys)]
        fill()
        return out

    def lanes_of_pair(cols, p):
        return jnp.concatenate([jnp.broadcast_to(cols[:, 2 * p:2 * p + 1], (ch, HEAD_D)),
                                jnp.broadcast_to(cols[:, 2 * p + 1:2 * p + 2], (ch, HEAD_D))],
                               axis=1)

    def block_rows(blk):
        start = blk * ch
        return pl.ds(start if isinstance(start, int) else pl.multiple_of(start, ch), ch)

    per_seq = max(p1_chunks // nseg, 1) if nch_seg > 1 else 0

    def phase1(it, carry_):
        units = []
        pre = {}
        for j in range(p1_chunks):
            if nch_seg == 1:
                c = it * p1_chunks + j
            else:
                c = (j // per_seq) * nch_seg + it * per_seq + j % per_seq
            rows = block_rows(c)
            g_c = g_s[rows, :]
            beta_c = beta_s[rows, :]
            for p in pairs:
                units.append((j, rows, p))
                pre[j, p] = (k_s[rows, pcols[p]], q_s[rows, pcols[p]], v_s[rows, pcols[p]])
            gc = jnp.dot(ltri, g_c, preferred_element_type=F32,
                         precision=lax.Precision.HIGHEST)
            gc_rows = jnp.concatenate([gc, gc], axis=0).T
            glast = gc[ch - 1:ch, :]
            pre[j] = (c, gc, gc_rows, beta_c, jnp.exp(gc), jnp.exp(glast - gc), jnp.exp(glast))

        dmat, lhs, rk, rhs = [], [], [], []
        for j, rows, p in units:
            kp, qp, vp = pre[j, p]
            _, gc, gc_rows, beta_c, egc, _, _ = pre[j]
            g_col = jnp.where(hi, gc[:, 2 * p + 1:2 * p + 2], gc[:, 2 * p:2 * p + 1])
            g_row = jnp.where(hi_row, gc_rows[2 * p + 1:2 * p + 2, :], gc_rows[2 * p:2 * p + 1, :])
            dmat.append(jnp.where(causal, jnp.exp(jnp.where(causal, g_col - g_row, 0.0)), 0.0))
            kb = kp * lanes_of_pair(beta_c, p)
            lhs.append(jnp.concatenate([kb, qp], axis=0).astype(BF16))
            rk.append(bd_wide(kp.astype(BF16)))
            rhs.append((vp * lanes_of_pair(beta_c, p), kb * lanes_of_pair(egc, p)))
        kkqk = [lax.dot_general(a, b, (((1,), (1,)), ((), ())), preferred_element_type=F32)
                for a, b in zip(lhs, rk)]
        fill()
        a_mat = [jnp.where(strict, x[:ch] * d, 0.0) for x, d in zip(kkqk, dmat)]
        t_x = _inv_unit_lower(a_mat, diag_mask, nblk, mm_pairs)
        duw = [jnp.dot(t.astype(BF16),
                       jnp.concatenate([bd_wide(ru.astype(BF16)), bd_wide(rw.astype(BF16))], axis=1),
                       preferred_element_type=F32)
               for t, (ru, rw) in zip(t_x, rhs)]

        fill()
        for i, (j, rows, p) in enumerate(units):
            kp, qp = k_s[rows, pcols[p]], q_s[rows, pcols[p]]
            _, _, _, _, egc, ekd, _ = pre[j]
            qk_s[p, rows, :] = (kkqk[i][ch:] * dmat[i]).astype(BF16)
            u_s[rows, pcols[p]] = rhs[i][0] + duw[i][:, :2 * HEAD_D]
            w_s[rows, pcols[p]] = (rhs[i][1] + duw[i][:, 2 * HEAD_D:]).astype(BF16)
            qd_s[rows, pcols[p]] = (qp * lanes_of_pair(egc, p)).astype(BF16)
            kd_s[rows, pcols[p]] = kp * lanes_of_pair(ekd, p)
        for j in range(p1_chunks):
            egl_s[pl.ds(pre[j][0], 1), :] = pre[j][6]
        return carry_


    ngroups = nseg // P2_SEGS
    assert ngroups == 1 or nch_seg == 1

    def phase2(it, carry_):
        c, sg = (it, 0) if ngroups == 1 else (0, it)
        units = []
        egl = {}
        for i in range(P2_SEGS):
            sq = sg * P2_SEGS + i
            blk = sq * nch_seg + c
            rows = block_rows(blk)
            egl[i] = egl_s[pl.ds(blk, 1), :]
            units += [(i, sq, rows, p) for p in pairs]
        s_old = {(i, hd): s_ref[sq, hd] for i, sq, _, p in units for hd in (2 * p, 2 * p + 1)}
        wq = [jnp.concatenate([w_s[rows, pcols[p]], qd_s[rows, pcols[p]]], axis=0)
              for _, _, rows, p in units]
        u = [u_s[rows, pcols[p]] for _, _, rows, p in units]
        qk = [qk_s[p, rows, :] for _, _, rows, p in units]
        kd_t = {(i, hd): kd_s[rows, hcols[hd]].T.astype(BF16)
                for i, _, rows, p in units for hd in (2 * p, 2 * p + 1)}
        s_bd = [_block_diag2(s_old[i, 2 * p].astype(BF16), s_old[i, 2 * p + 1].astype(BF16))
                for i, _, _, p in units]
        ws = [jnp.dot(a, sb, preferred_element_type=F32) for a, sb in zip(wq, s_bd)]
        v_new = [(a - b[:ch]).astype(BF16) for a, b in zip(u, ws)]
        o = [b[ch:] + jnp.dot(a, bd_wide(v), preferred_element_type=F32)
             for a, b, v in zip(qk, ws, v_new)]
        s_new = {}
        for n, (i, _, _, p) in enumerate(units):
            for half, hd in enumerate((2 * p, 2 * p + 1)):
                s_new[i, hd] = s_old[i, hd] * egl[i][:, hd:hd + 1] + jnp.dot(
                    kd_t[i, hd], v_new[n][:, half * HEAD_D:(half + 1) * HEAD_D],
                    preferred_element_type=F32)
        for n, (i, sq, rows, p) in enumerate(units):
            o_s[rows, pcols[p]] = o[n]
            s_ref[sq, 2 * p] = s_new[i, 2 * p]
            s_ref[sq, 2 * p + 1] = s_new[i, 2 * p + 1]
        return carry_

    n_p1, n_p2 = nch // p1_chunks, nch_seg * ngroups
    phase1(0, 0)
    done2 = 0
    for it1 in range(1, n_p1 + 1):
        if it1 < n_p1:
            phase1(it1, 0)
        ready2 = n_p2 * it1 // n_p1
        for it2 in range(done2, ready2):
            phase2(it2, 0)
        done2 = ready2
    while fill_queue:
        fill()

    za = jnp.dot(hbf_s[...], w_za[...], preferred_element_type=F32)
    ya = []
    for hd in range(N_HEADS):
        hc = slice(hd * HEAD_D, (hd + 1) * HEAD_D)
        oh = o_s[:, hc]
        ms = jnp.mean(oh * oh, axis=-1, keepdims=True)
        ya.append((oh * lax.rsqrt(ms + RMS_EPS) * norm_a_g[...] * _silu(za[:, hc])).astype(BF16))
    pa = jnp.dot(jnp.concatenate(ya, axis=1), w_proj_a[...], preferred_element_type=F32)

    gate_a = _sigmoid(jnp.dot(hbf_s[...], w_gate[:, :D_MODEL], preferred_element_type=F32))
    merged = gate_a * pa + pbg_s[...]
    h1 = _layer_norm(ALPHA * y_ref[...].reshape(tt, D_MODEL) + jnp.dot(merged.astype(BF16), w_out[...],
                                            preferred_element_type=F32),
                     ln1_g[...], ln1_b[...])
    h1_bf = h1.astype(BF16)
    ple = _sigmoid(jnp.dot(h1_bf, w_ple_gate[...], preferred_element_type=F32)) * jnp.dot(
        p_ref[0].reshape(tt, P_DIM).astype(BF16), w_ple[...], preferred_element_type=F32)
    y_ref[...] = _layer_norm(ALPHA * h1 + ple, ln2_g[...], ln2_b[...]).reshape(nseg, seg, D_MODEL)


def _encode(x, p, init_state, weights, *, nseg, seg, ch):
    nseq, t_len, _ = x.shape
    zero_init = init_state is None
    state_shapes = ((nseq, SUBLANES, QKV_DIM), (nseq, N_HEADS, HEAD_D, HEAD_D),
                    (nseq, SUBLANES, WIDTH_B))
    nb, nt = nseq // nseg, t_len // seg
    carry = nt > 1
    assert nseq % nseg == 0 and t_len % seg == 0 and seg % ch == 0 and seg % SUBLANES == 0
    assert ch & (ch - 1) == 0 and ch % INV_BLOCK == 0 and nseg % P2_SEGS == 0
    tt = nseg * seg
    nch = tt // ch
    cfg = (nseg, seg, ch, carry, zero_init)

    def tile_spec(width):
        return pl.BlockSpec((nseg, seg, width), lambda b, t: (b, t, 0))

    def state_spec(shape):
        nd = len(shape)
        return pl.BlockSpec((nseg,) + tuple(shape[1:]), lambda b, t: (b,) + (0,) * (nd - 1),
                            pipeline_mode=pl.Buffered(1))

    def const_spec(arr):
        nd = arr.ndim
        return pl.BlockSpec(arr.shape, lambda b, t: (0,) * nd, pipeline_mode=pl.Buffered(1))

    def cols_spec(arr, width, offset):
        assert offset % width == 0
        return pl.BlockSpec((arr.shape[0], width), lambda b, t: (0, offset // width),
                            pipeline_mode=pl.Buffered(1))

    w_in_packed, others = weights
    p_spec = pl.BlockSpec((1, nseg, seg, P_DIM), lambda b, t: (0, b, t, 0))
    in_specs = [tile_spec(D_MODEL), p_spec]
    operands = [x, p]
    if not zero_init:
        assert tuple(a.shape for a in init_state) == state_shapes
        in_specs += [state_spec(shape) for shape in state_shapes]
        operands += list(init_state)
    for name, arr in others:
        if name == "w_in":
            for width, offset in PACKED_COLS:
                in_specs.append(cols_spec(w_in_packed, width, offset))
                operands.append(w_in_packed)
        else:
            in_specs.append(const_spec(arr))
            operands.append(arr)
    out_shape = (jax.ShapeDtypeStruct(x.shape, F32),) + tuple(
        jax.ShapeDtypeStruct(shape, F32) for shape in state_shapes)
    out_specs = (tile_spec(D_MODEL),) + tuple(state_spec(shape) for shape in state_shapes)
    scratch = [
        pltpu.VMEM((tt, D_MODEL), BF16),
        pltpu.VMEM((tt, KEY_DIM), F32),
        pltpu.VMEM((tt, KEY_DIM), F32),
        pltpu.VMEM((tt, KEY_DIM), F32),
        pltpu.VMEM((tt, LANES), F32),
        pltpu.VMEM((tt, LANES), F32),
        pltpu.VMEM((max(nch, SUBLANES), LANES), F32),
        pltpu.VMEM((tt, KEY_DIM), BF16),
        pltpu.VMEM((tt, KEY_DIM), BF16),
        pltpu.VMEM((N_HEADS // 2, tt, 2 * ch), BF16),
        pltpu.VMEM((SUBLANES + seg, QKV_CHUNK), F32),
        pltpu.VMEM((tt, D_MODEL), F32),
        pltpu.VMEM((tt, WIDTH_B), BF16),
    ]
    return pl.pallas_call(
        functools.partial(_layer_kernel, cfg),
        grid=(nb, nt),
        in_specs=in_specs,
        out_specs=out_specs,
        out_shape=out_shape,
        scratch_shapes=scratch,
        compiler_params=pltpu.CompilerParams(
            dimension_semantics=("arbitrary", "arbitrary"),
            vmem_limit_bytes=VMEM_LIMIT_BYTES),
        name=f"gdn_shortconv_layer_n{nseg}_t{seg}_c{ch}",
    )(*operands)


def _pack_kernel(a_ref, b_ref, o_ref):
    j = pl.program_id(0)
    n_plain = OFF_BETA // PACK_W
    n_shift = (IN_DIM - OFF_BB) // PACK_W
    shift = OFF_BB - OFF_BETA
    lane = lax.broadcasted_iota(jnp.int32, (1, LANES), 1)
    nsub = PACK_W // LANES

    @pl.when(j < n_plain)
    def _():
        o_ref[...] = a_ref[...].astype(BF16)

    @pl.when((j >= n_plain) & (j < n_plain + n_shift))
    def _():
        groups = [a_ref[:, k * LANES:(k + 1) * LANES] for k in range(nsub)] + [b_ref[...]]
        rolled = [pltpu.roll(g, LANES - shift, 1) for g in groups]
        for k in range(nsub):
            o_ref[:, k * LANES:(k + 1) * LANES] = jnp.where(
                lane < LANES - shift, rolled[k], rolled[k + 1]).astype(BF16)

    @pl.when(j == n_plain + n_shift)
    def _():
        first = a_ref[:, :LANES]
        o_ref[:, :LANES] = jnp.where(lane < N_HEADS, first, 0.0).astype(BF16)
        o_ref[:, LANES:2 * LANES] = jnp.where(
            lane < N_HEADS, pltpu.roll(first, LANES - N_HEADS, 1), 0.0).astype(BF16)
        o_ref[:, 2 * LANES:] = jnp.zeros((o_ref.shape[0], PACK_W - 2 * LANES), BF16)


def _pack_w_in(w):
    n_plain = OFF_BETA // PACK_W
    n_shift = (IN_DIM - OFF_BB) // PACK_W
    assert OFF_BETA % PACK_W == 0 and (IN_DIM - OFF_BB) % PACK_W == 0 and OFF_DECAY - OFF_BETA == N_HEADS
    nsub = PACK_W // LANES

    def a_idx(j):
        return (0, jnp.where(j < n_plain + n_shift, j, n_plain))

    def b_idx(j):
        return (0, jnp.where((j >= n_plain) & (j < n_plain + n_shift), nsub * (j + 1), 0))

    return pl.pallas_call(
        _pack_kernel,
        grid=(n_plain + n_shift + 1,),
        in_specs=[pl.BlockSpec((D_MODEL, PACK_W), a_idx), pl.BlockSpec((D_MODEL, LANES), b_idx)],
        out_specs=pl.BlockSpec((D_MODEL, PACK_W), lambda j: (0, j)),
        out_shape=jax.ShapeDtypeStruct((D_MODEL, PACKED_DIM), BF16),
        compiler_params=pltpu.CompilerParams(dimension_semantics=("arbitrary",)),
        name="pack_w_in",
    )(w, w)


def _pad_rows_front(a, rows):
    pad = [(0, 0)] * a.ndim
    pad[-2] = (rows - a.shape[-2], 0)
    return jnp.pad(a, pad)


def _pad_lanes(a):
    pad = [(0, 0)] * a.ndim
    pad[-1] = (0, LANES - a.shape[-1])
    return jnp.pad(a, pad)


def kernel(x_prompt, x_sample, state_conv_a, state_gdn, state_conv_b, p_prompt, p_sample, ln_in_g, ln_in_b, w_in, w_conv_a, a_log, dt_bias, norm_a_g, w_conv_b, w_proj_a, w_proj_b, w_out, ln1_g, ln1_b, w_ple, w_ple_gate, ln2_g, ln2_b):
    assert w_in.shape[0] == DEPTH == 1
    row = lambda v: v.reshape(1, -1).astype(F32)
    w_in_packed = _pack_w_in(w_in.reshape(D_MODEL, IN_DIM))
    others = (
        ("ln_in_g", row(ln_in_g)), ("ln_in_b", row(ln_in_b)),
        ("w_in", None),
        ("w_conv_a", w_conv_a[0].astype(F32)),
        ("a_log", _pad_lanes(row(a_log[0]))), ("dt_bias", _pad_lanes(row(dt_bias[0]))),
        ("norm_a_g", row(norm_a_g[0])),
        ("w_conv_b", w_conv_b[0].astype(F32)),
        ("w_proj_a", w_proj_a[0].astype(BF16)), ("w_proj_b", w_proj_b[0].astype(BF16)),
        ("w_out", w_out[0].astype(BF16)),
        ("ln1_g", row(ln1_g[0])), ("ln1_b", row(ln1_b[0])),
        ("w_ple", w_ple[0].astype(BF16)), ("w_ple_gate", w_ple_gate[0].astype(BF16)),
        ("ln2_g", row(ln2_g[0])), ("ln2_b", row(ln2_b[0])),
    )
    weights = (w_in_packed, others)

    bp, seq, _ = x_prompt.shape
    y_p, ca_p, s_p, cb_p = _encode(
        x_prompt, p_prompt, None, weights, nseg=bp, seg=PROMPT_TILE, ch=PROMPT_CHUNK)

    bs, ts, _ = x_sample.shape
    y_s, ca_s, s_s, cb_s = _encode(
        x_sample, p_sample,
        (_pad_rows_front(state_conv_a[0], SUBLANES), state_gdn[0].astype(F32),
         _pad_rows_front(state_conv_b[0], SUBLANES)),
        weights, nseg=SAMPLE_TILE_SEQS, seg=ts, ch=ts)

    na, nb = CONV_A - 1, CONV_B - 1
    return (y_p, y_s,
            ca_p[None, :, SUBLANES - na:], s_p[None], cb_p[None, :, SUBLANES - nb:],
            ca_s[None, :, SUBLANES - na:], s_s[None].astype(state_gdn.dtype),
            cb_s[None, :, SUBLANES - nb:])
```

```python
import functools

import jax
import jax.numpy as jnp
from jax import lax
from jax.experimental import pallas as pl
from jax.experimental.pallas import tpu as pltpu

D_MODEL = 1024
N_HEADS = 8
HEAD_D = 128
KEY_DIM = N_HEADS * HEAD_D
QKV_DIM = 3 * KEY_DIM
WIDTH_B = D_MODEL
P_DIM = 256
CONV_A = 4
CONV_B = 3
PROMPT_CHUNK = 64
DEPTH = 1
ALPHA = (2 * DEPTH) ** 0.25
LN_EPS = 1e-5
RMS_EPS = 1e-6
L2_EPS = 1e-6

OFF_ZA = QKV_DIM
OFF_BETA = OFF_ZA + KEY_DIM
OFF_DECAY = OFF_BETA + N_HEADS
OFF_BB = OFF_DECAY + N_HEADS
OFF_CB = OFF_BB + WIDTH_B
OFF_UB = OFF_CB + WIDTH_B
OFF_ZB = OFF_UB + WIDTH_B
OFF_GATE = OFF_ZB + WIDTH_B
IN_DIM = OFF_GATE + 2 * D_MODEL

SUBLANES = 8
LANES = 128
INV_BLOCK = 16
PROMPT_TILE = 256
SAMPLE_TILE_SEQS = 8
P1_CHUNKS = 4
QKV_CHUNK = 256
P2_SEGS = 2
VMEM_LIMIT_BYTES = 127 * 512 * 1024

PACK_W = 512
PACKED_DIM = IN_DIM - (OFF_BB - OFF_BETA) + 2 * LANES
PACKED_COLS = ((QKV_DIM, 0), (KEY_DIM, OFF_ZA), (LANES, OFF_BETA + 4 * WIDTH_B + 2 * D_MODEL),
               (LANES, OFF_BETA + 4 * WIDTH_B + 2 * D_MODEL + LANES), (4 * WIDTH_B, OFF_BETA),
               (2 * D_MODEL, OFF_BETA + 4 * WIDTH_B))

F32 = jnp.float32
NEG_LOG2_E = -1.4426950408889634
BF16 = jnp.bfloat16


def _sigmoid(x):
    return 1.0 / (1.0 + jnp.exp2(x * NEG_LOG2_E))


def _silu(x):
    return x * _sigmoid(x)


def _softplus(x):
    return jnp.maximum(x, 0.0) + jnp.log(1.0 + jnp.exp(-jnp.abs(x)))


def _layer_norm(x, g, b):
    mu = jnp.mean(x, axis=-1, keepdims=True)
    xc = x - mu
    var = jnp.mean(xc * xc, axis=-1, keepdims=True)
    return xc * lax.rsqrt(var + LN_EPS) * g + b


def _mm(a, b):
    return jnp.dot(a.astype(BF16), b.astype(BF16), preferred_element_type=F32)


def _causal_conv(seg, hist8, w_ref, ntaps, buf):
    n = seg.shape[0]
    buf[0:SUBLANES, :] = hist8
    buf[SUBLANES:SUBLANES + n, :] = seg
    acc = seg * w_ref[ntaps - 1:ntaps, :]
    for s in range(1, ntaps):
        acc = acc + buf[SUBLANES - s:SUBLANES - s + n, :] * w_ref[ntaps - 1 - s:ntaps - s, :]
    return acc


def _block_diag2(y1, y2):
    z = jnp.zeros_like(y1)
    return jnp.concatenate([jnp.concatenate([y1, z], axis=1), jnp.concatenate([z, y2], axis=1)],
                           axis=0)


def _inv_unit_lower(a_list, diag_mask, nblk, mm_each):
    d = [jnp.where(diag_mask, a, 0.0) for a in a_list]
    low = [a - x for a, x in zip(a_list, d)]
    d2 = mm_each(d, d)
    d4 = mm_each(d2, d2)
    dd2 = mm_each(d, d2)
    d8 = mm_each(d4, d4)
    x1 = [b - a - c for a, b, c in zip(d, d2, dd2)]
    d4d8 = mm_each(d4, d8)
    x2 = [a + b + c for a, b, c in zip(d4, d8, d4d8)]
    x1x2 = mm_each(x1, x2)
    xd = [a + b + c for a, b, c in zip(x1, x2, x1x2)]
    xdl = mm_each(xd, low)
    n = [a + b for a, b in zip(low, xdl)]
    if nblk == 2:
        xq = [-a for a in n]
    elif nblk == 4:
        n2 = mm_each(n, n)
        nn2 = mm_each(n, n2)
        xq = [b - a - c for a, b, c in zip(n, n2, nn2)]
    else:
        raise NotImplementedError(nblk)
    xqxd = mm_each(xq, xd)
    return [a + b + c for a, b, c in zip(xq, xd, xqxd)]


def _layer_kernel(cfg, x_ref, p_ref, *refs):
    init_refs, refs = (None, refs) if cfg[4] else (refs[:3], refs[3:])
    _layer_body(cfg, x_ref, p_ref, init_refs, *refs)


def _layer_body(cfg,
                  x_ref, p_ref, init_refs,
                  ln_in_g, ln_in_b, w_qkv, w_za, w_beta, w_dec, w_b, w_gate,
                  w_conv_a, a_log, dt_bias, norm_a_g, w_conv_b,
                  w_proj_a, w_proj_b, w_out, ln1_g, ln1_b, w_ple, w_ple_gate, ln2_g, ln2_b,
                  y_ref, ca_ref, s_ref, cb_ref,
                  hbf_s, q_s, k_s, v_s, beta_s, g_s, egl_s, w_s, qd_s, qk_s, cbuf_s, pbg_s, ybf_s):
    u_s, kd_s, o_s = v_s, k_s, q_s
    nseg, seg, ch, carry, zero_init = cfg
    tt = nseg * seg
    nch = tt // ch
    nch_seg = seg // ch
    nblk = ch // INV_BLOCK
    p1_chunks = min(P1_CHUNKS, nch)
    t_idx = pl.program_id(1)

    def _seed():
        if zero_init:
            ca_ref[...] = jnp.zeros(ca_ref.shape, F32)
            cb_ref[...] = jnp.zeros(cb_ref.shape, F32)
            s_ref[...] = jnp.zeros(s_ref.shape, F32)
        else:
            hista_ref, sin_ref, histb_ref = init_refs
            ca_ref[...] = hista_ref[...]
            cb_ref[...] = histb_ref[...]
            s_ref[...] = sin_ref[...]

    if carry:
        pl.when(t_idx == 0)(_seed)
    else:
        _seed()

    def conv_tile(pre, hist_ref, col, w_ref, ntaps):
        outs = []
        for i in range(nseg):
            part = pre[i * seg:(i + 1) * seg]
            outs.append(_causal_conv(part, hist_ref[i, :, col], w_ref.at[:, col], ntaps,
                                     cbuf_s))
            hist_ref[i, :, col] = part[seg - SUBLANES:]
        return outs[0] if nseg == 1 else jnp.concatenate(outs, axis=0)

    h = _layer_norm(x_ref[...].reshape(tt, D_MODEL), ln_in_g[...], ln_in_b[...])
    y_ref[...] = h.reshape(nseg, seg, D_MODEL)
    hbf_s[...] = h.astype(BF16)

    for j in range(QKV_DIM // QKV_CHUNK):
        col = slice(j * QKV_CHUNK, (j + 1) * QKV_CHUNK)
        grp, sub = divmod(j, KEY_DIM // QKV_CHUNK)
        dst = (q_s, k_s, v_s)[grp]
        pre = jnp.dot(hbf_s[...], w_qkv[:, col], preferred_element_type=F32)
        act = _silu(conv_tile(pre, ca_ref, col, w_conv_a, CONV_A))
        for hh in range(QKV_CHUNK // HEAD_D):
            hc = slice(sub * QKV_CHUNK + hh * HEAD_D, sub * QKV_CHUNK + (hh + 1) * HEAD_D)
            xh = act[:, hh * HEAD_D:(hh + 1) * HEAD_D]
            if grp == 2:
                dst[:, hc] = xh
            else:
                scale = HEAD_D ** -0.5 if grp == 0 else 1.0
                ss = jnp.sum(xh * xh, axis=-1, keepdims=True)
                dst[:, hc] = xh * (lax.rsqrt(ss + L2_EPS) * scale)

    beta_s[...] = _sigmoid(jnp.dot(hbf_s[...], w_beta[...], preferred_element_type=F32))
    zdec = jnp.dot(hbf_s[...], w_dec[...], preferred_element_type=F32)
    g_s[...] = -jnp.exp(a_log[...]) * _softplus(zdec + dt_bias[...])

    fill_queue = []
    held = {}

    def fill():
        if fill_queue:
            fill_queue.pop(0)()

    def b_proj(k, cj):
        return jnp.dot(hbf_s[...], w_b[:, k * WIDTH_B + cj.start:k * WIDTH_B + cj.stop],
                       preferred_element_type=F32)

    def b_tile_steps(cj):
        def step_c():
            held["cb"] = b_proj(1, cj)

        def step_conv():
            cu = held.pop("cb") * b_proj(2, cj)
            pbg_s[:, cj] = conv_tile(cu, cb_ref, cj, w_conv_b, CONV_B)

        def step_b():
            pbg_s[:, cj] = b_proj(0, cj) * pbg_s[:, cj]

        def step_gate():
            ybf_s[:, cj] = (pbg_s[:, cj] * _silu(b_proj(3, cj))).astype(BF16)

        return [step_c, step_conv, step_b, step_gate]

    def b_out_steps(cj):
        def step_proj():
            pbg_s[:, cj] = jnp.dot(ybf_s[...], w_proj_b[:, cj], preferred_element_type=F32)

        def step_merge_gate():
            gate_b = _sigmoid(jnp.dot(hbf_s[...], w_gate[:, D_MODEL + cj.start:D_MODEL + cj.stop],
                                      preferred_element_type=F32))
            pbg_s[:, cj] = gate_b * pbg_s[:, cj]

        return [step_proj, step_merge_gate]

    b_tiles = [slice(j * QKV_CHUNK, (j + 1) * QKV_CHUNK) for j in range(WIDTH_B // QKV_CHUNK)]
    for cj in b_tiles:
        fill_queue += b_tile_steps(cj)
    for cj in b_tiles:
        fill_queue += b_out_steps(cj)

    pw = 2 * ch
    ri = lax.broadcasted_iota(jnp.int32, (ch, pw), 0)
    cn = lax.broadcasted_iota(jnp.int32, (ch, pw), 1)
    cj = cn & (ch - 1)
    hi = cn >= ch
    hi_row = lax.broadcasted_iota(jnp.int32, (1, pw), 1) >= ch
    causal = ri >= cj
    strict = ri > cj
    diag_mask = (ri // INV_BLOCK) == (cj // INV_BLOCK)
    rt = lax.broadcasted_iota(jnp.int32, (ch, ch), 0)
    ct = lax.broadcasted_iota(jnp.int32, (ch, ch), 1)
    ltri = (rt >= ct).astype(F32)

    pairs = range(N_HEADS // 2)
    pcols = [slice(p * 2 * HEAD_D, (p + 1) * 2 * HEAD_D) for p in pairs]
    hcols = [slice(hd * HEAD_D, (hd + 1) * HEAD_D) for hd in range(N_HEADS)]

    def bd_pair(y):
        zero = jnp.zeros_like(y)
        return jnp.concatenate([jnp.where(hi, zero, y), jnp.where(hi, y, zero)], axis=0)

    def bd_wide(y):
        return _block_diag2(y[:, :HEAD_D], y[:, HEAD_D:])

    def mm_pairs(xs, ys):
        out = [jnp.dot(x.astype(BF16), bd_pair(y.astype(BF16)), preferred_element_type=F32)
               for x, y in zip(xs, ys)]
        fill()
        return out

    def lanes_of_pair(cols, p):
        return jnp.concatenate([jnp.broadcast_to(cols[:, 2 * p:2 * p + 1], (ch, HEAD_D)),
                                jnp.broadcast_to(cols[:, 2 * p + 1:2 * p + 2], (ch, HEAD_D))],
                               axis=1)

    def block_rows(blk):
        start = blk * ch
        return pl.ds(start if isinstance(start, int) else pl.multiple_of(start, ch), ch)

    per_seq = max(p1_chunks // nseg, 1) if nch_seg > 1 else 0

    def phase1(it, carry_):
        units = []
        pre = {}
        for j in range(p1_chunks):
            if nch_seg == 1:
                c = it * p1_chunks + j
            else:
                c = (j // per_seq) * nch_seg + it * per_seq + j % per_seq
            rows = block_rows(c)
            g_c = g_s[rows, :]
            beta_c = beta_s[rows, :]
            for p in pairs:
                units.append((j, rows, p))
                pre[j, p] = (k_s[rows, pcols[p]], q_s[rows, pcols[p]], v_s[rows, pcols[p]])
            gc = jnp.dot(ltri, g_c, preferred_element_type=F32,
                         precision=lax.Precision.HIGHEST)
            gc_rows = jnp.concatenate([gc, gc], axis=0).T
            glast = gc[ch - 1:ch, :]
            pre[j] = (c, gc, gc_rows, beta_c, jnp.exp(gc), jnp.exp(glast - gc), jnp.exp(glast))

        dmat, lhs, rk, rhs = [], [], [], []
        for j, rows, p in units:
            kp, qp, vp = pre[j, p]
            _, gc, gc_rows, beta_c, egc, _, _ = pre[j]
            g_col = jnp.where(hi, gc[:, 2 * p + 1:2 * p + 2], gc[:, 2 * p:2 * p + 1])
            g_row = jnp.where(hi_row, gc_rows[2 * p + 1:2 * p + 2, :], gc_rows[2 * p:2 * p + 1, :])
            dmat.append(jnp.where(causal, jnp.exp(jnp.where(causal, g_col - g_row, 0.0)), 0.0))
            kb = kp * lanes_of_pair(beta_c, p)
            lhs.append(jnp.concatenate([kb, qp], axis=0).astype(BF16))
            rk.append(bd_wide(kp.astype(BF16)))
            rhs.append((vp * lanes_of_pair(beta_c, p), kb * lanes_of_pair(egc, p)))
        kkqk = [lax.dot_general(a, b, (((1,), (1,)), ((), ())), preferred_element_type=F32)
                for a, b in zip(lhs, rk)]
        fill()
        a_mat = [jnp.where(strict, x[:ch] * d, 0.0) for x, d in zip(kkqk, dmat)]
        t_x = _inv_unit_lower(a_mat, diag_mask, nblk, mm_pairs)
        duw = [jnp.dot(t.astype(BF16),
                       jnp.concatenate([bd_wide(ru.astype(BF16)), bd_wide(rw.astype(BF16))], axis=1),
                       preferred_element_type=F32)
               for t, (ru, rw) in zip(t_x, rhs)]

        fill()
        for i, (j, rows, p) in enumerate(units):
            kp, qp = k_s[rows, pcols[p]], q_s[rows, pcols[p]]
            _, _, _, _, egc, ekd, _ = pre[j]
            qk_s[p, rows, :] = (kkqk[i][ch:] * dmat[i]).astype(BF16)
            u_s[rows, pcols[p]] = rhs[i][0] + duw[i][:, :2 * HEAD_D]
            w_s[rows, pcols[p]] = (rhs[i][1] + duw[i][:, 2 * HEAD_D:]).astype(BF16)
            qd_s[rows, pcols[p]] = (qp * lanes_of_pair(egc, p)).astype(BF16)
            kd_s[rows, pcols[p]] = kp * lanes_of_pair(ekd, p)
        for j in range(p1_chunks):
            egl_s[pl.ds(pre[j][0], 1), :] = pre[j][6]
        return carry_


    ngroups = nseg // P2_SEGS
    assert ngroups == 1 or nch_seg == 1

    def phase2(it, carry_):
        c, sg = (it, 0) if ngroups == 1 else (0, it)
        units = []
        egl = {}
        for i in range(P2_SEGS):
            sq = sg * P2_SEGS + i
            blk = sq * nch_seg + c
            rows = block_rows(blk)
            egl[i] = egl_s[pl.ds(blk, 1), :]
            units += [(i, sq, rows, p) for p in pairs]
        s_old = {(i, hd): s_ref[sq, hd] for i, sq, _, p in units for hd in (2 * p, 2 * p + 1)}
        wq = [jnp.concatenate([w_s[rows, pcols[p]], qd_s[rows, pcols[p]]], axis=0)
              for _, _, rows, p in units]
        u = [u_s[rows, pcols[p]] for _, _, rows, p in units]
        qk = [qk_s[p, rows, :] for _, _, rows, p in units]
        kd_t = {(i, hd): kd_s[rows, hcols[hd]].T.astype(BF16)
                for i, _, rows, p in units for hd in (2 * p, 2 * p + 1)}
        s_bd = [_block_diag2(s_old[i, 2 * p].astype(BF16), s_old[i, 2 * p + 1].astype(BF16))
                for i, _, _, p in units]
        ws = [jnp.dot(a, sb, preferred_element_type=F32) for a, sb in zip(wq, s_bd)]
        v_new = [(a - b[:ch]).astype(BF16) for a, b in zip(u, ws)]
        o = [b[ch:] + jnp.dot(a, bd_wide(v), preferred_element_type=F32)
             for a, b, v in zip(qk, ws, v_new)]
        s_new = {}
        for n, (i, _, _, p) in enumerate(units):
            for half, hd in enumerate((2 * p, 2 * p + 1)):
                s_new[i, hd] = s_old[i, hd] * egl[i][:, hd:hd + 1] + jnp.dot(
                    kd_t[i, hd], v_new[n][:, half * HEAD_D:(half + 1) * HEAD_D],
                    preferred_element_type=F32)
        for n, (i, sq, rows, p) in enumerate(units):
            o_s[rows, pcols[p]] = o[n]
            s_ref[sq, 2 * p] = s_new[i, 2 * p]
            s_ref[sq, 2 * p + 1] = s_new[i, 2 * p + 1]
        return carry_

    n_p1, n_p2 = nch // p1_chunks, nch_seg * ngroups
    phase1(0, 0)
    done2 = 0
    for it1 in range(1, n_p1 + 1):
        if it1 < n_p1:
            phase1(it1, 0)
        ready2 = n_p2 * it1 // n_p1
        for it2 in range(done2, ready2):
            phase2(it2, 0)
        done2 = ready2
    while fill_queue:
        fill()

    za = jnp.dot(hbf_s[...], w_za[...], preferred_element_type=F32)
    ya = []
    for hd in range(N_HEADS):
        hc = slice(hd * HEAD_D, (hd + 1) * HEAD_D)
        oh = o_s[:, hc]
        ms = jnp.mean(oh * oh, axis=-1, keepdims=True)
        ya.append((oh * lax.rsqrt(ms + RMS_EPS) * norm_a_g[...] * _silu(za[:, hc])).astype(BF16))
    pa = jnp.dot(jnp.concatenate(ya, axis=1), w_proj_a[...], preferred_element_type=F32)

    gate_a = _sigmoid(jnp.dot(hbf_s[...], w_gate[:, :D_MODEL], preferred_element_type=F32))
    merged = gate_a * pa + pbg_s[...]
    h1 = _layer_norm(ALPHA * y_ref[...].reshape(tt, D_MODEL) + jnp.dot(merged.astype(BF16), w_out[...],
                                            preferred_element_type=F32),
                     ln1_g[...], ln1_b[...])
    h1_bf = h1.astype(BF16)
    ple = _sigmoid(jnp.dot(h1_bf, w_ple_gate[...], preferred_element_type=F32)) * jnp.dot(
        p_ref[0].reshape(tt, P_DIM).astype(BF16), w_ple[...], preferred_element_type=F32)
    y_ref[...] = _layer_norm(ALPHA * h1 + ple, ln2_g[...], ln2_b[...]).reshape(nseg, seg, D_MODEL)


def _encode(x, p, init_state, weights, *, nseg, seg, ch):
    nseq, t_len, _ = x.shape
    zero_init = init_state is None
    state_shapes = ((nseq, SUBLANES, QKV_DIM), (nseq, N_HEADS, HEAD_D, HEAD_D),
                    (nseq, SUBLANES, WIDTH_B))
    nb, nt = nseq // nseg, t_len // seg
    carry = nt > 1
    assert nseq % nseg == 0 and t_len % seg == 0 and seg % ch == 0 and seg % SUBLANES == 0
    assert ch & (ch - 1) == 0 and ch % INV_BLOCK == 0 and nseg % P2_SEGS == 0
    tt = nseg * seg
    nch = tt // ch
    cfg = (nseg, seg, ch, carry, zero_init)

    def tile_spec(width):
        return pl.BlockSpec((nseg, seg, width), lambda b, t: (b, t, 0))

    def state_spec(shape):
        nd = len(shape)
        return pl.BlockSpec((nseg,) + tuple(shape[1:]), lambda b, t: (b,) + (0,) * (nd - 1),
                            pipeline_mode=pl.Buffered(1))

    def const_spec(arr):
        nd = arr.ndim
        return pl.BlockSpec(arr.shape, lambda b, t: (0,) * nd, pipeline_mode=pl.Buffered(1))

    def cols_spec(arr, width, offset):
        assert offset % width == 0
        return pl.BlockSpec((arr.shape[0], width), lambda b, t: (0, offset // width),
                            pipeline_mode=pl.Buffered(1))

    w_in_packed, others = weights
    p_spec = pl.BlockSpec((1, nseg, seg, P_DIM), lambda b, t: (0, b, t, 0))
    in_specs = [tile_spec(D_MODEL), p_spec]
    operands = [x, p]
    if not zero_init:
        assert tuple(a.shape for a in init_state) == state_shapes
        in_specs += [state_spec(shape) for shape in state_shapes]
        operands += list(init_state)
    for name, arr in others:
        if name == "w_in":
            for width, offset in PACKED_COLS:
                in_specs.append(cols_spec(w_in_packed, width, offset))
                operands.append(w_in_packed)
        else:
            in_specs.append(const_spec(arr))
            operands.append(arr)
    out_shape = (jax.ShapeDtypeStruct(x.shape, F32),) + tuple(
        jax.ShapeDtypeStruct(shape, F32) for shape in state_shapes)
    out_specs = (tile_spec(D_MODEL),) + tuple(state_spec(shape) for shape in state_shapes)
    scratch = [
        pltpu.VMEM((tt, D_MODEL), BF16),
        pltpu.VMEM((tt, KEY_DIM), F32),
        pltpu.VMEM((tt, KEY_DIM), F32),
        pltpu.VMEM((tt, KEY_DIM), F32),
        pltpu.VMEM((tt, LANES), F32),
        pltpu.VMEM((tt, LANES), F32),
        pltpu.VMEM((max(nch, SUBLANES), LANES), F32),
        pltpu.VMEM((tt, KEY_DIM), BF16),
        pltpu.VMEM((tt, KEY_DIM), BF16),
        pltpu.VMEM((N_HEADS // 2, tt, 2 * ch), BF16),
        pltpu.VMEM((SUBLANES + seg, QKV_CHUNK), F32),
        pltpu.VMEM((tt, D_MODEL), F32),
        pltpu.VMEM((tt, WIDTH_B), BF16),
    ]
    return pl.pallas_call(
        functools.partial(_layer_kernel, cfg),
        grid=(nb, nt),
        in_specs=in_specs,
        out_specs=out_specs,
        out_shape=out_shape,
        scratch_shapes=scratch,
        compiler_params=pltpu.CompilerParams(
            dimension_semantics=("arbitrary", "arbitrary"),
            vmem_limit_bytes=VMEM_LIMIT_BYTES),
        name=f"gdn_shortconv_layer_n{nseg}_t{seg}_c{ch}",
    )(*operands)


def _pack_kernel(a_ref, b_ref, o_ref):
    j = pl.program_id(0)
    n_plain = OFF_BETA // PACK_W
    n_shift = (IN_DIM - OFF_BB) // PACK_W
    shift = OFF_BB - OFF_BETA
    lane = lax.broadcasted_iota(jnp.int32, (1, LANES), 1)
    nsub = PACK_W // LANES

    @pl.when(j < n_plain)
    def _():
        o_ref[...] = a_ref[...].astype(BF16)

    @pl.when((j >= n_plain) & (j < n_plain + n_shift))
    def _():
        groups = [a_ref[:, k * LANES:(k + 1) * LANES] for k in range(nsub)] + [b_ref[...]]
        rolled = [pltpu.roll(g, LANES - shift, 1) for g in groups]
        for k in range(nsub):
            o_ref[:, k * LANES:(k + 1) * LANES] = jnp.where(
                lane < LANES - shift, rolled[k], rolled[k + 1]).astype(BF16)

    @pl.when(j == n_plain + n_shift)
    def _():
        first = a_ref[:, :LANES]
        o_ref[:, :LANES] = jnp.where(lane < N_HEADS, first, 0.0).astype(BF16)
        o_ref[:, LANES:2 * LANES] = jnp.where(
            lane < N_HEADS, pltpu.roll(first, LANES - N_HEADS, 1), 0.0).astype(BF16)
        o_ref[:, 2 * LANES:] = jnp.zeros((o_ref.shape[0], PACK_W - 2 * LANES), BF16)


def _pack_w_in(w):
    n_plain = OFF_BETA // PACK_W
    n_shift = (IN_DIM - OFF_BB) // PACK_W
    assert OFF_BETA % PACK_W == 0 and (IN_DIM - OFF_BB) % PACK_W == 0 and OFF_DECAY - OFF_BETA == N_HEADS
    nsub = PACK_W // LANES

    def a_idx(j):
        return (0, 0, jnp.where(j < n_plain + n_shift, j, n_plain))

    def b_idx(j):
        return (0, 0, jnp.where((j >= n_plain) & (j < n_plain + n_shift), nsub * (j + 1), 0))

    return pl.pallas_call(
        _pack_kernel,
        grid=(n_plain + n_shift + 1,),
        in_specs=[pl.BlockSpec((None, D_MODEL, PACK_W), a_idx),
                  pl.BlockSpec((None, D_MODEL, LANES), b_idx)],
        out_specs=pl.BlockSpec((D_MODEL, PACK_W), lambda j: (0, j)),
        out_shape=jax.ShapeDtypeStruct((D_MODEL, PACKED_DIM), BF16),
        compiler_params=pltpu.CompilerParams(dimension_semantics=("arbitrary",)),
        name="pack_w_in",
    )(w, w)


def _pad_rows_front(a, rows):
    pad = [(0, 0)] * a.ndim
    pad[-2] = (rows - a.shape[-2], 0)
    return jnp.pad(a, pad)


def _pad_lanes(a):
    pad = [(0, 0)] * a.ndim
    pad[-1] = (0, LANES - a.shape[-1])
    return jnp.pad(a, pad)


def kernel(x_prompt, x_sample, state_conv_a, state_gdn, state_conv_b, p_prompt, p_sample, ln_in_g, ln_in_b, w_in, w_conv_a, a_log, dt_bias, norm_a_g, w_conv_b, w_proj_a, w_proj_b, w_out, ln1_g, ln1_b, w_ple, w_ple_gate, ln2_g, ln2_b):
    assert w_in.shape[0] == DEPTH == 1
    row = lambda v: v.reshape(1, -1).astype(F32)
    w_in_packed = _pack_w_in(w_in)
    others = (
        ("ln_in_g", row(ln_in_g)), ("ln_in_b", row(ln_in_b)),
        ("w_in", None),
        ("w_conv_a", w_conv_a[0].astype(F32)),
        ("a_log", _pad_lanes(row(a_log[0]))), ("dt_bias", _pad_lanes(row(dt_bias[0]))),
        ("norm_a_g", row(norm_a_g[0])),
        ("w_conv_b", w_conv_b[0].astype(F32)),
        ("w_proj_a", w_proj_a[0].astype(BF16)), ("w_proj_b", w_proj_b[0].astype(BF16)),
        ("w_out", w_out[0].astype(BF16)),
        ("ln1_g", row(ln1_g[0])), ("ln1_b", row(ln1_b[0])),
        ("w_ple", w_ple[0].astype(BF16)), ("w_ple_gate", w_ple_gate[0].astype(BF16)),
        ("ln2_g", row(ln2_g[0])), ("ln2_b", row(ln2_b[0])),
    )
    weights = (w_in_packed, others)

    bp, seq, _ = x_prompt.shape
    y_p, ca_p, s_p, cb_p = _encode(
        x_prompt, p_prompt, None, weights, nseg=bp, seg=PROMPT_TILE, ch=PROMPT_CHUNK)

    bs, ts, _ = x_sample.shape
    y_s, ca_s, s_s, cb_s = _encode(
        x_sample, p_sample,
        (_pad_rows_front(state_conv_a[0], SUBLANES), state_gdn[0].astype(F32),
         _pad_rows_front(state_conv_b[0], SUBLANES)),
        weights, nseg=SAMPLE_TILE_SEQS, seg=ts, ch=ts)

    na, nb = CONV_A - 1, CONV_B - 1
    return (y_p, y_s,
            ca_p[None, :, SUBLANES - na:], s_p[None], cb_p[None, :, SUBLANES - nb:],
            ca_s[None, :, SUBLANES - na:], s_s[None].astype(state_gdn.dtype),
            cb_s[None, :, SUBLANES - nb:])
```

```python
import functools

import jax
import jax.numpy as jnp
from jax import lax
from jax.experimental import pallas as pl
from jax.experimental.pallas import tpu as pltpu

D_MODEL = 1024
N_HEADS = 8
HEAD_D = 128
KEY_DIM = N_HEADS * HEAD_D
QKV_DIM = 3 * KEY_DIM
WIDTH_B = D_MODEL
P_DIM = 256
CONV_A = 4
CONV_B = 3
PROMPT_CHUNK = 64
DEPTH = 1
ALPHA = (2 * DEPTH) ** 0.25
LN_EPS = 1e-5
RMS_EPS = 1e-6
L2_EPS = 1e-6

OFF_ZA = QKV_DIM
OFF_BETA = OFF_ZA + KEY_DIM
OFF_DECAY = OFF_BETA + N_HEADS
OFF_BB = OFF_DECAY + N_HEADS
OFF_CB = OFF_BB + WIDTH_B
OFF_UB = OFF_CB + WIDTH_B
OFF_ZB = OFF_UB + WIDTH_B
OFF_GATE = OFF_ZB + WIDTH_B
IN_DIM = OFF_GATE + 2 * D_MODEL

SUBLANES = 8
LANES = 128
INV_BLOCK = 16
PROMPT_TILE = 256
SAMPLE_TILE_SEQS = 8
P1_CHUNKS = 4
QKV_CHUNK = 256
P2_SEGS = 2
VMEM_LIMIT_BYTES = 127 * 512 * 1024

PACK_W = 512
PACKED_DIM = IN_DIM - (OFF_BB - OFF_BETA) + 2 * LANES
PACKED_COLS = ((QKV_DIM, 0), (KEY_DIM, OFF_ZA), (LANES, OFF_BETA + 4 * WIDTH_B + 2 * D_MODEL),
               (LANES, OFF_BETA + 4 * WIDTH_B + 2 * D_MODEL + LANES), (4 * WIDTH_B, OFF_BETA),
               (2 * D_MODEL, OFF_BETA + 4 * WIDTH_B))

F32 = jnp.float32
NEG_LOG2_E = -1.4426950408889634
BF16 = jnp.bfloat16


def _sigmoid(x):
    return 1.0 / (1.0 + jnp.exp2(x * NEG_LOG2_E))


def _silu(x):
    return x * _sigmoid(x)


def _softplus(x):
    return jnp.maximum(x, 0.0) + jnp.log(1.0 + jnp.exp(-jnp.abs(x)))


def _layer_norm(x, g, b):
    mu = jnp.mean(x, axis=-1, keepdims=True)
    xc = x - mu
    var = jnp.mean(xc * xc, axis=-1, keepdims=True)
    return xc * lax.rsqrt(var + LN_EPS) * g + b


def _mm(a, b):
    return jnp.dot(a.astype(BF16), b.astype(BF16), preferred_element_type=F32)


def _causal_conv(seg, hist8, w_ref, ntaps, buf):
    n = seg.shape[0]
    buf[0:SUBLANES, :] = hist8
    buf[SUBLANES:SUBLANES + n, :] = seg
    acc = seg * w_ref[ntaps - 1:ntaps, :]
    for s in range(1, ntaps):
        acc = acc + buf[SUBLANES - s:SUBLANES - s + n, :] * w_ref[ntaps - 1 - s:ntaps - s, :]
    return acc


def _block_diag2(y1, y2):
    z = jnp.zeros_like(y1)
    return jnp.concatenate([jnp.concatenate([y1, z], axis=1), jnp.concatenate([z, y2], axis=1)],
                           axis=0)


def _inv_unit_lower(a_list, diag_mask, nblk, mm_each):
    d = [jnp.where(diag_mask, a, 0.0) for a in a_list]
    low = [a - x for a, x in zip(a_list, d)]
    d2 = mm_each(d, d)
    d4 = mm_each(d2, d2)
    dd2 = mm_each(d, d2)
    d8 = mm_each(d4, d4)
    x1 = [b - a - c for a, b, c in zip(d, d2, dd2)]
    d4d8 = mm_each(d4, d8)
    x2 = [a + b + c for a, b, c in zip(d4, d8, d4d8)]
    x1x2 = mm_each(x1, x2)
    xd = [a + b + c for a, b, c in zip(x1, x2, x1x2)]
    xdl = mm_each(xd, low)
    n = [a + b for a, b in zip(low, xdl)]
    if nblk == 2:
        xq = [-a for a in n]
    elif nblk == 4:
        n2 = mm_each(n, n)
        nn2 = mm_each(n, n2)
        xq = [b - a - c for a, b, c in zip(n, n2, nn2)]
    else:
        raise NotImplementedError(nblk)
    xqxd = mm_each(xq, xd)
    return [a + b + c for a, b, c in zip(xq, xd, xqxd)]


def _layer_kernel(cfg, x_ref, p_ref, *refs):
    init_refs, refs = (None, refs) if cfg[4] else (refs[:3], refs[3:])
    _layer_body(cfg, x_ref, p_ref, init_refs, *refs)


def _layer_body(cfg,
                  x_ref, p_ref, init_refs,
                  ln_in_g, ln_in_b, w_qkv, w_za, w_beta, w_dec, w_b, w_gate,
                  w_conv_a, a_log, dt_bias, norm_a_g, w_conv_b,
                  w_proj_a, w_proj_b, w_out, ln1_g, ln1_b, w_ple, w_ple_gate, ln2_g, ln2_b,
                  y_ref, ca_ref, s_ref, cb_ref,
                  hbf_s, q_s, k_s, v_s, beta_s, g_s, egl_s, w_s, qd_s, qk_s, cbuf_s, pbg_s, ybf_s):
    u_s, kd_s, o_s = v_s, k_s, q_s
    nseg, seg, ch, carry, zero_init = cfg
    tt = nseg * seg
    nch = tt // ch
    nch_seg = seg // ch
    nblk = ch // INV_BLOCK
    p1_chunks = min(P1_CHUNKS, nch)
    t_idx = pl.program_id(1)

    def _seed():
        if zero_init:
            ca_ref[...] = jnp.zeros(ca_ref.shape, F32)
            cb_ref[...] = jnp.zeros(cb_ref.shape, F32)
            s_ref[...] = jnp.zeros(s_ref.shape, F32)
        else:
            hista_ref, sin_ref, histb_ref = init_refs
            ca_ref[...] = hista_ref[...]
            cb_ref[...] = histb_ref[...]
            s_ref[...] = sin_ref[...]

    if carry:
        pl.when(t_idx == 0)(_seed)
    else:
        _seed()

    def conv_tile(pre, hist_ref, col, w_ref, ntaps):
        outs = []
        for i in range(nseg):
            part = pre[i * seg:(i + 1) * seg]
            outs.append(_causal_conv(part, hist_ref[i, :, col], w_ref.at[:, col], ntaps,
                                     cbuf_s))
            hist_ref[i, :, col] = part[seg - SUBLANES:]
        return outs[0] if nseg == 1 else jnp.concatenate(outs, axis=0)

    h = _layer_norm(x_ref[...].reshape(tt, D_MODEL), ln_in_g[...], ln_in_b[...])
    y_ref[...] = h.reshape(nseg, seg, D_MODEL)
    hbf_s[...] = h.astype(BF16)

    for j in range(QKV_DIM // QKV_CHUNK):
        col = slice(j * QKV_CHUNK, (j + 1) * QKV_CHUNK)
        grp, sub = divmod(j, KEY_DIM // QKV_CHUNK)
        dst = (q_s, k_s, v_s)[grp]
        pre = jnp.dot(hbf_s[...], w_qkv[:, col], preferred_element_type=F32)
        act = _silu(conv_tile(pre, ca_ref, col, w_conv_a, CONV_A))
        for hh in range(QKV_CHUNK // HEAD_D):
            hc = slice(sub * QKV_CHUNK + hh * HEAD_D, sub * QKV_CHUNK + (hh + 1) * HEAD_D)
            xh = act[:, hh * HEAD_D:(hh + 1) * HEAD_D]
            if grp == 2:
                dst[:, hc] = xh
            else:
                scale = HEAD_D ** -0.5 if grp == 0 else 1.0
                ss = jnp.sum(xh * xh, axis=-1, keepdims=True)
                dst[:, hc] = xh * (lax.rsqrt(ss + L2_EPS) * scale)

    beta_s[...] = _sigmoid(jnp.dot(hbf_s[...], w_beta[...], preferred_element_type=F32))
    zdec = jnp.dot(hbf_s[...], w_dec[...], preferred_element_type=F32)
    g_s[...] = -jnp.exp(a_log[...]) * _softplus(zdec + dt_bias[...])

    fill_queue = []
    held = {}

    def fill():
        if fill_queue:
            fill_queue.pop(0)()

    def b_proj(k, cj):
        return jnp.dot(hbf_s[...], w_b[:, k * WIDTH_B + cj.start:k * WIDTH_B + cj.stop],
                       preferred_element_type=F32)

    def b_tile_steps(cj):
        def step_c():
            held["cb"] = b_proj(1, cj)

        def step_conv():
            cu = held.pop("cb") * b_proj(2, cj)
            pbg_s[:, cj] = conv_tile(cu, cb_ref, cj, w_conv_b, CONV_B)

        def step_b():
            pbg_s[:, cj] = b_proj(0, cj) * pbg_s[:, cj]

        def step_gate():
            ybf_s[:, cj] = (pbg_s[:, cj] * _silu(b_proj(3, cj))).astype(BF16)

        return [step_c, step_conv, step_b, step_gate]

    def b_out_steps(cj):
        def step_proj():
            pbg_s[:, cj] = jnp.dot(ybf_s[...], w_proj_b[:, cj], preferred_element_type=F32)

        def step_merge_gate():
            gate_b = _sigmoid(jnp.dot(hbf_s[...], w_gate[:, D_MODEL + cj.start:D_MODEL + cj.stop],
                                      preferred_element_type=F32))
            pbg_s[:, cj] = gate_b * pbg_s[:, cj]

        return [step_proj, step_merge_gate]

    b_tiles = [slice(j * QKV_CHUNK, (j + 1) * QKV_CHUNK) for j in range(WIDTH_B // QKV_CHUNK)]
    for cj in b_tiles:
        fill_queue += b_tile_steps(cj)
    for cj in b_tiles:
        fill_queue += b_out_steps(cj)

    pw = 2 * ch
    ri = lax.broadcasted_iota(jnp.int32, (ch, pw), 0)
    cn = lax.broadcasted_iota(jnp.int32, (ch, pw), 1)
    cj = cn & (ch - 1)
    hi = cn >= ch
    hi_row = lax.broadcasted_iota(jnp.int32, (1, pw), 1) >= ch
    causal = ri >= cj
    strict = ri > cj
    diag_mask = (ri // INV_BLOCK) == (cj // INV_BLOCK)
    rt = lax.broadcasted_iota(jnp.int32, (ch, ch), 0)
    ct = lax.broadcasted_iota(jnp.int32, (ch, ch), 1)
    ltri = (rt >= ct).astype(F32)

    pairs = range(N_HEADS // 2)
    pcols = [slice(p * 2 * HEAD_D, (p + 1) * 2 * HEAD_D) for p in pairs]
    hcols = [slice(hd * HEAD_D, (hd + 1) * HEAD_D) for hd in range(N_HEADS)]

    def bd_pair(y):
        zero = jnp.zeros_like(y)
        return jnp.concatenate([jnp.where(hi, zero, y), jnp.where(hi, y, zero)], axis=0)

    def bd_wide(y):
        return _block_diag2(y[:, :HEAD_D], y[:, HEAD_D:])

    def mm_pairs(xs, ys):
        out = [jnp.dot(x.astype(BF16), bd_pair(y.astype(BF16)), preferred_element_type=F32)
               for x, y in zip(xs, ys)]
        fill()
        return out

    def lanes_of_pair(cols, p):
        return jnp.concatenate([jnp.broadcast_to(cols[:, 2 * p:2 * p + 1], (ch, HEAD_D)),
                                jnp.broadcast_to(cols[:, 2 * p + 1:2 * p + 2], (ch, HEAD_D))],
                               axis=1)

    def block_rows(blk):
        start = blk * ch
        return pl.ds(start if isinstance(start, int) else pl.multiple_of(start, ch), ch)

    per_seq = max(p1_chunks // nseg, 1) if nch_seg > 1 else 0

    def phase1(it, carry_):
        units = []
        pre = {}
        for j in range(p1_chunks):
            if nch_seg == 1:
                c = it * p1_chunks + j
            else:
                c = (j // per_seq) * nch_seg + it * per_seq + j % per_seq
            rows = block_rows(c)
            g_c = g_s[rows, :]
            beta_c = beta_s[rows, :]
            for p in pairs:
                units.append((j, rows, p))
                pre[j, p] = (k_s[rows, pcols[p]], q_s[rows, pcols[p]], v_s[rows, pcols[p]])
            gc = jnp.dot(ltri, g_c, preferred_element_type=F32,
                         precision=lax.Precision.HIGHEST)
            gc_rows = jnp.concatenate([gc, gc], axis=0).T
            glast = gc[ch - 1:ch, :]
            pre[j] = (c, gc, gc_rows, beta_c, jnp.exp(gc), jnp.exp(glast - gc), jnp.exp(glast))

        dmat, lhs, rk, rhs = [], [], [], []
        for j, rows, p in units:
            kp, qp, vp = pre[j, p]
            _, gc, gc_rows, beta_c, egc, _, _ = pre[j]
            g_col = jnp.where(hi, gc[:, 2 * p + 1:2 * p + 2], gc[:, 2 * p:2 * p + 1])
            g_row = jnp.where(hi_row, gc_rows[2 * p + 1:2 * p + 2, :], gc_rows[2 * p:2 * p + 1, :])
            dmat.append(jnp.where(causal, jnp.exp(jnp.where(causal, g_col - g_row, 0.0)), 0.0))
            kb = kp * lanes_of_pair(beta_c, p)
            lhs.append(jnp.concatenate([kb, qp], axis=0).astype(BF16))
            rk.append(bd_wide(kp.astype(BF16)))
            rhs.append((vp * lanes_of_pair(beta_c, p), kb * lanes_of_pair(egc, p)))
        kkqk = [lax.dot_general(a, b, (((1,), (1,)), ((), ())), preferred_element_type=F32)
                for a, b in zip(lhs, rk)]
        fill()
        a_mat = [jnp.where(strict, x[:ch] * d, 0.0) for x, d in zip(kkqk, dmat)]
        t_x = _inv_unit_lower(a_mat, diag_mask, nblk, mm_pairs)
        duw = [jnp.dot(t.astype(BF16),
                       jnp.concatenate([bd_wide(ru.astype(BF16)), bd_wide(rw.astype(BF16))], axis=1),
                       preferred_element_type=F32)
               for t, (ru, rw) in zip(t_x, rhs)]

        fill()
        for i, (j, rows, p) in enumerate(units):
            kp, qp = k_s[rows, pcols[p]], q_s[rows, pcols[p]]
            _, _, _, _, egc, ekd, _ = pre[j]
            qk_s[p, rows, :] = (kkqk[i][ch:] * dmat[i]).astype(BF16)
            u_s[rows, pcols[p]] = rhs[i][0] + duw[i][:, :2 * HEAD_D]
            w_s[rows, pcols[p]] = (rhs[i][1] + duw[i][:, 2 * HEAD_D:]).astype(BF16)
            qd_s[rows, pcols[p]] = (qp * lanes_of_pair(egc, p)).astype(BF16)
            kd_s[rows, pcols[p]] = kp * lanes_of_pair(ekd, p)
        for j in range(p1_chunks):
            egl_s[pl.ds(pre[j][0], 1), :] = pre[j][6]
        return carry_


    ngroups = nseg // P2_SEGS
    assert ngroups == 1 or nch_seg == 1

    def phase2(it, carry_):
        c, sg = (it, 0) if ngroups == 1 else (0, it)
        units = []
        egl = {}
        for i in range(P2_SEGS):
            sq = sg * P2_SEGS + i
            blk = sq * nch_seg + c
            rows = block_rows(blk)
            egl[i] = egl_s[pl.ds(blk, 1), :]
            units += [(i, sq, rows, p) for p in pairs]
        s_old = {(i, hd): s_ref[sq, hd] for i, sq, _, p in units for hd in (2 * p, 2 * p + 1)}
        wq = [jnp.concatenate([w_s[rows, pcols[p]], qd_s[rows, pcols[p]]], axis=0)
              for _, _, rows, p in units]
        u = [u_s[rows, pcols[p]] for _, _, rows, p in units]
        qk = [qk_s[p, rows, :] for _, _, rows, p in units]
        kd_t = {(i, hd): kd_s[rows, hcols[hd]].T.astype(BF16)
                for i, _, rows, p in units for hd in (2 * p, 2 * p + 1)}
        s_bd = [_block_diag2(s_old[i, 2 * p].astype(BF16), s_old[i, 2 * p + 1].astype(BF16))
                for i, _, _, p in units]
        ws = [jnp.dot(a, sb, preferred_element_type=F32) for a, sb in zip(wq, s_bd)]
        v_new = [(a - b[:ch]).astype(BF16) for a, b in zip(u, ws)]
        o = [b[ch:] + jnp.dot(a, bd_wide(v), preferred_element_type=F32)
             for a, b, v in zip(qk, ws, v_new)]
        s_new = {}
        for n, (i, _, _, p) in enumerate(units):
            for half, hd in enumerate((2 * p, 2 * p + 1)):
                s_new[i, hd] = s_old[i, hd] * egl[i][:, hd:hd + 1] + jnp.dot(
                    kd_t[i, hd], v_new[n][:, half * HEAD_D:(half + 1) * HEAD_D],
                    preferred_element_type=F32)
        for n, (i, sq, rows, p) in enumerate(units):
            o_s[rows, pcols[p]] = o[n]
            s_ref[sq, 2 * p] = s_new[i, 2 * p]
            s_ref[sq, 2 * p + 1] = s_new[i, 2 * p + 1]
        return carry_

    n_p1, n_p2 = nch // p1_chunks, nch_seg * ngroups
    phase1(0, 0)
    done2 = 0
    for it1 in range(1, n_p1 + 1):
        if it1 < n_p1:
            phase1(it1, 0)
        ready2 = n_p2 * it1 // n_p1
        for it2 in range(done2, ready2):
            phase2(it2, 0)
        done2 = ready2
    while fill_queue:
        fill()

    za = jnp.dot(hbf_s[...], w_za[...], preferred_element_type=F32)
    ya = []
    for hd in range(N_HEADS):
        hc = slice(hd * HEAD_D, (hd + 1) * HEAD_D)
        oh = o_s[:, hc]
        ms = jnp.mean(oh * oh, axis=-1, keepdims=True)
        ya.append((oh * lax.rsqrt(ms + RMS_EPS) * norm_a_g[...] * _silu(za[:, hc])).astype(BF16))
    pa = jnp.dot(jnp.concatenate(ya, axis=1), w_proj_a[...], preferred_element_type=F32)

    gate_a = _sigmoid(jnp.dot(hbf_s[...], w_gate[:, :D_MODEL], preferred_element_type=F32))
    merged = gate_a * pa + pbg_s[...]
    h1 = _layer_norm(ALPHA * y_ref[...].reshape(tt, D_MODEL) + jnp.dot(merged.astype(BF16), w_out[...],
                                            preferred_element_type=F32),
                     ln1_g[...], ln1_b[...])
    h1_bf = h1.astype(BF16)
    ple = _sigmoid(jnp.dot(h1_bf, w_ple_gate[...], preferred_element_type=F32)) * jnp.dot(
        p_ref[0].reshape(tt, P_DIM).astype(BF16), w_ple[...], preferred_element_type=F32)
    y_ref[...] = _layer_norm(ALPHA * h1 + ple, ln2_g[...], ln2_b[...]).reshape(nseg, seg, D_MODEL)


def _encode(x, p, init_state, weights, *, nseg, seg, ch):
    nseq, t_len, _ = x.shape
    zero_init = init_state is None
    state_shapes = ((nseq, SUBLANES, QKV_DIM), (nseq, N_HEADS, HEAD_D, HEAD_D),
                    (nseq, SUBLANES, WIDTH_B))
    nb, nt = nseq // nseg, t_len // seg
    carry = nt > 1
    assert nseq % nseg == 0 and t_len % seg == 0 and seg % ch == 0 and seg % SUBLANES == 0
    assert ch & (ch - 1) == 0 and ch % INV_BLOCK == 0 and nseg % P2_SEGS == 0
    tt = nseg * seg
    nch = tt // ch
    cfg = (nseg, seg, ch, carry, zero_init)

    def tile_spec(width):
        return pl.BlockSpec((nseg, seg, width), lambda b, t: (b, t, 0))

    def state_spec(shape):
        nd = len(shape)
        return pl.BlockSpec((nseg,) + tuple(shape[1:]), lambda b, t: (b,) + (0,) * (nd - 1),
                            pipeline_mode=pl.Buffered(1))

    def const_spec(arr):
        nd = arr.ndim
        return pl.BlockSpec(arr.shape, lambda b, t: (0,) * nd, pipeline_mode=pl.Buffered(1))

    def cols_spec(arr, width, offset):
        assert offset % width == 0
        return pl.BlockSpec((arr.shape[0], width), lambda b, t: (0, offset // width),
                            pipeline_mode=pl.Buffered(1))

    w_in_packed, others = weights
    p_spec = pl.BlockSpec((1, nseg, seg, P_DIM), lambda b, t: (0, b, t, 0))
    in_specs = [tile_spec(D_MODEL), p_spec]
    operands = [x, p]
    if not zero_init:
        assert tuple(a.shape for a in init_state) == state_shapes
        in_specs += [state_spec(shape) for shape in state_shapes]
        operands += list(init_state)
    for name, arr in others:
        if name == "w_in":
            for width, offset in PACKED_COLS:
                in_specs.append(cols_spec(w_in_packed, width, offset))
                operands.append(w_in_packed)
        else:
            in_specs.append(const_spec(arr))
            operands.append(arr)
    out_shape = (jax.ShapeDtypeStruct(x.shape, F32),) + tuple(
        jax.ShapeDtypeStruct(shape, F32) for shape in state_shapes)
    out_specs = (tile_spec(D_MODEL),) + tuple(state_spec(shape) for shape in state_shapes)
    scratch = [
        pltpu.VMEM((tt, D_MODEL), BF16),
        pltpu.VMEM((tt, KEY_DIM), F32),
        pltpu.VMEM((tt, KEY_DIM), F32),
        pltpu.VMEM((tt, KEY_DIM), F32),
        pltpu.VMEM((tt, LANES), F32),
        pltpu.VMEM((tt, LANES), F32),
        pltpu.VMEM((max(nch, SUBLANES), LANES), F32),
        pltpu.VMEM((tt, KEY_DIM), BF16),
        pltpu.VMEM((tt, KEY_DIM), BF16),
        pltpu.VMEM((N_HEADS // 2, tt, 2 * ch), BF16),
        pltpu.VMEM((SUBLANES + seg, QKV_CHUNK), F32),
        pltpu.VMEM((tt, D_MODEL), F32),
        pltpu.VMEM((tt, WIDTH_B), BF16),
    ]
    return pl.pallas_call(
        functools.partial(_layer_kernel, cfg),
        grid=(nb, nt),
        in_specs=in_specs,
        out_specs=out_specs,
        out_shape=out_shape,
        scratch_shapes=scratch,
        compiler_params=pltpu.CompilerParams(
            dimension_semantics=("arbitrary", "arbitrary"),
            vmem_limit_bytes=VMEM_LIMIT_BYTES),
        name=f"gdn_shortconv_layer_n{nseg}_t{seg}_c{ch}",
    )(*operands)


def _pack_kernel(a_ref, b_ref, o_ref):
    j = pl.program_id(0)
    n_plain = OFF_BETA // PACK_W
    n_shift = (IN_DIM - OFF_BB) // PACK_W
    shift = OFF_BB - OFF_BETA
    row = lax.broadcasted_iota(jnp.int32, (LANES, 1), 0)

    @pl.when(j < n_plain)
    def _():
        o_ref[...] = a_ref[...].T.astype(BF16)

    @pl.when((j >= n_plain) & (j < n_plain + n_shift))
    def _():
        o_ref[...] = jnp.concatenate([a_ref[shift:, :], b_ref[...]], axis=0).T.astype(BF16)

    @pl.when(j == n_plain + n_shift)
    def _():
        beta = jnp.where(row < N_HEADS, a_ref[0:LANES, :], 0.0)
        decay = jnp.where(row < N_HEADS, a_ref[N_HEADS:N_HEADS + LANES, :], 0.0)
        o_ref[:, :LANES] = beta.T.astype(BF16)
        o_ref[:, LANES:2 * LANES] = decay.T.astype(BF16)
        o_ref[:, 2 * LANES:] = jnp.zeros((o_ref.shape[0], PACK_W - 2 * LANES), BF16)


def _pack_w_in(w_t):
    n_plain = OFF_BETA // PACK_W
    n_shift = (IN_DIM - OFF_BB) // PACK_W
    shift = OFF_BB - OFF_BETA
    assert OFF_BETA % PACK_W == 0 and (IN_DIM - OFF_BB) % PACK_W == 0
    assert OFF_DECAY - OFF_BETA == N_HEADS == SUBLANES and shift % SUBLANES == 0

    def a_idx(j):
        return (0, jnp.where(j < n_plain + n_shift, j, n_plain), 0)

    def b_idx(j):
        return (0, jnp.where((j >= n_plain) & (j < n_plain + n_shift), (j + 1) * (PACK_W // shift), 0), 0)

    return pl.pallas_call(
        _pack_kernel,
        grid=(n_plain + n_shift + 1,),
        in_specs=[pl.BlockSpec((None, PACK_W, D_MODEL), a_idx),
                  pl.BlockSpec((None, shift, D_MODEL), b_idx)],
        out_specs=pl.BlockSpec((D_MODEL, PACK_W), lambda j: (0, j)),
        out_shape=jax.ShapeDtypeStruct((D_MODEL, PACKED_DIM), BF16),
        compiler_params=pltpu.CompilerParams(dimension_semantics=("arbitrary",)),
        name="pack_w_in",
    )(w_t, w_t)


def _pad_rows_front(a, rows):
    pad = [(0, 0)] * a.ndim
    pad[-2] = (rows - a.shape[-2], 0)
    return jnp.pad(a, pad)


def _pad_lanes(a):
    pad = [(0, 0)] * a.ndim
    pad[-1] = (0, LANES - a.shape[-1])
    return jnp.pad(a, pad)


def kernel(x_prompt, x_sample, state_conv_a, state_gdn, state_conv_b, p_prompt, p_sample, ln_in_g, ln_in_b, w_in, w_conv_a, a_log, dt_bias, norm_a_g, w_conv_b, w_proj_a, w_proj_b, w_out, ln1_g, ln1_b, w_ple, w_ple_gate, ln2_g, ln2_b):
    assert w_in.shape[0] == DEPTH == 1
    row = lambda v: v.reshape(1, -1).astype(F32)
    w_in_packed = _pack_w_in(jnp.swapaxes(w_in, 1, 2))
    others = (
        ("ln_in_g", row(ln_in_g)), ("ln_in_b", row(ln_in_b)),
        ("w_in", None),
        ("w_conv_a", w_conv_a[0].astype(F32)),
        ("a_log", _pad_lanes(row(a_log[0]))), ("dt_bias", _pad_lanes(row(dt_bias[0]))),
        ("norm_a_g", row(norm_a_g[0])),
        ("w_conv_b", w_conv_b[0].astype(F32)),
        ("w_proj_a", w_proj_a[0].astype(BF16)), ("w_proj_b", w_proj_b[0].astype(BF16)),
        ("w_out", w_out[0].astype(BF16)),
        ("ln1_g", row(ln1_g[0])), ("ln1_b", row(ln1_b[0])),
        ("w_ple", w_ple[0].astype(BF16)), ("w_ple_gate", w_ple_gate[0].astype(BF16)),
        ("ln2_g", row(ln2_g[0])), ("ln2_b", row(ln2_b[0])),
    )
    weights = (w_in_packed, others)

    bp, seq, _ = x_prompt.shape
    y_p, ca_p, s_p, cb_p = _encode(
        x_prompt, p_prompt, None, weights, nseg=bp, seg=PROMPT_TILE, ch=PROMPT_CHUNK)

    bs, ts, _ = x_sample.shape
    y_s, ca_s, s_s, cb_s = _encode(
        x_sample, p_sample,
        (_pad_rows_front(state_conv_a[0], SUBLANES), state_gdn[0].astype(F32),
         _pad_rows_front(state_conv_b[0], SUBLANES)),
        weights, nseg=SAMPLE_TILE_SEQS, seg=ts, ch=ts)

    na, nb = CONV_A - 1, CONV_B - 1
    return (y_p, y_s,
            ca_p[None, :, SUBLANES - na:], s_p[None], cb_p[None, :, SUBLANES - nb:],
            ca_s[None, :, SUBLANES - na:], s_s[None].astype(state_gdn.dtype),
            cb_s[None, :, SUBLANES - nb:])
```

```python
import functools

import jax
import jax.numpy as jnp
from jax import lax
from jax.experimental import pallas as pl
from jax.experimental.pallas import tpu as pltpu

D_MODEL = 1024
N_HEADS = 8
HEAD_D = 128
KEY_DIM = N_HEADS * HEAD_D
QKV_DIM = 3 * KEY_DIM
WIDTH_B = D_MODEL
P_DIM = 256
CONV_A = 4
CONV_B = 3
PROMPT_CHUNK = 64
DEPTH = 1
ALPHA = (2 * DEPTH) ** 0.25
LN_EPS = 1e-5
RMS_EPS = 1e-6
L2_EPS = 1e-6

OFF_ZA = QKV_DIM
OFF_BETA = OFF_ZA + KEY_DIM
OFF_DECAY = OFF_BETA + N_HEADS
OFF_BB = OFF_DECAY + N_HEADS
OFF_CB = OFF_BB + WIDTH_B
OFF_UB = OFF_CB + WIDTH_B
OFF_ZB = OFF_UB + WIDTH_B
OFF_GATE = OFF_ZB + WIDTH_B
IN_DIM = OFF_GATE + 2 * D_MODEL

SUBLANES = 8
LANES = 128
INV_BLOCK = 16
PROMPT_TILE = 256
SAMPLE_TILE_SEQS = 8
P1_CHUNKS = 4
QKV_CHUNK = 256
P2_SEGS = 2
VMEM_LIMIT_BYTES = 127 * 512 * 1024

PACK_W = 512
PACKED_DIM = IN_DIM - (OFF_BB - OFF_BETA) + 2 * LANES
PACKED_COLS = ((QKV_DIM, 0), (KEY_DIM, OFF_ZA), (LANES, OFF_BETA + 4 * WIDTH_B + 2 * D_MODEL),
               (LANES, OFF_BETA + 4 * WIDTH_B + 2 * D_MODEL + LANES), (4 * WIDTH_B, OFF_BETA),
               (2 * D_MODEL, OFF_BETA + 4 * WIDTH_B))

F32 = jnp.float32
NEG_LOG2_E = -1.4426950408889634
BF16 = jnp.bfloat16


def _sigmoid(x):
    return 1.0 / (1.0 + jnp.exp2(x * NEG_LOG2_E))


def _silu(x):
    return x * _sigmoid(x)


def _softplus(x):
    return jnp.maximum(x, 0.0) + jnp.log(1.0 + jnp.exp(-jnp.abs(x)))


def _layer_norm(x, g, b):
    mu = jnp.mean(x, axis=-1, keepdims=True)
    xc = x - mu
    var = jnp.mean(xc * xc, axis=-1, keepdims=True)
    return xc * lax.rsqrt(var + LN_EPS) * g + b


def _mm(a, b):
    return jnp.dot(a.astype(BF16), b.astype(BF16), preferred_element_type=F32)


def _causal_conv(seg, hist8, w_ref, ntaps, buf):
    n = seg.shape[0]
    buf[0:SUBLANES, :] = hist8
    buf[SUBLANES:SUBLANES + n, :] = seg
    acc = seg * w_ref[ntaps - 1:ntaps, :]
    for s in range(1, ntaps):
        acc = acc + buf[SUBLANES - s:SUBLANES - s + n, :] * w_ref[ntaps - 1 - s:ntaps - s, :]
    return acc


def _block_diag2(y1, y2):
    z = jnp.zeros_like(y1)
    return jnp.concatenate([jnp.concatenate([y1, z], axis=1), jnp.concatenate([z, y2], axis=1)],
                           axis=0)


def _inv_unit_lower(a_list, diag_mask, nblk, mm_each):
    d = [jnp.where(diag_mask, a, 0.0) for a in a_list]
    low = [a - x for a, x in zip(a_list, d)]
    d2 = mm_each(d, d)
    d4 = mm_each(d2, d2)
    dd2 = mm_each(d, d2)
    d8 = mm_each(d4, d4)
    x1 = [b - a - c for a, b, c in zip(d, d2, dd2)]
    d4d8 = mm_each(d4, d8)
    x2 = [a + b + c for a, b, c in zip(d4, d8, d4d8)]
    x1x2 = mm_each(x1, x2)
    xd = [a + b + c for a, b, c in zip(x1, x2, x1x2)]
    xdl = mm_each(xd, low)
    n = [a + b for a, b in zip(low, xdl)]
    if nblk == 2:
        xq = [-a for a in n]
    elif nblk == 4:
        n2 = mm_each(n, n)
        nn2 = mm_each(n, n2)
        xq = [b - a - c for a, b, c in zip(n, n2, nn2)]
    else:
        raise NotImplementedError(nblk)
    xqxd = mm_each(xq, xd)
    return [a + b + c for a, b, c in zip(xq, xd, xqxd)]


def _layer_kernel(cfg, x_ref, p_ref, *refs):
    init_refs, refs = (None, refs) if cfg[4] else (refs[:3], refs[3:])
    _layer_body(cfg, x_ref, p_ref, init_refs, *refs)


def _layer_body(cfg,
                  x_ref, p_ref, init_refs,
                  ln_in_g, ln_in_b, w_qkv, w_za, w_beta, w_dec, w_b, w_gate,
                  w_conv_a, a_log, dt_bias, norm_a_g, w_conv_b,
                  w_proj_a, w_proj_b, w_out, ln1_g, ln1_b, w_ple, w_ple_gate, ln2_g, ln2_b,
                  y_ref, ca_ref, s_ref, cb_ref,
                  hbf_s, q_s, k_s, v_s, beta_s, g_s, egl_s, w_s, qd_s, qk_s, cbuf_s, pbg_s, ybf_s):
    u_s, kd_s, o_s = v_s, k_s, q_s
    nseg, seg, ch, carry, zero_init = cfg
    tt = nseg * seg
    nch = tt // ch
    nch_seg = seg // ch
    nblk = ch // INV_BLOCK
    p1_chunks = min(P1_CHUNKS, nch)
    t_idx = pl.program_id(1)

    def _seed():
        if zero_init:
            ca_ref[...] = jnp.zeros(ca_ref.shape, F32)
            cb_ref[...] = jnp.zeros(cb_ref.shape, F32)
            s_ref[...] = jnp.zeros(s_ref.shape, F32)
        else:
            hista_ref, sin_ref, histb_ref = init_refs
            ca_ref[...] = hista_ref[...]
            cb_ref[...] = histb_ref[...]
            s_ref[...] = sin_ref[...]

    if carry:
        pl.when(t_idx == 0)(_seed)
    else:
        _seed()

    def conv_tile(pre, hist_ref, col, w_ref, ntaps):
        outs = []
        for i in range(nseg):
            part = pre[i * seg:(i + 1) * seg]
            outs.append(_causal_conv(part, hist_ref[i, :, col], w_ref.at[:, col], ntaps,
                                     cbuf_s))
            hist_ref[i, :, col] = part[seg - SUBLANES:]
        return outs[0] if nseg == 1 else jnp.concatenate(outs, axis=0)

    h = _layer_norm(x_ref[...].reshape(tt, D_MODEL), ln_in_g[...], ln_in_b[...])
    y_ref[...] = h.reshape(nseg, seg, D_MODEL)
    hbf_s[...] = h.astype(BF16)

    for j in range(QKV_DIM // QKV_CHUNK):
        col = slice(j * QKV_CHUNK, (j + 1) * QKV_CHUNK)
        grp, sub = divmod(j, KEY_DIM // QKV_CHUNK)
        dst = (q_s, k_s, v_s)[grp]
        pre = jnp.dot(hbf_s[...], w_qkv[:, col], preferred_element_type=F32)
        act = _silu(conv_tile(pre, ca_ref, col, w_conv_a, CONV_A))
        for hh in range(QKV_CHUNK // HEAD_D):
            hc = slice(sub * QKV_CHUNK + hh * HEAD_D, sub * QKV_CHUNK + (hh + 1) * HEAD_D)
            xh = act[:, hh * HEAD_D:(hh + 1) * HEAD_D]
            if grp == 2:
                dst[:, hc] = xh
            else:
                scale = HEAD_D ** -0.5 if grp == 0 else 1.0
                ss = jnp.sum(xh * xh, axis=-1, keepdims=True)
                dst[:, hc] = xh * (lax.rsqrt(ss + L2_EPS) * scale)

    beta_s[...] = _sigmoid(jnp.dot(hbf_s[...], w_beta[...], preferred_element_type=F32))
    zdec = jnp.dot(hbf_s[...], w_dec[...], preferred_element_type=F32)
    g_s[...] = -jnp.exp(a_log[...]) * _softplus(zdec + dt_bias[...])

    fill_queue = []
    held = {}

    def fill():
        if fill_queue:
            fill_queue.pop(0)()

    def b_proj(k, cj):
        return jnp.dot(hbf_s[...], w_b[:, k * WIDTH_B + cj.start:k * WIDTH_B + cj.stop],
                       preferred_element_type=F32)

    def b_tile_steps(cj):
        def step_c():
            held["cb"] = b_proj(1, cj)

        def step_conv():
            cu = held.pop("cb") * b_proj(2, cj)
            pbg_s[:, cj] = conv_tile(cu, cb_ref, cj, w_conv_b, CONV_B)

        def step_b():
            pbg_s[:, cj] = b_proj(0, cj) * pbg_s[:, cj]

        def step_gate():
            ybf_s[:, cj] = (pbg_s[:, cj] * _silu(b_proj(3, cj))).astype(BF16)

        return [step_c, step_conv, step_b, step_gate]

    def b_out_steps(cj):
        def step_proj():
            pbg_s[:, cj] = jnp.dot(ybf_s[...], w_proj_b[:, cj], preferred_element_type=F32)

        def step_merge_gate():
            gate_b = _sigmoid(jnp.dot(hbf_s[...], w_gate[:, D_MODEL + cj.start:D_MODEL + cj.stop],
                                      preferred_element_type=F32))
            pbg_s[:, cj] = gate_b * pbg_s[:, cj]

        return [step_proj, step_merge_gate]

    b_tiles = [slice(j * QKV_CHUNK, (j + 1) * QKV_CHUNK) for j in range(WIDTH_B // QKV_CHUNK)]
    for cj in b_tiles:
        fill_queue += b_tile_steps(cj)
    for cj in b_tiles:
        fill_queue += b_out_steps(cj)

    pw = 2 * ch
    ri = lax.broadcasted_iota(jnp.int32, (ch, pw), 0)
    cn = lax.broadcasted_iota(jnp.int32, (ch, pw), 1)
    cj = cn & (ch - 1)
    hi = cn >= ch
    hi_row = lax.broadcasted_iota(jnp.int32, (1, pw), 1) >= ch
    causal = ri >= cj
    strict = ri > cj
    diag_mask = (ri // INV_BLOCK) == (cj // INV_BLOCK)
    rt = lax.broadcasted_iota(jnp.int32, (ch, ch), 0)
    ct = lax.broadcasted_iota(jnp.int32, (ch, ch), 1)
    ltri = (rt >= ct).astype(F32)

    pairs = range(N_HEADS // 2)
    pcols = [slice(p * 2 * HEAD_D, (p + 1) * 2 * HEAD_D) for p in pairs]
    hcols = [slice(hd * HEAD_D, (hd + 1) * HEAD_D) for hd in range(N_HEADS)]

    def bd_pair(y):
        zero = jnp.zeros_like(y)
        return jnp.concatenate([jnp.where(hi, zero, y), jnp.where(hi, y, zero)], axis=0)

    def bd_wide(y):
        return _block_diag2(y[:, :HEAD_D], y[:, HEAD_D:])

    def mm_pairs(xs, ys):
        out = [jnp.dot(x.astype(BF16), bd_pair(y.astype(BF16)), preferred_element_type=F32)
               for x, y in zip(xs, ys)]
        fill()
        return out

    def lanes_of_pair(cols, p):
        return jnp.concatenate([jnp.broadcast_to(cols[:, 2 * p:2 * p + 1], (ch, HEAD_D)),
                                jnp.broadcast_to(cols[:, 2 * p + 1:2 * p + 2], (ch, HEAD_D))],
                               axis=1)

    def block_rows(blk):
        start = blk * ch
        return pl.ds(start if isinstance(start, int) else pl.multiple_of(start, ch), ch)

    per_seq = max(p1_chunks // nseg, 1) if nch_seg > 1 else 0

    def phase1(it, carry_):
        units = []
        pre = {}
        for j in range(p1_chunks):
            if nch_seg == 1:
                c = it * p1_chunks + j
            else:
                c = (j // per_seq) * nch_seg + it * per_seq + j % per_seq
            rows = block_rows(c)
            g_c = g_s[rows, :]
            beta_c = beta_s[rows, :]
            for p in pairs:
                units.append((j, rows, p))
                pre[j, p] = (k_s[rows, pcols[p]], q_s[rows, pcols[p]], v_s[rows, pcols[p]])
            gc = jnp.dot(ltri, g_c, preferred_element_type=F32,
                         precision=lax.Precision.HIGHEST)
            gc_rows = jnp.concatenate([gc, gc], axis=0).T
            glast = gc[ch - 1:ch, :]
            pre[j] = (c, gc, gc_rows, beta_c, jnp.exp(gc), jnp.exp(glast - gc), jnp.exp(glast))

        dmat, lhs, rk, rhs = [], [], [], []
        for j, rows, p in units:
            kp, qp, vp = pre[j, p]
            _, gc, gc_rows, beta_c, egc, _, _ = pre[j]
            g_col = jnp.where(hi, gc[:, 2 * p + 1:2 * p + 2], gc[:, 2 * p:2 * p + 1])
            g_row = jnp.where(hi_row, gc_rows[2 * p + 1:2 * p + 2, :], gc_rows[2 * p:2 * p + 1, :])
            dmat.append(jnp.where(causal, jnp.exp(jnp.where(causal, g_col - g_row, 0.0)), 0.0))
            kb = kp * lanes_of_pair(beta_c, p)
            lhs.append(jnp.concatenate([kb, qp], axis=0).astype(BF16))
            rk.append(bd_wide(kp.astype(BF16)))
            rhs.append((vp * lanes_of_pair(beta_c, p), kb * lanes_of_pair(egc, p)))
        kkqk = [lax.dot_general(a, b, (((1,), (1,)), ((), ())), preferred_element_type=F32)
                for a, b in zip(lhs, rk)]
        fill()
        a_mat = [jnp.where(strict, x[:ch] * d, 0.0) for x, d in zip(kkqk, dmat)]
        t_x = _inv_unit_lower(a_mat, diag_mask, nblk, mm_pairs)
        duw = [jnp.dot(t.astype(BF16),
                       jnp.concatenate([bd_wide(ru.astype(BF16)), bd_wide(rw.astype(BF16))], axis=1),
                       preferred_element_type=F32)
               for t, (ru, rw) in zip(t_x, rhs)]

        fill()
        for i, (j, rows, p) in enumerate(units):
            kp, qp = k_s[rows, pcols[p]], q_s[rows, pcols[p]]
            _, _, _, _, egc, ekd, _ = pre[j]
            qk_s[p, rows, :] = (kkqk[i][ch:] * dmat[i]).astype(BF16)
            u_s[rows, pcols[p]] = rhs[i][0] + duw[i][:, :2 * HEAD_D]
            w_s[rows, pcols[p]] = (rhs[i][1] + duw[i][:, 2 * HEAD_D:]).astype(BF16)
            qd_s[rows, pcols[p]] = (qp * lanes_of_pair(egc, p)).astype(BF16)
            kd_s[rows, pcols[p]] = kp * lanes_of_pair(ekd, p)
        for j in range(p1_chunks):
            egl_s[pl.ds(pre[j][0], 1), :] = pre[j][6]
        return carry_


    ngroups = nseg // P2_SEGS
    assert ngroups == 1 or nch_seg == 1

    def phase2(it, carry_):
        c, sg = (it, 0) if ngroups == 1 else (0, it)
        units = []
        egl = {}
        for i in range(P2_SEGS):
            sq = sg * P2_SEGS + i
            blk = sq * nch_seg + c
            rows = block_rows(blk)
            egl[i] = egl_s[pl.ds(blk, 1), :]
            units += [(i, sq, rows, p) for p in pairs]
        s_old = {(i, hd): s_ref[sq, hd] for i, sq, _, p in units for hd in (2 * p, 2 * p + 1)}
        wq = [jnp.concatenate([w_s[rows, pcols[p]], qd_s[rows, pcols[p]]], axis=0)
              for _, _, rows, p in units]
        u = [u_s[rows, pcols[p]] for _, _, rows, p in units]
        qk = [qk_s[p, rows, :] for _, _, rows, p in units]
        kd_t = {(i, hd): kd_s[rows, hcols[hd]].T.astype(BF16)
                for i, _, rows, p in units for hd in (2 * p, 2 * p + 1)}
        s_bd = [_block_diag2(s_old[i, 2 * p].astype(BF16), s_old[i, 2 * p + 1].astype(BF16))
                for i, _, _, p in units]
        ws = [jnp.dot(a, sb, preferred_element_type=F32) for a, sb in zip(wq, s_bd)]
        v_new = [(a - b[:ch]).astype(BF16) for a, b in zip(u, ws)]
        o = [b[ch:] + jnp.dot(a, bd_wide(v), preferred_element_type=F32)
             for a, b, v in zip(qk, ws, v_new)]
        s_new = {}
        for n, (i, _, _, p) in enumerate(units):
            for half, hd in enumerate((2 * p, 2 * p + 1)):
                s_new[i, hd] = s_old[i, hd] * egl[i][:, hd:hd + 1] + jnp.dot(
                    kd_t[i, hd], v_new[n][:, half * HEAD_D:(half + 1) * HEAD_D],
                    preferred_element_type=F32)
        for n, (i, sq, rows, p) in enumerate(units):
            o_s[rows, pcols[p]] = o[n]
            s_ref[sq, 2 * p] = s_new[i, 2 * p]
            s_ref[sq, 2 * p + 1] = s_new[i, 2 * p + 1]
        return carry_

    n_p1, n_p2 = nch // p1_chunks, nch_seg * ngroups
    phase1(0, 0)
    done2 = 0
    for it1 in range(1, n_p1 + 1):
        if it1 < n_p1:
            phase1(it1, 0)
        ready2 = n_p2 * it1 // n_p1
        for it2 in range(done2, ready2):
            phase2(it2, 0)
        done2 = ready2
    while fill_queue:
        fill()

    halves = []
    for k in range(2):
        sq = slice(k * nseg // 2, (k + 1) * nseg // 2)
        halves.append({"rs": slice(k * tt // 2, (k + 1) * tt // 2), "sq": sq})

    def stage_gate_proj(hf):
        lhs = hbf_s[hf["rs"], :]
        hf["za"] = jnp.dot(lhs, w_za[...], preferred_element_type=F32)
        hf["ga"] = jnp.dot(lhs, w_gate[:, :D_MODEL], preferred_element_type=F32)

    def stage_gated_norm(hf):
        ya = []
        for hd in range(N_HEADS):
            hc = slice(hd * HEAD_D, (hd + 1) * HEAD_D)
            oh = o_s[hf["rs"], hc]
            ms = jnp.mean(oh * oh, axis=-1, keepdims=True)
            ya.append((oh * lax.rsqrt(ms + RMS_EPS) * norm_a_g[...]
                       * _silu(hf["za"][:, hc])).astype(BF16))
        hf["ya"] = jnp.concatenate(ya, axis=1)

    def stage_merge(hf):
        pa = jnp.dot(hf.pop("ya"), w_proj_a[...], preferred_element_type=F32)
        hf["merged"] = (_sigmoid(hf.pop("ga")) * pa + pbg_s[hf["rs"], :]).astype(BF16)

    def stage_out_proj(hf):
        hf["mo"] = jnp.dot(hf.pop("merged"), w_out[...], preferred_element_type=F32)

    def stage_norm1(hf):
        h_in = y_ref[hf["sq"]].reshape(tt // 2, D_MODEL)
        hf["h1"] = _layer_norm(ALPHA * h_in + hf.pop("mo"), ln1_g[...], ln1_b[...])

    def stage_ple_proj(hf):
        hf["pg"] = jnp.dot(hf["h1"].astype(BF16), w_ple_gate[...], preferred_element_type=F32)
        hf["pp"] = jnp.dot(p_ref[0, hf["sq"]].reshape(tt // 2, P_DIM).astype(BF16), w_ple[...],
                           preferred_element_type=F32)

    def stage_norm2(hf):
        ple = _sigmoid(hf.pop("pg")) * hf.pop("pp")
        y_ref[hf["sq"]] = _layer_norm(ALPHA * hf.pop("h1") + ple, ln2_g[...], ln2_b[...]).reshape(
            nseg // 2, seg, D_MODEL)

    for stage in (stage_gate_proj, stage_gated_norm, stage_merge, stage_out_proj, stage_norm1,
                  stage_ple_proj, stage_norm2):
        for hf in halves:
            stage(hf)


def _encode(x, p, init_state, weights, *, nseg, seg, ch):
    nseq, t_len, _ = x.shape
    zero_init = init_state is None
    state_shapes = ((nseq, SUBLANES, QKV_DIM), (nseq, N_HEADS, HEAD_D, HEAD_D),
                    (nseq, SUBLANES, WIDTH_B))
    nb, nt = nseq // nseg, t_len // seg
    carry = nt > 1
    assert nseq % nseg == 0 and t_len % seg == 0 and seg % ch == 0 and seg % SUBLANES == 0
    assert ch & (ch - 1) == 0 and ch % INV_BLOCK == 0 and nseg % P2_SEGS == 0
    tt = nseg * seg
    nch = tt // ch
    cfg = (nseg, seg, ch, carry, zero_init)

    def tile_spec(width):
        return pl.BlockSpec((nseg, seg, width), lambda b, t: (b, t, 0))

    def state_spec(shape):
        nd = len(shape)
        return pl.BlockSpec((nseg,) + tuple(shape[1:]), lambda b, t: (b,) + (0,) * (nd - 1),
                            pipeline_mode=pl.Buffered(1))

    def const_spec(arr):
        nd = arr.ndim
        return pl.BlockSpec(arr.shape, lambda b, t: (0,) * nd, pipeline_mode=pl.Buffered(1))

    def cols_spec(arr, width, offset):
        assert offset % width == 0
        return pl.BlockSpec((arr.shape[0], width), lambda b, t: (0, offset // width),
                            pipeline_mode=pl.Buffered(1))

    w_in_packed, others = weights
    p_spec = pl.BlockSpec((1, nseg, seg, P_DIM), lambda b, t: (0, b, t, 0))
    in_specs = [tile_spec(D_MODEL), p_spec]
    operands = [x, p]
    if not zero_init:
        assert tuple(a.shape for a in init_state) == state_shapes
        in_specs += [state_spec(shape) for shape in state_shapes]
        operands += list(init_state)
    for name, arr in others:
        if name == "w_in":
            for width, offset in PACKED_COLS:
                in_specs.append(cols_spec(w_in_packed, width, offset))
                operands.append(w_in_packed)
        else:
            in_specs.append(const_spec(arr))
            operands.append(arr)
    out_shape = (jax.ShapeDtypeStruct(x.shape, F32),) + tuple(
        jax.ShapeDtypeStruct(shape, F32) for shape in state_shapes)
    out_specs = (tile_spec(D_MODEL),) + tuple(state_spec(shape) for shape in state_shapes)
    scratch = [
        pltpu.VMEM((tt, D_MODEL), BF16),
        pltpu.VMEM((tt, KEY_DIM), F32),
        pltpu.VMEM((tt, KEY_DIM), F32),
        pltpu.VMEM((tt, KEY_DIM), F32),
        pltpu.VMEM((tt, LANES), F32),
        pltpu.VMEM((tt, LANES), F32),
        pltpu.VMEM((max(nch, SUBLANES), LANES), F32),
        pltpu.VMEM((tt, KEY_DIM), BF16),
        pltpu.VMEM((tt, KEY_DIM), BF16),
        pltpu.VMEM((N_HEADS // 2, tt, 2 * ch), BF16),
        pltpu.VMEM((SUBLANES + seg, QKV_CHUNK), F32),
        pltpu.VMEM((tt, D_MODEL), F32),
        pltpu.VMEM((tt, WIDTH_B), BF16),
    ]
    return pl.pallas_call(
        functools.partial(_layer_kernel, cfg),
        grid=(nb, nt),
        in_specs=in_specs,
        out_specs=out_specs,
        out_shape=out_shape,
        scratch_shapes=scratch,
        compiler_params=pltpu.CompilerParams(
            dimension_semantics=("arbitrary", "arbitrary"),
            vmem_limit_bytes=VMEM_LIMIT_BYTES),
        name=f"gdn_shortconv_layer_n{nseg}_t{seg}_c{ch}",
    )(*operands)


def _pack_kernel(a_ref, b_ref, o_ref):
    j = pl.program_id(0)
    n_plain = OFF_BETA // PACK_W
    n_shift = (IN_DIM - OFF_BB) // PACK_W
    shift = OFF_BB - OFF_BETA
    row = lax.broadcasted_iota(jnp.int32, (LANES, 1), 0)

    @pl.when(j < n_plain)
    def _():
        o_ref[...] = a_ref[...].T.astype(BF16)

    @pl.when((j >= n_plain) & (j < n_plain + n_shift))
    def _():
        o_ref[...] = jnp.concatenate([a_ref[shift:, :], b_ref[...]], axis=0).T.astype(BF16)

    @pl.when(j == n_plain + n_shift)
    def _():
        beta = jnp.where(row < N_HEADS, a_ref[0:LANES, :], 0.0)
        decay = jnp.where(row < N_HEADS, a_ref[N_HEADS:N_HEADS + LANES, :], 0.0)
        o_ref[:, :LANES] = beta.T.astype(BF16)
        o_ref[:, LANES:2 * LANES] = decay.T.astype(BF16)
        o_ref[:, 2 * LANES:] = jnp.zeros((o_ref.shape[0], PACK_W - 2 * LANES), BF16)


def _pack_w_in(w_t):
    n_plain = OFF_BETA // PACK_W
    n_shift = (IN_DIM - OFF_BB) // PACK_W
    shift = OFF_BB - OFF_BETA
    assert OFF_BETA % PACK_W == 0 and (IN_DIM - OFF_BB) % PACK_W == 0
    assert OFF_DECAY - OFF_BETA == N_HEADS == SUBLANES and shift % SUBLANES == 0

    def a_idx(j):
        return (0, jnp.where(j < n_plain + n_shift, j, n_plain), 0)

    def b_idx(j):
        return (0, jnp.where((j >= n_plain) & (j < n_plain + n_shift), (j + 1) * (PACK_W // shift), 0), 0)

    return pl.pallas_call(
        _pack_kernel,
        grid=(n_plain + n_shift + 1,),
        in_specs=[pl.BlockSpec((None, PACK_W, D_MODEL), a_idx),
                  pl.BlockSpec((None, shift, D_MODEL), b_idx)],
        out_specs=pl.BlockSpec((D_MODEL, PACK_W), lambda j: (0, j)),
        out_shape=jax.ShapeDtypeStruct((D_MODEL, PACKED_DIM), BF16),
        compiler_params=pltpu.CompilerParams(dimension_semantics=("arbitrary",)),
        name="pack_w_in",
    )(w_t, w_t)


def _pad_rows_front(a, rows):
    pad = [(0, 0)] * a.ndim
    pad[-2] = (rows - a.shape[-2], 0)
    return jnp.pad(a, pad)


def _pad_lanes(a):
    pad = [(0, 0)] * a.ndim
    pad[-1] = (0, LANES - a.shape[-1])
    return jnp.pad(a, pad)


def kernel(x_prompt, x_sample, state_conv_a, state_gdn, state_conv_b, p_prompt, p_sample, ln_in_g, ln_in_b, w_in, w_conv_a, a_log, dt_bias, norm_a_g, w_conv_b, w_proj_a, w_proj_b, w_out, ln1_g, ln1_b, w_ple, w_ple_gate, ln2_g, ln2_b):
    assert w_in.shape[0] == DEPTH == 1
    row = lambda v: v.reshape(1, -1).astype(F32)
    w_in_packed = _pack_w_in(jnp.swapaxes(w_in, 1, 2))
    others = (
        ("ln_in_g", row(ln_in_g)), ("ln_in_b", row(ln_in_b)),
        ("w_in", None),
        ("w_conv_a", w_conv_a[0].astype(F32)),
        ("a_log", _pad_lanes(row(a_log[0]))), ("dt_bias", _pad_lanes(row(dt_bias[0]))),
        ("norm_a_g", row(norm_a_g[0])),
        ("w_conv_b", w_conv_b[0].astype(F32)),
        ("w_proj_a", w_proj_a[0].astype(BF16)), ("w_proj_b", w_proj_b[0].astype(BF16)),
        ("w_out", w_out[0].astype(BF16)),
        ("ln1_g", row(ln1_g[0])), ("ln1_b", row(ln1_b[0])),
        ("w_ple", w_ple[0].astype(BF16)), ("w_ple_gate", w_ple_gate[0].astype(BF16)),
        ("ln2_g", row(ln2_g[0])), ("ln2_b", row(ln2_b[0])),
    )
    weights = (w_in_packed, others)

    bp, seq, _ = x_prompt.shape
    y_p, ca_p, s_p, cb_p = _encode(
        x_prompt, p_prompt, None, weights, nseg=bp, seg=PROMPT_TILE, ch=PROMPT_CHUNK)

    bs, ts, _ = x_sample.shape
    y_s, ca_s, s_s, cb_s = _encode(
        x_sample, p_sample,
        (_pad_rows_front(state_conv_a[0], SUBLANES), state_gdn[0].astype(F32),
         _pad_rows_front(state_conv_b[0], SUBLANES)),
        weights, nseg=SAMPLE_TILE_SEQS, seg=ts, ch=ts)

    na, nb = CONV_A - 1, CONV_B - 1
    return (y_p, y_s,
            ca_p[None, :, SUBLANES - na:], s_p[None], cb_p[None, :, SUBLANES - nb:],
            ca_s[None, :, SUBLANES - na:], s_s[None].astype(state_gdn.dtype),
            cb_s[None, :, SUBLANES - nb:])
```

```python
import functools

import jax
import jax.numpy as jnp
from jax import lax
from jax.experimental import pallas as pl
from jax.experimental.pallas import tpu as pltpu

D_MODEL = 1024
N_HEADS = 8
HEAD_D = 128
KEY_DIM = N_HEADS * HEAD_D
QKV_DIM = 3 * KEY_DIM
WIDTH_B = D_MODEL
P_DIM = 256
CONV_A = 4
CONV_B = 3
PROMPT_CHUNK = 64
DEPTH = 1
ALPHA = (2 * DEPTH) ** 0.25
LN_EPS = 1e-5
RMS_EPS = 1e-6
L2_EPS = 1e-6

OFF_ZA = QKV_DIM
OFF_BETA = OFF_ZA + KEY_DIM
OFF_DECAY = OFF_BETA + N_HEADS
OFF_BB = OFF_DECAY + N_HEADS
OFF_CB = OFF_BB + WIDTH_B
OFF_UB = OFF_CB + WIDTH_B
OFF_ZB = OFF_UB + WIDTH_B
OFF_GATE = OFF_ZB + WIDTH_B
IN_DIM = OFF_GATE + 2 * D_MODEL

SUBLANES = 8
LANES = 128
INV_BLOCK = 16
PROMPT_TILE = 256
SAMPLE_TILE_SEQS = 8
P1_CHUNKS = 4
QKV_CHUNK = 256
P2_SEGS = 2
VMEM_LIMIT_BYTES = 127 * 512 * 1024

PACK_W = 512
PACKED_DIM = IN_DIM - (OFF_BB - OFF_BETA) + 2 * LANES
PACKED_COLS = ((QKV_DIM, 0), (KEY_DIM, OFF_ZA), (LANES, OFF_BETA + 4 * WIDTH_B + 2 * D_MODEL),
               (LANES, OFF_BETA + 4 * WIDTH_B + 2 * D_MODEL + LANES), (4 * WIDTH_B, OFF_BETA),
               (2 * D_MODEL, OFF_BETA + 4 * WIDTH_B))

F32 = jnp.float32
NEG_LOG2_E = -1.4426950408889634
BF16 = jnp.bfloat16


def _sigmoid(x):
    return 1.0 / (1.0 + jnp.exp2(x * NEG_LOG2_E))


def _silu(x):
    return x * _sigmoid(x)


def _softplus(x):
    return jnp.maximum(x, 0.0) + jnp.log(1.0 + jnp.exp(-jnp.abs(x)))


def _layer_norm(x, g, b):
    mu = jnp.mean(x, axis=-1, keepdims=True)
    xc = x - mu
    var = jnp.mean(xc * xc, axis=-1, keepdims=True)
    return xc * lax.rsqrt(var + LN_EPS) * g + b


def _mm(a, b):
    return jnp.dot(a.astype(BF16), b.astype(BF16), preferred_element_type=F32)


def _causal_conv(seg, hist8, w_ref, ntaps, buf):
    n = seg.shape[0]
    buf[0:SUBLANES, :] = hist8
    buf[SUBLANES:SUBLANES + n, :] = seg
    acc = seg * w_ref[ntaps - 1:ntaps, :]
    for s in range(1, ntaps):
        acc = acc + buf[SUBLANES - s:SUBLANES - s + n, :] * w_ref[ntaps - 1 - s:ntaps - s, :]
    return acc


def _block_diag2(y1, y2):
    z = jnp.zeros_like(y1)
    return jnp.concatenate([jnp.concatenate([y1, z], axis=1), jnp.concatenate([z, y2], axis=1)],
                           axis=0)


def _inv_unit_lower(a_list, diag_mask, nblk, mm_each):
    d = [jnp.where(diag_mask, a, 0.0) for a in a_list]
    low = [a - x for a, x in zip(a_list, d)]
    d2 = mm_each(d, d)
    d4 = mm_each(d2, d2)
    dd2 = mm_each(d, d2)
    d8 = mm_each(d4, d4)
    x1 = [b - a - c for a, b, c in zip(d, d2, dd2)]
    d4d8 = mm_each(d4, d8)
    x2 = [a + b + c for a, b, c in zip(d4, d8, d4d8)]
    x1x2 = mm_each(x1, x2)
    xd = [a + b + c for a, b, c in zip(x1, x2, x1x2)]
    xdl = mm_each(xd, low)
    n = [a + b for a, b in zip(low, xdl)]
    if nblk == 2:
        xq = [-a for a in n]
    elif nblk == 4:
        n2 = mm_each(n, n)
        nn2 = mm_each(n, n2)
        xq = [b - a - c for a, b, c in zip(n, n2, nn2)]
    else:
        raise NotImplementedError(nblk)
    xqxd = mm_each(xq, xd)
    return [a + b + c for a, b, c in zip(xq, xd, xqxd)]


def _layer_kernel(cfg, x_ref, p_ref, *refs):
    init_refs, refs = (None, refs) if cfg[4] else (refs[:3], refs[3:])
    _layer_body(cfg, x_ref, p_ref, init_refs, *refs)


def _layer_body(cfg,
                  x_ref, p_ref, init_refs,
                  ln_in_g, ln_in_b, w_qkv, w_za, w_beta, w_dec, w_b, w_gate,
                  w_conv_a, a_log, dt_bias, norm_a_g, w_conv_b,
                  w_proj_a, w_proj_b, w_out, ln1_g, ln1_b, w_ple, w_ple_gate, ln2_g, ln2_b,
                  y_ref, ca_ref, s_ref, cb_ref,
                  hbf_s, q_s, k_s, v_s, beta_s, g_s, egl_s, w_s, qd_s, qk_s, cbuf_s, pbg_s, ybf_s):
    u_s, kd_s, o_s = v_s, k_s, q_s
    nseg, seg, ch, carry, zero_init = cfg
    tt = nseg * seg
    nch = tt // ch
    nch_seg = seg // ch
    nblk = ch // INV_BLOCK
    p1_chunks = min(P1_CHUNKS, nch)
    t_idx = pl.program_id(1)

    def _seed():
        if zero_init:
            ca_ref[...] = jnp.zeros(ca_ref.shape, F32)
            cb_ref[...] = jnp.zeros(cb_ref.shape, F32)
            s_ref[...] = jnp.zeros(s_ref.shape, F32)
        else:
            hista_ref, sin_ref, histb_ref = init_refs
            ca_ref[...] = hista_ref[...]
            cb_ref[...] = histb_ref[...]
            s_ref[...] = sin_ref[...]

    if carry:
        pl.when(t_idx == 0)(_seed)
    else:
        _seed()

    def conv_tile(pre, hist_ref, col, w_ref, ntaps, seqs=None):
        seqs = range(nseg) if seqs is None else seqs
        outs = []
        for n, i in enumerate(seqs):
            part = pre[n * seg:(n + 1) * seg]
            outs.append(_causal_conv(part, hist_ref[i, :, col], w_ref.at[:, col], ntaps,
                                     cbuf_s))
            hist_ref[i, :, col] = part[seg - SUBLANES:]
        return outs[0] if len(outs) == 1 else jnp.concatenate(outs, axis=0)

    halves = []
    for k in range(2):
        sq = slice(k * nseg // 2, (k + 1) * nseg // 2)
        halves.append({"rs": slice(k * tt // 2, (k + 1) * tt // 2), "sq": sq,
                       "seqs": range(sq.start, sq.stop)})

    for hf in halves:
        h = _layer_norm(x_ref[hf["sq"]].reshape(tt // 2, D_MODEL), ln_in_g[...], ln_in_b[...])
        y_ref[hf["sq"]] = h.reshape(nseg // 2, seg, D_MODEL)
        hbf_s[hf["rs"], :] = h.astype(BF16)

    for j in range(QKV_DIM // QKV_CHUNK):
        col = slice(j * QKV_CHUNK, (j + 1) * QKV_CHUNK)
        grp, sub = divmod(j, KEY_DIM // QKV_CHUNK)
        dst = (q_s, k_s, v_s)[grp]
        for hf in halves:
            hf["pre"] = jnp.dot(hbf_s[hf["rs"], :], w_qkv[:, col], preferred_element_type=F32)
        for hf in halves:
            act = _silu(conv_tile(hf.pop("pre"), ca_ref, col, w_conv_a, CONV_A, hf["seqs"]))
            for hh in range(QKV_CHUNK // HEAD_D):
                hc = slice(sub * QKV_CHUNK + hh * HEAD_D, sub * QKV_CHUNK + (hh + 1) * HEAD_D)
                xh = act[:, hh * HEAD_D:(hh + 1) * HEAD_D]
                if grp == 2:
                    dst[hf["rs"], hc] = xh
                else:
                    scale = HEAD_D ** -0.5 if grp == 0 else 1.0
                    ss = jnp.sum(xh * xh, axis=-1, keepdims=True)
                    dst[hf["rs"], hc] = xh * (lax.rsqrt(ss + L2_EPS) * scale)

    beta_s[...] = _sigmoid(jnp.dot(hbf_s[...], w_beta[...], preferred_element_type=F32))
    zdec = jnp.dot(hbf_s[...], w_dec[...], preferred_element_type=F32)
    g_s[...] = -jnp.exp(a_log[...]) * _softplus(zdec + dt_bias[...])

    fill_queue = []
    held = {}

    def fill():
        if fill_queue:
            fill_queue.pop(0)()

    def b_proj(k, cj):
        return jnp.dot(hbf_s[...], w_b[:, k * WIDTH_B + cj.start:k * WIDTH_B + cj.stop],
                       preferred_element_type=F32)

    def b_tile_steps(cj):
        def step_c():
            held["cb"] = b_proj(1, cj)

        def step_conv():
            cu = held.pop("cb") * b_proj(2, cj)
            pbg_s[:, cj] = conv_tile(cu, cb_ref, cj, w_conv_b, CONV_B)

        def step_b():
            pbg_s[:, cj] = b_proj(0, cj) * pbg_s[:, cj]

        def step_gate():
            ybf_s[:, cj] = (pbg_s[:, cj] * _silu(b_proj(3, cj))).astype(BF16)

        return [step_c, step_conv, step_b, step_gate]

    def b_out_steps(cj):
        def step_proj():
            pbg_s[:, cj] = jnp.dot(ybf_s[...], w_proj_b[:, cj], preferred_element_type=F32)

        def step_merge_gate():
            gate_b = _sigmoid(jnp.dot(hbf_s[...], w_gate[:, D_MODEL + cj.start:D_MODEL + cj.stop],
                                      preferred_element_type=F32))
            pbg_s[:, cj] = gate_b * pbg_s[:, cj]

        return [step_proj, step_merge_gate]

    b_tiles = [slice(j * QKV_CHUNK, (j + 1) * QKV_CHUNK) for j in range(WIDTH_B // QKV_CHUNK)]
    for cj in b_tiles:
        fill_queue += b_tile_steps(cj)
    for cj in b_tiles:
        fill_queue += b_out_steps(cj)

    pw = 2 * ch
    ri = lax.broadcasted_iota(jnp.int32, (ch, pw), 0)
    cn = lax.broadcasted_iota(jnp.int32, (ch, pw), 1)
    cj = cn & (ch - 1)
    hi = cn >= ch
    hi_row = lax.broadcasted_iota(jnp.int32, (1, pw), 1) >= ch
    causal = ri >= cj
    strict = ri > cj
    diag_mask = (ri // INV_BLOCK) == (cj // INV_BLOCK)
    rt = lax.broadcasted_iota(jnp.int32, (ch, ch), 0)
    ct = lax.broadcasted_iota(jnp.int32, (ch, ch), 1)
    ltri = (rt >= ct).astype(F32)

    pairs = range(N_HEADS // 2)
    pcols = [slice(p * 2 * HEAD_D, (p + 1) * 2 * HEAD_D) for p in pairs]
    hcols = [slice(hd * HEAD_D, (hd + 1) * HEAD_D) for hd in range(N_HEADS)]

    def bd_pair(y):
        zero = jnp.zeros_like(y)
        return jnp.concatenate([jnp.where(hi, zero, y), jnp.where(hi, y, zero)], axis=0)

    def bd_wide(y):
        return _block_diag2(y[:, :HEAD_D], y[:, HEAD_D:])

    def mm_pairs(xs, ys):
        out = [jnp.dot(x.astype(BF16), bd_pair(y.astype(BF16)), preferred_element_type=F32)
               for x, y in zip(xs, ys)]
        fill()
        return out

    def lanes_of_pair(cols, p):
        return jnp.concatenate([jnp.broadcast_to(cols[:, 2 * p:2 * p + 1], (ch, HEAD_D)),
                                jnp.broadcast_to(cols[:, 2 * p + 1:2 * p + 2], (ch, HEAD_D))],
                               axis=1)

    def block_rows(blk):
        start = blk * ch
        return pl.ds(start if isinstance(start, int) else pl.multiple_of(start, ch), ch)

    per_seq = max(p1_chunks // nseg, 1) if nch_seg > 1 else 0

    def phase1(it, carry_):
        units = []
        pre = {}
        for j in range(p1_chunks):
            if nch_seg == 1:
                c = it * p1_chunks + j
            else:
                c = (j // per_seq) * nch_seg + it * per_seq + j % per_seq
            rows = block_rows(c)
            g_c = g_s[rows, :]
            beta_c = beta_s[rows, :]
            for p in pairs:
                units.append((j, rows, p))
                pre[j, p] = (k_s[rows, pcols[p]], q_s[rows, pcols[p]], v_s[rows, pcols[p]])
            gc = jnp.dot(ltri, g_c, preferred_element_type=F32,
                         precision=lax.Precision.HIGHEST)
            gc_rows = jnp.concatenate([gc, gc], axis=0).T
            glast = gc[ch - 1:ch, :]
            pre[j] = (c, gc, gc_rows, beta_c, jnp.exp(gc), jnp.exp(glast - gc), jnp.exp(glast))

        dmat, lhs, rk, rhs = [], [], [], []
        for j, rows, p in units:
            kp, qp, vp = pre[j, p]
            _, gc, gc_rows, beta_c, egc, _, _ = pre[j]
            g_col = jnp.where(hi, gc[:, 2 * p + 1:2 * p + 2], gc[:, 2 * p:2 * p + 1])
            g_row = jnp.where(hi_row, gc_rows[2 * p + 1:2 * p + 2, :], gc_rows[2 * p:2 * p + 1, :])
            dmat.append(jnp.where(causal, jnp.exp(jnp.where(causal, g_col - g_row, 0.0)), 0.0))
            kb = kp * lanes_of_pair(beta_c, p)
            lhs.append(jnp.concatenate([kb, qp], axis=0).astype(BF16))
            rk.append(bd_wide(kp.astype(BF16)))
            rhs.append((vp * lanes_of_pair(beta_c, p), kb * lanes_of_pair(egc, p)))
        kkqk = [lax.dot_general(a, b, (((1,), (1,)), ((), ())), preferred_element_type=F32)
                for a, b in zip(lhs, rk)]
        fill()
        a_mat = [jnp.where(strict, x[:ch] * d, 0.0) for x, d in zip(kkqk, dmat)]
        t_x = _inv_unit_lower(a_mat, diag_mask, nblk, mm_pairs)
        duw = [jnp.dot(t.astype(BF16),
                       jnp.concatenate([bd_wide(ru.astype(BF16)), bd_wide(rw.astype(BF16))], axis=1),
                       preferred_element_type=F32)
               for t, (ru, rw) in zip(t_x, rhs)]

        fill()
        for i, (j, rows, p) in enumerate(units):
            kp, qp = k_s[rows, pcols[p]], q_s[rows, pcols[p]]
            _, _, _, _, egc, ekd, _ = pre[j]
            qk_s[p, rows, :] = (kkqk[i][ch:] * dmat[i]).astype(BF16)
            u_s[rows, pcols[p]] = rhs[i][0] + duw[i][:, :2 * HEAD_D]
            w_s[rows, pcols[p]] = (rhs[i][1] + duw[i][:, 2 * HEAD_D:]).astype(BF16)
            qd_s[rows, pcols[p]] = (qp * lanes_of_pair(egc, p)).astype(BF16)
            kd_s[rows, pcols[p]] = kp * lanes_of_pair(ekd, p)
        for j in range(p1_chunks):
            egl_s[pl.ds(pre[j][0], 1), :] = pre[j][6]
        return carry_


    ngroups = nseg // P2_SEGS
    assert ngroups == 1 or nch_seg == 1

    def phase2(it, carry_):
        c, sg = (it, 0) if ngroups == 1 else (0, it)
        units = []
        egl = {}
        for i in range(P2_SEGS):
            sq = sg * P2_SEGS + i
            blk = sq * nch_seg + c
            rows = block_rows(blk)
            egl[i] = egl_s[pl.ds(blk, 1), :]
            units += [(i, sq, rows, p) for p in pairs]
        s_old = {(i, hd): s_ref[sq, hd] for i, sq, _, p in units for hd in (2 * p, 2 * p + 1)}
        wq = [jnp.concatenate([w_s[rows, pcols[p]], qd_s[rows, pcols[p]]], axis=0)
              for _, _, rows, p in units]
        u = [u_s[rows, pcols[p]] for _, _, rows, p in units]
        qk = [qk_s[p, rows, :] for _, _, rows, p in units]
        kd_t = {(i, hd): kd_s[rows, hcols[hd]].T.astype(BF16)
                for i, _, rows, p in units for hd in (2 * p, 2 * p + 1)}
        s_bd = [_block_diag2(s_old[i, 2 * p].astype(BF16), s_old[i, 2 * p + 1].astype(BF16))
                for i, _, _, p in units]
        ws = [jnp.dot(a, sb, preferred_element_type=F32) for a, sb in zip(wq, s_bd)]
        v_new = [(a - b[:ch]).astype(BF16) for a, b in zip(u, ws)]
        o = [b[ch:] + jnp.dot(a, bd_wide(v), preferred_element_type=F32)
             for a, b, v in zip(qk, ws, v_new)]
        s_new = {}
        for n, (i, _, _, p) in enumerate(units):
            for half, hd in enumerate((2 * p, 2 * p + 1)):
                s_new[i, hd] = s_old[i, hd] * egl[i][:, hd:hd + 1] + jnp.dot(
                    kd_t[i, hd], v_new[n][:, half * HEAD_D:(half + 1) * HEAD_D],
                    preferred_element_type=F32)
        for n, (i, sq, rows, p) in enumerate(units):
            o_s[rows, pcols[p]] = o[n]
            s_ref[sq, 2 * p] = s_new[i, 2 * p]
            s_ref[sq, 2 * p + 1] = s_new[i, 2 * p + 1]
        return carry_

    n_p1, n_p2 = nch // p1_chunks, nch_seg * ngroups
    phase1(0, 0)
    done2 = 0
    for it1 in range(1, n_p1 + 1):
        if it1 < n_p1:
            phase1(it1, 0)
        ready2 = n_p2 * it1 // n_p1
        for it2 in range(done2, ready2):
            phase2(it2, 0)
        done2 = ready2
    while fill_queue:
        fill()

    def stage_gate_proj(hf):
        lhs = hbf_s[hf["rs"], :]
        hf["za"] = jnp.dot(lhs, w_za[...], preferred_element_type=F32)
        hf["ga"] = jnp.dot(lhs, w_gate[:, :D_MODEL], preferred_element_type=F32)

    def stage_gated_norm(hf):
        ya = []
        for hd in range(N_HEADS):
            hc = slice(hd * HEAD_D, (hd + 1) * HEAD_D)
            oh = o_s[hf["rs"], hc]
            ms = jnp.mean(oh * oh, axis=-1, keepdims=True)
            ya.append((oh * lax.rsqrt(ms + RMS_EPS) * norm_a_g[...]
                       * _silu(hf["za"][:, hc])).astype(BF16))
        hf["ya"] = jnp.concatenate(ya, axis=1)

    def stage_merge(hf):
        pa = jnp.dot(hf.pop("ya"), w_proj_a[...], preferred_element_type=F32)
        hf["merged"] = (_sigmoid(hf.pop("ga")) * pa + pbg_s[hf["rs"], :]).astype(BF16)

    def stage_out_proj(hf):
        hf["mo"] = jnp.dot(hf.pop("merged"), w_out[...], preferred_element_type=F32)

    def stage_norm1(hf):
        h_in = y_ref[hf["sq"]].reshape(tt // 2, D_MODEL)
        hf["h1"] = _layer_norm(ALPHA * h_in + hf.pop("mo"), ln1_g[...], ln1_b[...])

    def stage_ple_proj(hf):
        hf["pg"] = jnp.dot(hf["h1"].astype(BF16), w_ple_gate[...], preferred_element_type=F32)
        hf["pp"] = jnp.dot(p_ref[0, hf["sq"]].reshape(tt // 2, P_DIM).astype(BF16), w_ple[...],
                           preferred_element_type=F32)

    def stage_norm2(hf):
        ple = _sigmoid(hf.pop("pg")) * hf.pop("pp")
        y_ref[hf["sq"]] = _layer_norm(ALPHA * hf.pop("h1") + ple, ln2_g[...], ln2_b[...]).reshape(
            nseg // 2, seg, D_MODEL)

    for stage in (stage_gate_proj, stage_gated_norm, stage_merge, stage_out_proj, stage_norm1,
                  stage_ple_proj, stage_norm2):
        for hf in halves:
            stage(hf)


def _encode(x, p, init_state, weights, *, nseg, seg, ch):
    nseq, t_len, _ = x.shape
    zero_init = init_state is None
    state_shapes = ((nseq, SUBLANES, QKV_DIM), (nseq, N_HEADS, HEAD_D, HEAD_D),
                    (nseq, SUBLANES, WIDTH_B))
    nb, nt = nseq // nseg, t_len // seg
    carry = nt > 1
    assert nseq % nseg == 0 and t_len % seg == 0 and seg % ch == 0 and seg % SUBLANES == 0
    assert ch & (ch - 1) == 0 and ch % INV_BLOCK == 0 and nseg % P2_SEGS == 0
    tt = nseg * seg
    nch = tt // ch
    cfg = (nseg, seg, ch, carry, zero_init)

    def tile_spec(width):
        return pl.BlockSpec((nseg, seg, width), lambda b, t: (b, t, 0))

    def state_spec(shape):
        nd = len(shape)
        return pl.BlockSpec((nseg,) + tuple(shape[1:]), lambda b, t: (b,) + (0,) * (nd - 1),
                            pipeline_mode=pl.Buffered(1))

    def const_spec(arr):
        nd = arr.ndim
        return pl.BlockSpec(arr.shape, lambda b, t: (0,) * nd, pipeline_mode=pl.Buffered(1))

    def cols_spec(arr, width, offset):
        assert offset % width == 0
        return pl.BlockSpec((arr.shape[0], width), lambda b, t: (0, offset // width),
                            pipeline_mode=pl.Buffered(1))

    w_in_packed, others = weights
    p_spec = pl.BlockSpec((1, nseg, seg, P_DIM), lambda b, t: (0, b, t, 0))
    in_specs = [tile_spec(D_MODEL), p_spec]
    operands = [x, p]
    if not zero_init:
        assert tuple(a.shape for a in init_state) == state_shapes
        in_specs += [state_spec(shape) for shape in state_shapes]
        operands += list(init_state)
    for name, arr in others:
        if name == "w_in":
            for width, offset in PACKED_COLS:
                in_specs.append(cols_spec(w_in_packed, width, offset))
                operands.append(w_in_packed)
        else:
            in_specs.append(const_spec(arr))
            operands.append(arr)
    out_shape = (jax.ShapeDtypeStruct(x.shape, F32),) + tuple(
        jax.ShapeDtypeStruct(shape, F32) for shape in state_shapes)
    out_specs = (tile_spec(D_MODEL),) + tuple(state_spec(shape) for shape in state_shapes)
    scratch = [
        pltpu.VMEM((tt, D_MODEL), BF16),
        pltpu.VMEM((tt, KEY_DIM), F32),
        pltpu.VMEM((tt, KEY_DIM), F32),
        pltpu.VMEM((tt, KEY_DIM), F32),
        pltpu.VMEM((tt, LANES), F32),
        pltpu.VMEM((tt, LANES), F32),
        pltpu.VMEM((max(nch, SUBLANES), LANES), F32),
        pltpu.VMEM((tt, KEY_DIM), BF16),
        pltpu.VMEM((tt, KEY_DIM), BF16),
        pltpu.VMEM((N_HEADS // 2, tt, 2 * ch), BF16),
        pltpu.VMEM((SUBLANES + seg, QKV_CHUNK), F32),
        pltpu.VMEM((tt, D_MODEL), F32),
        pltpu.VMEM((tt, WIDTH_B), BF16),
    ]
    return pl.pallas_call(
        functools.partial(_layer_kernel, cfg),
        grid=(nb, nt),
        in_specs=in_specs,
        out_specs=out_specs,
        out_shape=out_shape,
        scratch_shapes=scratch,
        compiler_params=pltpu.CompilerParams(
            dimension_semantics=("arbitrary", "arbitrary"),
            vmem_limit_bytes=VMEM_LIMIT_BYTES),
        name=f"gdn_shortconv_layer_n{nseg}_t{seg}_c{ch}",
    )(*operands)


def _pack_kernel(a_ref, b_ref, o_ref):
    j = pl.program_id(0)
    n_plain = OFF_BETA // PACK_W
    n_shift = (IN_DIM - OFF_BB) // PACK_W
    shift = OFF_BB - OFF_BETA
    row = lax.broadcasted_iota(jnp.int32, (LANES, 1), 0)

    @pl.when(j < n_plain)
    def _():
        o_ref[...] = a_ref[...].T.astype(BF16)

    @pl.when((j >= n_plain) & (j < n_plain + n_shift))
    def _():
        o_ref[...] = jnp.concatenate([a_ref[shift:, :], b_ref[...]], axis=0).T.astype(BF16)

    @pl.when(j == n_plain + n_shift)
    def _():
        beta = jnp.where(row < N_HEADS, a_ref[0:LANES, :], 0.0)
        decay = jnp.where(row < N_HEADS, a_ref[N_HEADS:N_HEADS + LANES, :], 0.0)
        o_ref[:, :LANES] = beta.T.astype(BF16)
        o_ref[:, LANES:2 * LANES] = decay.T.astype(BF16)
        o_ref[:, 2 * LANES:] = jnp.zeros((o_ref.shape[0], PACK_W - 2 * LANES), BF16)


def _pack_w_in(w_t):
    n_plain = OFF_BETA // PACK_W
    n_shift = (IN_DIM - OFF_BB) // PACK_W
    shift = OFF_BB - OFF_BETA
    assert OFF_BETA % PACK_W == 0 and (IN_DIM - OFF_BB) % PACK_W == 0
    assert OFF_DECAY - OFF_BETA == N_HEADS == SUBLANES and shift % SUBLANES == 0

    def a_idx(j):
        return (0, jnp.where(j < n_plain + n_shift, j, n_plain), 0)

    def b_idx(j):
        return (0, jnp.where((j >= n_plain) & (j < n_plain + n_shift), (j + 1) * (PACK_W // shift), 0), 0)

    return pl.pallas_call(
        _pack_kernel,
        grid=(n_plain + n_shift + 1,),
        in_specs=[pl.BlockSpec((None, PACK_W, D_MODEL), a_idx),
                  pl.BlockSpec((None, shift, D_MODEL), b_idx)],
        out_specs=pl.BlockSpec((D_MODEL, PACK_W), lambda j: (0, j)),
        out_shape=jax.ShapeDtypeStruct((D_MODEL, PACKED_DIM), BF16),
        compiler_params=pltpu.CompilerParams(dimension_semantics=("arbitrary",)),
        name="pack_w_in",
    )(w_t, w_t)


def _pad_rows_front(a, rows):
    pad = [(0, 0)] * a.ndim
    pad[-2] = (rows - a.shape[-2], 0)
    return jnp.pad(a, pad)


def _pad_lanes(a):
    pad = [(0, 0)] * a.ndim
    pad[-1] = (0, LANES - a.shape[-1])
    return jnp.pad(a, pad)


def kernel(x_prompt, x_sample, state_conv_a, state_gdn, state_conv_b, p_prompt, p_sample, ln_in_g, ln_in_b, w_in, w_conv_a, a_log, dt_bias, norm_a_g, w_conv_b, w_proj_a, w_proj_b, w_out, ln1_g, ln1_b, w_ple, w_ple_gate, ln2_g, ln2_b):
    assert w_in.shape[0] == DEPTH == 1
    row = lambda v: v.reshape(1, -1).astype(F32)
    w_in_packed = _pack_w_in(jnp.swapaxes(w_in, 1, 2))
    others = (
        ("ln_in_g", row(ln_in_g)), ("ln_in_b", row(ln_in_b)),
        ("w_in", None),
        ("w_conv_a", w_conv_a[0].astype(F32)),
        ("a_log", _pad_lanes(row(a_log[0]))), ("dt_bias", _pad_lanes(row(dt_bias[0]))),
        ("norm_a_g", row(norm_a_g[0])),
        ("w_conv_b", w_conv_b[0].astype(F32)),
        ("w_proj_a", w_proj_a[0].astype(BF16)), ("w_proj_b", w_proj_b[0].astype(BF16)),
        ("w_out", w_out[0].astype(BF16)),
        ("ln1_g", row(ln1_g[0])), ("ln1_b", row(ln1_b[0])),
        ("w_ple", w_ple[0].astype(BF16)), ("w_ple_gate", w_ple_gate[0].astype(BF16)),
        ("ln2_g", row(ln2_g[0])), ("ln2_b", row(ln2_b[0])),
    )
    weights = (w_in_packed, others)

    bp, seq, _ = x_prompt.shape
    y_p, ca_p, s_p, cb_p = _encode(
        x_prompt, p_prompt, None, weights, nseg=bp, seg=PROMPT_TILE, ch=PROMPT_CHUNK)

    bs, ts, _ = x_sample.shape
    y_s, ca_s, s_s, cb_s = _encode(
        x_sample, p_sample,
        (_pad_rows_front(state_conv_a[0], SUBLANES), state_gdn[0].astype(F32),
         _pad_rows_front(state_conv_b[0], SUBLANES)),
        weights, nseg=SAMPLE_TILE_SEQS, seg=ts, ch=ts)

    na, nb = CONV_A - 1, CONV_B - 1
    return (y_p, y_s,
            ca_p[None, :, SUBLANES - na:], s_p[None], cb_p[None, :, SUBLANES - nb:],
            ca_s[None, :, SUBLANES - na:], s_s[None].astype(state_gdn.dtype),
            cb_s[None, :, SUBLANES - nb:])
```

```python
import functools

import jax
import jax.numpy as jnp
from jax import lax
from jax.experimental import pallas as pl
from jax.experimental.pallas import tpu as pltpu

D_MODEL = 1024
N_HEADS = 8
HEAD_D = 128
KEY_DIM = N_HEADS * HEAD_D
QKV_DIM = 3 * KEY_DIM
WIDTH_B = D_MODEL
P_DIM = 256
CONV_A = 4
CONV_B = 3
PROMPT_CHUNK = 64
DEPTH = 1
ALPHA = (2 * DEPTH) ** 0.25
LN_EPS = 1e-5
RMS_EPS = 1e-6
L2_EPS = 1e-6

OFF_ZA = QKV_DIM
OFF_BETA = OFF_ZA + KEY_DIM
OFF_DECAY = OFF_BETA + N_HEADS
OFF_BB = OFF_DECAY + N_HEADS
OFF_CB = OFF_BB + WIDTH_B
OFF_UB = OFF_CB + WIDTH_B
OFF_ZB = OFF_UB + WIDTH_B
OFF_GATE = OFF_ZB + WIDTH_B
IN_DIM = OFF_GATE + 2 * D_MODEL

SUBLANES = 8
LANES = 128
INV_BLOCK = 4
PROMPT_TILE = 256
SAMPLE_TILE_SEQS = 8
P1_CHUNKS = 4
QKV_CHUNK = 256
P2_SEGS = 2
VMEM_LIMIT_BYTES = 127 * 512 * 1024

PACK_W = 512
PACKED_DIM = IN_DIM - (OFF_BB - OFF_BETA) + 2 * LANES
PACKED_COLS = ((QKV_DIM, 0), (KEY_DIM, OFF_ZA), (LANES, OFF_BETA + 4 * WIDTH_B + 2 * D_MODEL),
               (LANES, OFF_BETA + 4 * WIDTH_B + 2 * D_MODEL + LANES), (4 * WIDTH_B, OFF_BETA),
               (2 * D_MODEL, OFF_BETA + 4 * WIDTH_B))

F32 = jnp.float32
NEG_LOG2_E = -1.4426950408889634
BF16 = jnp.bfloat16


def _sigmoid(x):
    return 1.0 / (1.0 + jnp.exp2(x * NEG_LOG2_E))


def _silu(x):
    return x * _sigmoid(x)


def _softplus(x):
    return jnp.maximum(x, 0.0) + jnp.log(1.0 + jnp.exp(-jnp.abs(x)))


def _layer_norm(x, g, b):
    mu = jnp.mean(x, axis=-1, keepdims=True)
    xc = x - mu
    var = jnp.mean(xc * xc, axis=-1, keepdims=True)
    return xc * lax.rsqrt(var + LN_EPS) * g + b


def _mm(a, b):
    return jnp.dot(a.astype(BF16), b.astype(BF16), preferred_element_type=F32)


def _causal_conv(seg, hist8, w_ref, ntaps, buf):
    n = seg.shape[0]
    buf[0:SUBLANES, :] = hist8
    buf[SUBLANES:SUBLANES + n, :] = seg
    acc = seg * w_ref[ntaps - 1:ntaps, :]
    for s in range(1, ntaps):
        acc = acc + buf[SUBLANES - s:SUBLANES - s + n, :] * w_ref[ntaps - 1 - s:ntaps - s, :]
    return acc


def _block_diag2(y1, y2):
    z = jnp.zeros_like(y1)
    return jnp.concatenate([jnp.concatenate([y1, z], axis=1), jnp.concatenate([z, y2], axis=1)],
                           axis=0)


def _inv_unit_lower(a_list, row, col, size, mm_each):
    base = (row // INV_BLOCK) == (col // INV_BLOCK)
    d = [jnp.where(base, a, 0.0) for a in a_list]
    d2 = mm_each(d, d)
    dd2 = mm_each(d, d2)
    x = [b - a - c for a, b, c in zip(d, d2, dd2)]
    s = INV_BLOCK
    while s < size:
        pair = ((row // (2 * s)) == (col // (2 * s))) & ((row // s) != (col // s))
        a21 = [jnp.where(pair, a, 0.0) for a in a_list]
        m1 = [a + b for a, b in zip(a21, mm_each(a21, x))]
        m2 = [a + b for a, b in zip(m1, mm_each(x, m1))]
        x = [a - b for a, b in zip(x, m2)]
        s *= 2
    return x


def _layer_kernel(cfg, x_ref, p_ref, *refs):
    init_refs, refs = (None, refs) if cfg[4] else (refs[:3], refs[3:])
    _layer_body(cfg, x_ref, p_ref, init_refs, *refs)


def _layer_body(cfg,
                  x_ref, p_ref, init_refs,
                  ln_in_g, ln_in_b, w_qkv, w_za, w_beta, w_dec, w_b, w_gate,
                  w_conv_a, a_log, dt_bias, norm_a_g, w_conv_b,
                  w_proj_a, w_proj_b, w_out, ln1_g, ln1_b, w_ple, w_ple_gate, ln2_g, ln2_b,
                  y_ref, ca_ref, s_ref, cb_ref,
                  hbf_s, q_s, k_s, v_s, beta_s, g_s, egl_s, w_s, qd_s, qk_s, cbuf_s, pbg_s, ybf_s):
    u_s, kd_s, o_s = v_s, k_s, q_s
    nseg, seg, ch, carry, zero_init = cfg
    tt = nseg * seg
    nch = tt // ch
    nch_seg = seg // ch
    p1_chunks = min(P1_CHUNKS, nch)
    t_idx = pl.program_id(1)

    def _seed():
        if zero_init:
            ca_ref[...] = jnp.zeros(ca_ref.shape, F32)
            cb_ref[...] = jnp.zeros(cb_ref.shape, F32)
            s_ref[...] = jnp.zeros(s_ref.shape, F32)
        else:
            hista_ref, sin_ref, histb_ref = init_refs
            ca_ref[...] = hista_ref[...]
            cb_ref[...] = histb_ref[...]
            s_ref[...] = sin_ref[...]

    if carry:
        pl.when(t_idx == 0)(_seed)
    else:
        _seed()

    def conv_tile(pre, hist_ref, col, w_ref, ntaps, seqs=None):
        seqs = range(nseg) if seqs is None else seqs
        outs = []
        for n, i in enumerate(seqs):
            part = pre[n * seg:(n + 1) * seg]
            outs.append(_causal_conv(part, hist_ref[i, :, col], w_ref.at[:, col], ntaps,
                                     cbuf_s))
            hist_ref[i, :, col] = part[seg - SUBLANES:]
        return outs[0] if len(outs) == 1 else jnp.concatenate(outs, axis=0)

    halves = []
    for k in range(2):
        sq = slice(k * nseg // 2, (k + 1) * nseg // 2)
        halves.append({"rs": slice(k * tt // 2, (k + 1) * tt // 2), "sq": sq,
                       "seqs": range(sq.start, sq.stop)})

    for hf in halves:
        h = _layer_norm(x_ref[hf["sq"]].reshape(tt // 2, D_MODEL), ln_in_g[...], ln_in_b[...])
        y_ref[hf["sq"]] = h.reshape(nseg // 2, seg, D_MODEL)
        hbf_s[hf["rs"], :] = h.astype(BF16)

    for j in range(QKV_DIM // QKV_CHUNK):
        col = slice(j * QKV_CHUNK, (j + 1) * QKV_CHUNK)
        grp, sub = divmod(j, KEY_DIM // QKV_CHUNK)
        dst = (q_s, k_s, v_s)[grp]
        for hf in halves:
            hf["pre"] = jnp.dot(hbf_s[hf["rs"], :], w_qkv[:, col], preferred_element_type=F32)
        for hf in halves:
            act = _silu(conv_tile(hf.pop("pre"), ca_ref, col, w_conv_a, CONV_A, hf["seqs"]))
            for hh in range(QKV_CHUNK // HEAD_D):
                hc = slice(sub * QKV_CHUNK + hh * HEAD_D, sub * QKV_CHUNK + (hh + 1) * HEAD_D)
                xh = act[:, hh * HEAD_D:(hh + 1) * HEAD_D]
                if grp == 2:
                    dst[hf["rs"], hc] = xh
                else:
                    scale = HEAD_D ** -0.5 if grp == 0 else 1.0
                    ss = jnp.sum(xh * xh, axis=-1, keepdims=True)
                    dst[hf["rs"], hc] = xh * (lax.rsqrt(ss + L2_EPS) * scale)

    beta_s[...] = _sigmoid(jnp.dot(hbf_s[...], w_beta[...], preferred_element_type=F32))
    zdec = jnp.dot(hbf_s[...], w_dec[...], preferred_element_type=F32)
    g_s[...] = -jnp.exp(a_log[...]) * _softplus(zdec + dt_bias[...])

    fill_queue = []
    held = {}

    def fill():
        if fill_queue:
            fill_queue.pop(0)()

    def b_proj(k, cj):
        return jnp.dot(hbf_s[...], w_b[:, k * WIDTH_B + cj.start:k * WIDTH_B + cj.stop],
                       preferred_element_type=F32)

    def b_tile_steps(cj):
        def step_c():
            held["cb"] = b_proj(1, cj)

        def step_conv():
            cu = held.pop("cb") * b_proj(2, cj)
            pbg_s[:, cj] = conv_tile(cu, cb_ref, cj, w_conv_b, CONV_B)

        def step_b():
            pbg_s[:, cj] = b_proj(0, cj) * pbg_s[:, cj]

        def step_gate():
            ybf_s[:, cj] = (pbg_s[:, cj] * _silu(b_proj(3, cj))).astype(BF16)

        return [step_c, step_conv, step_b, step_gate]

    def b_out_steps(cj):
        def step_proj():
            pbg_s[:, cj] = jnp.dot(ybf_s[...], w_proj_b[:, cj], preferred_element_type=F32)

        def step_merge_gate():
            gate_b = _sigmoid(jnp.dot(hbf_s[...], w_gate[:, D_MODEL + cj.start:D_MODEL + cj.stop],
                                      preferred_element_type=F32))
            pbg_s[:, cj] = gate_b * pbg_s[:, cj]

        return [step_proj, step_merge_gate]

    b_tiles = [slice(j * QKV_CHUNK, (j + 1) * QKV_CHUNK) for j in range(WIDTH_B // QKV_CHUNK)]
    for cj in b_tiles:
        fill_queue += b_tile_steps(cj)
    for cj in b_tiles:
        fill_queue += b_out_steps(cj)

    pw = 2 * ch
    ri = lax.broadcasted_iota(jnp.int32, (ch, pw), 0)
    cn = lax.broadcasted_iota(jnp.int32, (ch, pw), 1)
    cj = cn & (ch - 1)
    hi = cn >= ch
    hi_row = lax.broadcasted_iota(jnp.int32, (1, pw), 1) >= ch
    causal = ri >= cj
    strict = ri > cj
    rt = lax.broadcasted_iota(jnp.int32, (ch, ch), 0)
    ct = lax.broadcasted_iota(jnp.int32, (ch, ch), 1)
    ltri = (rt >= ct).astype(F32)

    pairs = range(N_HEADS // 2)
    pcols = [slice(p * 2 * HEAD_D, (p + 1) * 2 * HEAD_D) for p in pairs]
    hcols = [slice(hd * HEAD_D, (hd + 1) * HEAD_D) for hd in range(N_HEADS)]

    def bd_pair(y):
        zero = jnp.zeros_like(y)
        return jnp.concatenate([jnp.where(hi, zero, y), jnp.where(hi, y, zero)], axis=0)

    def bd_wide(y):
        return _block_diag2(y[:, :HEAD_D], y[:, HEAD_D:])

    def mm_pairs(xs, ys):
        out = [jnp.dot(x.astype(BF16), bd_pair(y.astype(BF16)), preferred_element_type=F32)
               for x, y in zip(xs, ys)]
        fill()
        return out

    def lanes_of_pair(cols, p):
        return jnp.concatenate([jnp.broadcast_to(cols[:, 2 * p:2 * p + 1], (ch, HEAD_D)),
                                jnp.broadcast_to(cols[:, 2 * p + 1:2 * p + 2], (ch, HEAD_D))],
                               axis=1)

    def block_rows(blk):
        start = blk * ch
        return pl.ds(start if isinstance(start, int) else pl.multiple_of(start, ch), ch)

    per_seq = max(p1_chunks // nseg, 1) if nch_seg > 1 else 0

    def phase1(it, carry_):
        units = []
        pre = {}
        for j in range(p1_chunks):
            if nch_seg == 1:
                c = it * p1_chunks + j
            else:
                c = (j // per_seq) * nch_seg + it * per_seq + j % per_seq
            rows = block_rows(c)
            g_c = g_s[rows, :]
            beta_c = beta_s[rows, :]
            for p in pairs:
                units.append((j, rows, p))
                pre[j, p] = (k_s[rows, pcols[p]], q_s[rows, pcols[p]], v_s[rows, pcols[p]])
            gc = jnp.dot(ltri, g_c, preferred_element_type=F32,
                         precision=lax.Precision.HIGHEST)
            gc_rows = jnp.concatenate([gc, gc], axis=0).T
            glast = gc[ch - 1:ch, :]
            pre[j] = (c, gc, gc_rows, beta_c, jnp.exp(gc), jnp.exp(glast - gc), jnp.exp(glast))

        dmat, lhs, rk, rhs = [], [], [], []
        for j, rows, p in units:
            kp, qp, vp = pre[j, p]
            _, gc, gc_rows, beta_c, egc, _, _ = pre[j]
            g_col = jnp.where(hi, gc[:, 2 * p + 1:2 * p + 2], gc[:, 2 * p:2 * p + 1])
            g_row = jnp.where(hi_row, gc_rows[2 * p + 1:2 * p + 2, :], gc_rows[2 * p:2 * p + 1, :])
            dmat.append(jnp.where(causal, jnp.exp(jnp.where(causal, g_col - g_row, 0.0)), 0.0))
            kb = kp * lanes_of_pair(beta_c, p)
            lhs.append(jnp.concatenate([kb, qp], axis=0).astype(BF16))
            rk.append(bd_wide(kp.astype(BF16)))
            rhs.append((vp * lanes_of_pair(beta_c, p), kb * lanes_of_pair(egc, p)))
        kkqk = [lax.dot_general(a, b, (((1,), (1,)), ((), ())), preferred_element_type=F32)
                for a, b in zip(lhs, rk)]
        fill()
        a_mat = [jnp.where(strict, x[:ch] * d, 0.0) for x, d in zip(kkqk, dmat)]
        t_x = _inv_unit_lower(a_mat, ri, cj, ch, mm_pairs)
        duw = [jnp.dot(t.astype(BF16),
                       jnp.concatenate([bd_wide(ru.astype(BF16)), bd_wide(rw.astype(BF16))], axis=1),
                       preferred_element_type=F32)
               for t, (ru, rw) in zip(t_x, rhs)]

        fill()
        for i, (j, rows, p) in enumerate(units):
            kp, qp = k_s[rows, pcols[p]], q_s[rows, pcols[p]]
            _, _, _, _, egc, ekd, _ = pre[j]
            qk_s[p, rows, :] = (kkqk[i][ch:] * dmat[i]).astype(BF16)
            u_s[rows, pcols[p]] = rhs[i][0] + duw[i][:, :2 * HEAD_D]
            w_s[rows, pcols[p]] = (rhs[i][1] + duw[i][:, 2 * HEAD_D:]).astype(BF16)
            qd_s[rows, pcols[p]] = (qp * lanes_of_pair(egc, p)).astype(BF16)
            kd_s[rows, pcols[p]] = kp * lanes_of_pair(ekd, p)
        for j in range(p1_chunks):
            egl_s[pl.ds(pre[j][0], 1), :] = pre[j][6]
        return carry_


    ngroups = nseg // P2_SEGS
    assert ngroups == 1 or nch_seg == 1

    def phase2(it, carry_):
        c, sg = (it, 0) if ngroups == 1 else (0, it)
        units = []
        egl = {}
        for i in range(P2_SEGS):
            sq = sg * P2_SEGS + i
            blk = sq * nch_seg + c
            rows = block_rows(blk)
            egl[i] = egl_s[pl.ds(blk, 1), :]
            units += [(i, sq, rows, p) for p in pairs]
        s_old = {(i, hd): s_ref[sq, hd] for i, sq, _, p in units for hd in (2 * p, 2 * p + 1)}
        wq = [jnp.concatenate([w_s[rows, pcols[p]], qd_s[rows, pcols[p]]], axis=0)
              for _, _, rows, p in units]
        u = [u_s[rows, pcols[p]] for _, _, rows, p in units]
        qk = [qk_s[p, rows, :] for _, _, rows, p in units]
        kd_t = {(i, hd): kd_s[rows, hcols[hd]].T.astype(BF16)
                for i, _, rows, p in units for hd in (2 * p, 2 * p + 1)}
        s_bd = [_block_diag2(s_old[i, 2 * p].astype(BF16), s_old[i, 2 * p + 1].astype(BF16))
                for i, _, _, p in units]
        ws = [jnp.dot(a, sb, preferred_element_type=F32) for a, sb in zip(wq, s_bd)]
        v_new = [(a - b[:ch]).astype(BF16) for a, b in zip(u, ws)]
        o = [b[ch:] + jnp.dot(a, bd_wide(v), preferred_element_type=F32)
             for a, b, v in zip(qk, ws, v_new)]
        s_new = {}
        for n, (i, _, _, p) in enumerate(units):
            for half, hd in enumerate((2 * p, 2 * p + 1)):
                s_new[i, hd] = s_old[i, hd] * egl[i][:, hd:hd + 1] + jnp.dot(
                    kd_t[i, hd], v_new[n][:, half * HEAD_D:(half + 1) * HEAD_D],
                    preferred_element_type=F32)
        for n, (i, sq, rows, p) in enumerate(units):
            o_s[rows, pcols[p]] = o[n]
            s_ref[sq, 2 * p] = s_new[i, 2 * p]
            s_ref[sq, 2 * p + 1] = s_new[i, 2 * p + 1]
        return carry_

    n_p1, n_p2 = nch // p1_chunks, nch_seg * ngroups
    phase1(0, 0)
    done2 = 0
    for it1 in range(1, n_p1 + 1):
        if it1 < n_p1:
            phase1(it1, 0)
        ready2 = n_p2 * it1 // n_p1
        for it2 in range(done2, ready2):
            phase2(it2, 0)
        done2 = ready2
    while fill_queue:
        fill()

    def stage_gate_proj(hf):
        lhs = hbf_s[hf["rs"], :]
        hf["za"] = jnp.dot(lhs, w_za[...], preferred_element_type=F32)
        hf["ga"] = jnp.dot(lhs, w_gate[:, :D_MODEL], preferred_element_type=F32)

    def stage_gated_norm(hf):
        ya = []
        for hd in range(N_HEADS):
            hc = slice(hd * HEAD_D, (hd + 1) * HEAD_D)
            oh = o_s[hf["rs"], hc]
            ms = jnp.mean(oh * oh, axis=-1, keepdims=True)
            ya.append((oh * lax.rsqrt(ms + RMS_EPS) * norm_a_g[...]
                       * _silu(hf["za"][:, hc])).astype(BF16))
        hf["ya"] = jnp.concatenate(ya, axis=1)

    def stage_merge(hf):
        pa = jnp.dot(hf.pop("ya"), w_proj_a[...], preferred_element_type=F32)
        hf["merged"] = (_sigmoid(hf.pop("ga")) * pa + pbg_s[hf["rs"], :]).astype(BF16)

    def stage_out_proj(hf):
        hf["mo"] = jnp.dot(hf.pop("merged"), w_out[...], preferred_element_type=F32)

    def stage_norm1(hf):
        h_in = y_ref[hf["sq"]].reshape(tt // 2, D_MODEL)
        hf["h1"] = _layer_norm(ALPHA * h_in + hf.pop("mo"), ln1_g[...], ln1_b[...])

    def stage_ple_proj(hf):
        hf["pg"] = jnp.dot(hf["h1"].astype(BF16), w_ple_gate[...], preferred_element_type=F32)
        hf["pp"] = jnp.dot(p_ref[0, hf["sq"]].reshape(tt // 2, P_DIM).astype(BF16), w_ple[...],
                           preferred_element_type=F32)

    def stage_norm2(hf):
        ple = _sigmoid(hf.pop("pg")) * hf.pop("pp")
        y_ref[hf["sq"]] = _layer_norm(ALPHA * hf.pop("h1") + ple, ln2_g[...], ln2_b[...]).reshape(
            nseg // 2, seg, D_MODEL)

    for stage in (stage_gate_proj, stage_gated_norm, stage_merge, stage_out_proj, stage_norm1,
                  stage_ple_proj, stage_norm2):
        for hf in halves:
            stage(hf)


def _encode(x, p, init_state, weights, *, nseg, seg, ch):
    nseq, t_len, _ = x.shape
    zero_init = init_state is None
    state_shapes = ((nseq, SUBLANES, QKV_DIM), (nseq, N_HEADS, HEAD_D, HEAD_D),
                    (nseq, SUBLANES, WIDTH_B))
    nb, nt = nseq // nseg, t_len // seg
    carry = nt > 1
    assert nseq % nseg == 0 and t_len % seg == 0 and seg % ch == 0 and seg % SUBLANES == 0
    assert ch & (ch - 1) == 0 and ch % INV_BLOCK == 0 and nseg % P2_SEGS == 0
    tt = nseg * seg
    nch = tt // ch
    cfg = (nseg, seg, ch, carry, zero_init)

    def tile_spec(width):
        return pl.BlockSpec((nseg, seg, width), lambda b, t: (b, t, 0))

    def state_spec(shape):
        nd = len(shape)
        return pl.BlockSpec((nseg,) + tuple(shape[1:]), lambda b, t: (b,) + (0,) * (nd - 1),
                            pipeline_mode=pl.Buffered(1))

    def const_spec(arr):
        nd = arr.ndim
        return pl.BlockSpec(arr.shape, lambda b, t: (0,) * nd, pipeline_mode=pl.Buffered(1))

    def cols_spec(arr, width, offset):
        assert offset % width == 0
        return pl.BlockSpec((arr.shape[0], width), lambda b, t: (0, offset // width),
                            pipeline_mode=pl.Buffered(1))

    w_in_packed, others = weights
    p_spec = pl.BlockSpec((1, nseg, seg, P_DIM), lambda b, t: (0, b, t, 0))
    in_specs = [tile_spec(D_MODEL), p_spec]
    operands = [x, p]
    if not zero_init:
        assert tuple(a.shape for a in init_state) == state_shapes
        in_specs += [state_spec(shape) for shape in state_shapes]
        operands += list(init_state)
    for name, arr in others:
        if name == "w_in":
            for width, offset in PACKED_COLS:
                in_specs.append(cols_spec(w_in_packed, width, offset))
                operands.append(w_in_packed)
        else:
            in_specs.append(const_spec(arr))
            operands.append(arr)
    out_shape = (jax.ShapeDtypeStruct(x.shape, F32),) + tuple(
        jax.ShapeDtypeStruct(shape, F32) for shape in state_shapes)
    out_specs = (tile_spec(D_MODEL),) + tuple(state_spec(shape) for shape in state_shapes)
    scratch = [
        pltpu.VMEM((tt, D_MODEL), BF16),
        pltpu.VMEM((tt, KEY_DIM), F32),
        pltpu.VMEM((tt, KEY_DIM), F32),
        pltpu.VMEM((tt, KEY_DIM), F32),
        pltpu.VMEM((tt, LANES), F32),
        pltpu.VMEM((tt, LANES), F32),
        pltpu.VMEM((max(nch, SUBLANES), LANES), F32),
        pltpu.VMEM((tt, KEY_DIM), BF16),
        pltpu.VMEM((tt, KEY_DIM), BF16),
        pltpu.VMEM((N_HEADS // 2, tt, 2 * ch), BF16),
        pltpu.VMEM((SUBLANES + seg, QKV_CHUNK), F32),
        pltpu.VMEM((tt, D_MODEL), F32),
        pltpu.VMEM((tt, WIDTH_B), BF16),
    ]
    return pl.pallas_call(
        functools.partial(_layer_kernel, cfg),
        grid=(nb, nt),
        in_specs=in_specs,
        out_specs=out_specs,
        out_shape=out_shape,
        scratch_shapes=scratch,
        compiler_params=pltpu.CompilerParams(
            dimension_semantics=("arbitrary", "arbitrary"),
            vmem_limit_bytes=VMEM_LIMIT_BYTES),
        name=f"gdn_shortconv_layer_n{nseg}_t{seg}_c{ch}",
    )(*operands)


def _pack_kernel(a_ref, b_ref, o_ref):
    j = pl.program_id(0)
    n_plain = OFF_BETA // PACK_W
    n_shift = (IN_DIM - OFF_BB) // PACK_W
    shift = OFF_BB - OFF_BETA
    row = lax.broadcasted_iota(jnp.int32, (LANES, 1), 0)

    @pl.when(j < n_plain)
    def _():
        o_ref[...] = a_ref[...].T.astype(BF16)

    @pl.when((j >= n_plain) & (j < n_plain + n_shift))
    def _():
        o_ref[...] = jnp.concatenate([a_ref[shift:, :], b_ref[...]], axis=0).T.astype(BF16)

    @pl.when(j == n_plain + n_shift)
    def _():
        beta = jnp.where(row < N_HEADS, a_ref[0:LANES, :], 0.0)
        decay = jnp.where(row < N_HEADS, a_ref[N_HEADS:N_HEADS + LANES, :], 0.0)
        o_ref[:, :LANES] = beta.T.astype(BF16)
        o_ref[:, LANES:2 * LANES] = decay.T.astype(BF16)
        o_ref[:, 2 * LANES:] = jnp.zeros((o_ref.shape[0], PACK_W - 2 * LANES), BF16)


def _pack_w_in(w_t):
    n_plain = OFF_BETA // PACK_W
    n_shift = (IN_DIM - OFF_BB) // PACK_W
    shift = OFF_BB - OFF_BETA
    assert OFF_BETA % PACK_W == 0 and (IN_DIM - OFF_BB) % PACK_W == 0
    assert OFF_DECAY - OFF_BETA == N_HEADS == SUBLANES and shift % SUBLANES == 0

    def a_idx(j):
        return (0, jnp.where(j < n_plain + n_shift, j, n_plain), 0)

    def b_idx(j):
        return (0, jnp.where((j >= n_plain) & (j < n_plain + n_shift), (j + 1) * (PACK_W // shift), 0), 0)

    return pl.pallas_call(
        _pack_kernel,
        grid=(n_plain + n_shift + 1,),
        in_specs=[pl.BlockSpec((None, PACK_W, D_MODEL), a_idx),
                  pl.BlockSpec((None, shift, D_MODEL), b_idx)],
        out_specs=pl.BlockSpec((D_MODEL, PACK_W), lambda j: (0, j)),
        out_shape=jax.ShapeDtypeStruct((D_MODEL, PACKED_DIM), BF16),
        compiler_params=pltpu.CompilerParams(dimension_semantics=("arbitrary",)),
        name="pack_w_in",
    )(w_t, w_t)


def _pad_rows_front(a, rows):
    pad = [(0, 0)] * a.ndim
    pad[-2] = (rows - a.shape[-2], 0)
    return jnp.pad(a, pad)


def _pad_lanes(a):
    pad = [(0, 0)] * a.ndim
    pad[-1] = (0, LANES - a.shape[-1])
    return jnp.pad(a, pad)


def kernel(x_prompt, x_sample, state_conv_a, state_gdn, state_conv_b, p_prompt, p_sample, ln_in_g, ln_in_b, w_in, w_conv_a, a_log, dt_bias, norm_a_g, w_conv_b, w_proj_a, w_proj_b, w_out, ln1_g, ln1_b, w_ple, w_ple_gate, ln2_g, ln2_b):
    assert w_in.shape[0] == DEPTH == 1
    row = lambda v: v.reshape(1, -1).astype(F32)
    w_in_packed = _pack_w_in(jnp.swapaxes(w_in, 1, 2))
    others = (
        ("ln_in_g", row(ln_in_g)), ("ln_in_b", row(ln_in_b)),
        ("w_in", None),
        ("w_conv_a", w_conv_a[0].astype(F32)),
        ("a_log", _pad_lanes(row(a_log[0]))), ("dt_bias", _pad_lanes(row(dt_bias[0]))),
        ("norm_a_g", row(norm_a_g[0])),
        ("w_conv_b", w_conv_b[0].astype(F32)),
        ("w_proj_a", w_proj_a[0].astype(BF16)), ("w_proj_b", w_proj_b[0].astype(BF16)),
        ("w_out", w_out[0].astype(BF16)),
        ("ln1_g", row(ln1_g[0])), ("ln1_b", row(ln1_b[0])),
        ("w_ple", w_ple[0].astype(BF16)), ("w_ple_gate", w_ple_gate[0].astype(BF16)),
        ("ln2_g", row(ln2_g[0])), ("ln2_b", row(ln2_b[0])),
    )
    weights = (w_in_packed, others)

    bp, seq, _ = x_prompt.shape
    y_p, ca_p, s_p, cb_p = _encode(
        x_prompt, p_prompt, None, weights, nseg=bp, seg=PROMPT_TILE, ch=PROMPT_CHUNK)

    bs, ts, _ = x_sample.shape
    y_s, ca_s, s_s, cb_s = _encode(
        x_sample, p_sample,
        (_pad_rows_front(state_conv_a[0], SUBLANES), state_gdn[0].astype(F32),
         _pad_rows_front(state_conv_b[0], SUBLANES)),
        weights, nseg=SAMPLE_TILE_SEQS, seg=ts, ch=ts)

    na, nb = CONV_A - 1, CONV_B - 1
    return (y_p, y_s,
            ca_p[None, :, SUBLANES - na:], s_p[None], cb_p[None, :, SUBLANES - nb:],
            ca_s[None, :, SUBLANES - na:], s_s[None].astype(state_gdn.dtype),
            cb_s[None, :, SUBLANES - nb:])
```

```python
import functools

import jax
import jax.numpy as jnp
from jax import lax
from jax.experimental import pallas as pl
from jax.experimental.pallas import tpu as pltpu

D_MODEL = 1024
N_HEADS = 8
HEAD_D = 128
KEY_DIM = N_HEADS * HEAD_D
QKV_DIM = 3 * KEY_DIM
WIDTH_B = D_MODEL
P_DIM = 256
CONV_A = 4
CONV_B = 3
PROMPT_CHUNK = 64
DEPTH = 1
ALPHA = (2 * DEPTH) ** 0.25
LN_EPS = 1e-5
RMS_EPS = 1e-6
L2_EPS = 1e-6

OFF_ZA = QKV_DIM
OFF_BETA = OFF_ZA + KEY_DIM
OFF_DECAY = OFF_BETA + N_HEADS
OFF_BB = OFF_DECAY + N_HEADS
OFF_CB = OFF_BB + WIDTH_B
OFF_UB = OFF_CB + WIDTH_B
OFF_ZB = OFF_UB + WIDTH_B
OFF_GATE = OFF_ZB + WIDTH_B
IN_DIM = OFF_GATE + 2 * D_MODEL

SUBLANES = 8
LANES = 128
INV_BLOCK = 4
PROMPT_TILE = 256
SAMPLE_TILE_SEQS = 8
P1_CHUNKS = 4
QKV_CHUNK = 256
P2_SEGS = 2
VMEM_LIMIT_BYTES = 127 * 512 * 1024

PACK_W = 512
PACKED_DIM = IN_DIM - (OFF_BB - OFF_BETA) + 2 * LANES
PACKED_COLS = ((QKV_DIM, 0), (KEY_DIM, OFF_ZA), (LANES, OFF_BETA + 4 * WIDTH_B + 2 * D_MODEL),
               (LANES, OFF_BETA + 4 * WIDTH_B + 2 * D_MODEL + LANES), (4 * WIDTH_B, OFF_BETA),
               (2 * D_MODEL, OFF_BETA + 4 * WIDTH_B))

F32 = jnp.float32
NEG_LOG2_E = -1.4426950408889634
BF16 = jnp.bfloat16


def _sigmoid(x):
    return 1.0 / (1.0 + jnp.exp2(x * NEG_LOG2_E))


def _silu(x):
    return x * _sigmoid(x)


def _softplus(x):
    return jnp.maximum(x, 0.0) + jnp.log(1.0 + jnp.exp(-jnp.abs(x)))


def _layer_norm(x, g, b):
    mu = jnp.mean(x, axis=-1, keepdims=True)
    xc = x - mu
    var = jnp.mean(xc * xc, axis=-1, keepdims=True)
    return xc * lax.rsqrt(var + LN_EPS) * g + b


def _causal_conv(seg, hist8, w_ref, ntaps, buf):
    n = seg.shape[0]
    buf[0:SUBLANES, :] = hist8
    buf[SUBLANES:SUBLANES + n, :] = seg
    acc = seg * w_ref[ntaps - 1:ntaps, :]
    for s in range(1, ntaps):
        acc = acc + buf[SUBLANES - s:SUBLANES - s + n, :] * w_ref[ntaps - 1 - s:ntaps - s, :]
    return acc


def _block_diag2(y1, y2):
    z = jnp.zeros_like(y1)
    return jnp.concatenate([jnp.concatenate([y1, z], axis=1), jnp.concatenate([z, y2], axis=1)],
                           axis=0)


def _inv_unit_lower(a_list, row, col, size, mm_each):
    base = (row // INV_BLOCK) == (col // INV_BLOCK)
    d = [jnp.where(base, a, 0.0) for a in a_list]
    d2 = mm_each(d, d)
    dd2 = mm_each(d, d2)
    x = [b - a - c for a, b, c in zip(d, d2, dd2)]
    s = INV_BLOCK
    while s < size:
        pair = ((row // (2 * s)) == (col // (2 * s))) & ((row // s) != (col // s))
        a21 = [jnp.where(pair, a, 0.0) for a in a_list]
        m1 = [a + b for a, b in zip(a21, mm_each(a21, x))]
        m2 = [a + b for a, b in zip(m1, mm_each(x, m1))]
        x = [a - b for a, b in zip(x, m2)]
        s *= 2
    return x


def _layer_kernel(cfg, x_ref, p_ref, *refs):
    init_refs, refs = (None, refs) if cfg[4] else (refs[:3], refs[3:])
    _layer_body(cfg, x_ref, p_ref, init_refs, *refs)


def _layer_body(cfg,
                  x_ref, p_ref, init_refs,
                  ln_in_g, ln_in_b, w_qkv, w_za, w_beta, w_dec, w_b, w_gate,
                  w_conv_a, a_log, dt_bias, norm_a_g, w_conv_b,
                  w_proj_a, w_proj_b, w_out, ln1_g, ln1_b, w_ple, w_ple_gate, ln2_g, ln2_b,
                  y_ref, ca_ref, s_ref, cb_ref,
                  hbf_s, q_s, k_s, v_s, beta_s, g_s, egl_s, w_s, qd_s, qk_s, cbuf_s, pbg_s, ybf_s):
    u_s, kd_s, o_s = v_s, k_s, q_s
    nseg, seg, ch, carry, zero_init = cfg
    tt = nseg * seg
    nch = tt // ch
    nch_seg = seg // ch
    p1_chunks = min(P1_CHUNKS, nch)
    t_idx = pl.program_id(1)

    def _seed():
        if zero_init:
            ca_ref[...] = jnp.zeros(ca_ref.shape, F32)
            cb_ref[...] = jnp.zeros(cb_ref.shape, F32)
            s_ref[...] = jnp.zeros(s_ref.shape, F32)
        else:
            hista_ref, sin_ref, histb_ref = init_refs
            ca_ref[...] = hista_ref[...]
            cb_ref[...] = histb_ref[...]
            s_ref[...] = sin_ref[...]

    if carry:
        pl.when(t_idx == 0)(_seed)
    else:
        _seed()

    def conv_tile(pre, hist_ref, col, w_ref, ntaps, seqs=None):
        seqs = range(nseg) if seqs is None else seqs
        outs = []
        for n, i in enumerate(seqs):
            part = pre[n * seg:(n + 1) * seg]
            outs.append(_causal_conv(part, hist_ref[i, :, col], w_ref.at[:, col], ntaps,
                                     cbuf_s))
            hist_ref[i, :, col] = part[seg - SUBLANES:]
        return outs[0] if len(outs) == 1 else jnp.concatenate(outs, axis=0)

    halves = []
    for k in range(2):
        sq = slice(k * nseg // 2, (k + 1) * nseg // 2)
        halves.append({"rs": slice(k * tt // 2, (k + 1) * tt // 2), "sq": sq,
                       "seqs": range(sq.start, sq.stop)})

    for hf in halves:
        h = _layer_norm(x_ref[hf["sq"]].reshape(tt // 2, D_MODEL), ln_in_g[...], ln_in_b[...])
        y_ref[hf["sq"]] = h.reshape(nseg // 2, seg, D_MODEL)
        hbf_s[hf["rs"], :] = h.astype(BF16)

    for j in range(QKV_DIM // QKV_CHUNK):
        col = slice(j * QKV_CHUNK, (j + 1) * QKV_CHUNK)
        grp, sub = divmod(j, KEY_DIM // QKV_CHUNK)
        dst = (q_s, k_s, v_s)[grp]
        for hf in halves:
            hf["pre"] = jnp.dot(hbf_s[hf["rs"], :], w_qkv[:, col], preferred_element_type=F32)
        for hf in halves:
            act = _silu(conv_tile(hf.pop("pre"), ca_ref, col, w_conv_a, CONV_A, hf["seqs"]))
            for hh in range(QKV_CHUNK // HEAD_D):
                hc = slice(sub * QKV_CHUNK + hh * HEAD_D, sub * QKV_CHUNK + (hh + 1) * HEAD_D)
                xh = act[:, hh * HEAD_D:(hh + 1) * HEAD_D]
                if grp == 2:
                    dst[hf["rs"], hc] = xh
                else:
                    scale = HEAD_D ** -0.5 if grp == 0 else 1.0
                    ss = jnp.sum(xh * xh, axis=-1, keepdims=True)
                    dst[hf["rs"], hc] = xh * (lax.rsqrt(ss + L2_EPS) * scale)

    beta_s[...] = _sigmoid(jnp.dot(hbf_s[...], w_beta[...], preferred_element_type=F32))
    zdec = jnp.dot(hbf_s[...], w_dec[...], preferred_element_type=F32)
    g_s[...] = -jnp.exp(a_log[...]) * _softplus(zdec + dt_bias[...])

    fill_queue = []
    held = {}

    def fill():
        if fill_queue:
            fill_queue.pop(0)()

    def b_proj(k, cj):
        return jnp.dot(hbf_s[...], w_b[:, k * WIDTH_B + cj.start:k * WIDTH_B + cj.stop],
                       preferred_element_type=F32)

    def b_tile_steps(cj):
        def step_c():
            held["cb"] = b_proj(1, cj)

        def step_conv():
            cu = held.pop("cb") * b_proj(2, cj)
            pbg_s[:, cj] = conv_tile(cu, cb_ref, cj, w_conv_b, CONV_B)

        def step_b():
            pbg_s[:, cj] = b_proj(0, cj) * pbg_s[:, cj]

        def step_gate():
            ybf_s[:, cj] = (pbg_s[:, cj] * _silu(b_proj(3, cj))).astype(BF16)

        return [step_c, step_conv, step_b, step_gate]

    def b_out_steps(cj):
        def step_proj():
            pbg_s[:, cj] = jnp.dot(ybf_s[...], w_proj_b[:, cj], preferred_element_type=F32)

        def step_merge_gate():
            gate_b = _sigmoid(jnp.dot(hbf_s[...], w_gate[:, D_MODEL + cj.start:D_MODEL + cj.stop],
                                      preferred_element_type=F32))
            pbg_s[:, cj] = gate_b * pbg_s[:, cj]

        return [step_proj, step_merge_gate]

    b_tiles = [slice(j * QKV_CHUNK, (j + 1) * QKV_CHUNK) for j in range(WIDTH_B // QKV_CHUNK)]
    for cj in b_tiles:
        fill_queue += b_tile_steps(cj)
    for cj in b_tiles:
        fill_queue += b_out_steps(cj)

    pw = 2 * ch
    ri = lax.broadcasted_iota(jnp.int32, (ch, pw), 0)
    cn = lax.broadcasted_iota(jnp.int32, (ch, pw), 1)
    cj = cn & (ch - 1)
    hi = cn >= ch
    hi_row = lax.broadcasted_iota(jnp.int32, (1, pw), 1) >= ch
    causal = ri >= cj
    strict = ri > cj
    rt = lax.broadcasted_iota(jnp.int32, (ch, ch), 0)
    ct = lax.broadcasted_iota(jnp.int32, (ch, ch), 1)
    ltri = (rt >= ct).astype(F32)

    pairs = range(N_HEADS // 2)
    pcols = [slice(p * 2 * HEAD_D, (p + 1) * 2 * HEAD_D) for p in pairs]
    hcols = [slice(hd * HEAD_D, (hd + 1) * HEAD_D) for hd in range(N_HEADS)]

    def bd_pair(y):
        zero = jnp.zeros_like(y)
        return jnp.concatenate([jnp.where(hi, zero, y), jnp.where(hi, y, zero)], axis=0)

    def bd_wide(y):
        return _block_diag2(y[:, :HEAD_D], y[:, HEAD_D:])

    def mm_pairs(xs, ys):
        out = [jnp.dot(x.astype(BF16), bd_pair(y.astype(BF16)), preferred_element_type=F32)
               for x, y in zip(xs, ys)]
        fill()
        return out

    def lanes_of_pair(cols, p):
        return jnp.concatenate([jnp.broadcast_to(cols[:, 2 * p:2 * p + 1], (ch, HEAD_D)),
                                jnp.broadcast_to(cols[:, 2 * p + 1:2 * p + 2], (ch, HEAD_D))],
                               axis=1)

    def block_rows(blk):
        start = blk * ch
        return pl.ds(start if isinstance(start, int) else pl.multiple_of(start, ch), ch)

    per_seq = max(p1_chunks // nseg, 1) if nch_seg > 1 else 0

    def phase1(it):
        units = []
        pre = {}
        for j in range(p1_chunks):
            if nch_seg == 1:
                c = it * p1_chunks + j
            else:
                c = (j // per_seq) * nch_seg + it * per_seq + j % per_seq
            rows = block_rows(c)
            g_c = g_s[rows, :]
            beta_c = beta_s[rows, :]
            for p in pairs:
                units.append((j, rows, p))
                pre[j, p] = (k_s[rows, pcols[p]], q_s[rows, pcols[p]], v_s[rows, pcols[p]])
            gc = jnp.dot(ltri, g_c, preferred_element_type=F32,
                         precision=lax.Precision.HIGHEST)
            gc_rows = jnp.concatenate([gc, gc], axis=0).T
            glast = gc[ch - 1:ch, :]
            pre[j] = (c, gc, gc_rows, beta_c, jnp.exp(gc), jnp.exp(glast - gc), jnp.exp(glast))

        dmat, lhs, rk, rhs = [], [], [], []
        for j, rows, p in units:
            kp, qp, vp = pre[j, p]
            _, gc, gc_rows, beta_c, egc, _, _ = pre[j]
            g_col = jnp.where(hi, gc[:, 2 * p + 1:2 * p + 2], gc[:, 2 * p:2 * p + 1])
            g_row = jnp.where(hi_row, gc_rows[2 * p + 1:2 * p + 2, :], gc_rows[2 * p:2 * p + 1, :])
            dmat.append(jnp.where(causal, jnp.exp(jnp.where(causal, g_col - g_row, 0.0)), 0.0))
            kb = kp * lanes_of_pair(beta_c, p)
            lhs.append(jnp.concatenate([kb, qp], axis=0).astype(BF16))
            rk.append(bd_wide(kp.astype(BF16)))
            rhs.append((vp * lanes_of_pair(beta_c, p), kb * lanes_of_pair(egc, p)))
        kkqk = [lax.dot_general(a, b, (((1,), (1,)), ((), ())), preferred_element_type=F32)
                for a, b in zip(lhs, rk)]
        fill()
        a_mat = [jnp.where(strict, x[:ch] * d, 0.0) for x, d in zip(kkqk, dmat)]
        t_x = _inv_unit_lower(a_mat, ri, cj, ch, mm_pairs)
        duw = [jnp.dot(t.astype(BF16),
                       jnp.concatenate([bd_wide(ru.astype(BF16)), bd_wide(rw.astype(BF16))], axis=1),
                       preferred_element_type=F32)
               for t, (ru, rw) in zip(t_x, rhs)]

        fill()
        for i, (j, rows, p) in enumerate(units):
            kp, qp = k_s[rows, pcols[p]], q_s[rows, pcols[p]]
            _, _, _, _, egc, ekd, _ = pre[j]
            qk_s[p, rows, :] = (kkqk[i][ch:] * dmat[i]).astype(BF16)
            u_s[rows, pcols[p]] = rhs[i][0] + duw[i][:, :2 * HEAD_D]
            w_s[rows, pcols[p]] = (rhs[i][1] + duw[i][:, 2 * HEAD_D:]).astype(BF16)
            qd_s[rows, pcols[p]] = (qp * lanes_of_pair(egc, p)).astype(BF16)
            kd_s[rows, pcols[p]] = kp * lanes_of_pair(ekd, p)
        for j in range(p1_chunks):
            egl_s[pl.ds(pre[j][0], 1), :] = pre[j][6]


    ngroups = nseg // P2_SEGS
    assert ngroups == 1 or nch_seg == 1

    def phase2(it):
        c, sg = (it, 0) if ngroups == 1 else (0, it)
        units = []
        egl = {}
        for i in range(P2_SEGS):
            sq = sg * P2_SEGS + i
            blk = sq * nch_seg + c
            rows = block_rows(blk)
            egl[i] = egl_s[pl.ds(blk, 1), :]
            units += [(i, sq, rows, p) for p in pairs]
        s_old = {(i, hd): s_ref[sq, hd] for i, sq, _, p in units for hd in (2 * p, 2 * p + 1)}
        wq = [jnp.concatenate([w_s[rows, pcols[p]], qd_s[rows, pcols[p]]], axis=0)
              for _, _, rows, p in units]
        u = [u_s[rows, pcols[p]] for _, _, rows, p in units]
        qk = [qk_s[p, rows, :] for _, _, rows, p in units]
        kd_t = {(i, hd): kd_s[rows, hcols[hd]].T.astype(BF16)
                for i, _, rows, p in units for hd in (2 * p, 2 * p + 1)}
        s_bd = [_block_diag2(s_old[i, 2 * p].astype(BF16), s_old[i, 2 * p + 1].astype(BF16))
                for i, _, _, p in units]
        ws = [jnp.dot(a, sb, preferred_element_type=F32) for a, sb in zip(wq, s_bd)]
        v_new = [(a - b[:ch]).astype(BF16) for a, b in zip(u, ws)]
        o = [b[ch:] + jnp.dot(a, bd_wide(v), preferred_element_type=F32)
             for a, b, v in zip(qk, ws, v_new)]
        s_new = {}
        for n, (i, _, _, p) in enumerate(units):
            for half, hd in enumerate((2 * p, 2 * p + 1)):
                s_new[i, hd] = s_old[i, hd] * egl[i][:, hd:hd + 1] + jnp.dot(
                    kd_t[i, hd], v_new[n][:, half * HEAD_D:(half + 1) * HEAD_D],
                    preferred_element_type=F32)
        for n, (i, sq, rows, p) in enumerate(units):
            o_s[rows, pcols[p]] = o[n]
            s_ref[sq, 2 * p] = s_new[i, 2 * p]
            s_ref[sq, 2 * p + 1] = s_new[i, 2 * p + 1]

    n_p1, n_p2 = nch // p1_chunks, nch_seg * ngroups
    phase1(0)
    done2 = 0
    for it1 in range(1, n_p1 + 1):
        if it1 < n_p1:
            phase1(it1)
        ready2 = n_p2 * it1 // n_p1
        for it2 in range(done2, ready2):
            phase2(it2)
        done2 = ready2
    while fill_queue:
        fill()

    def stage_gate_proj(hf):
        lhs = hbf_s[hf["rs"], :]
        hf["za"] = jnp.dot(lhs, w_za[...], preferred_element_type=F32)
        hf["ga"] = jnp.dot(lhs, w_gate[:, :D_MODEL], preferred_element_type=F32)

    def stage_gated_norm(hf):
        ya = []
        for hd in range(N_HEADS):
            hc = slice(hd * HEAD_D, (hd + 1) * HEAD_D)
            oh = o_s[hf["rs"], hc]
            ms = jnp.mean(oh * oh, axis=-1, keepdims=True)
            ya.append((oh * lax.rsqrt(ms + RMS_EPS) * norm_a_g[...]
                       * _silu(hf["za"][:, hc])).astype(BF16))
        hf["ya"] = jnp.concatenate(ya, axis=1)

    def stage_merge(hf):
        pa = jnp.dot(hf.pop("ya"), w_proj_a[...], preferred_element_type=F32)
        hf["merged"] = (_sigmoid(hf.pop("ga")) * pa + pbg_s[hf["rs"], :]).astype(BF16)

    def stage_out_proj(hf):
        hf["mo"] = jnp.dot(hf.pop("merged"), w_out[...], preferred_element_type=F32)

    def stage_norm1(hf):
        h_in = y_ref[hf["sq"]].reshape(tt // 2, D_MODEL)
        hf["h1"] = _layer_norm(ALPHA * h_in + hf.pop("mo"), ln1_g[...], ln1_b[...])

    def stage_ple_proj(hf):
        hf["pg"] = jnp.dot(hf["h1"].astype(BF16), w_ple_gate[...], preferred_element_type=F32)
        hf["pp"] = jnp.dot(p_ref[0, hf["sq"]].reshape(tt // 2, P_DIM).astype(BF16), w_ple[...],
                           preferred_element_type=F32)

    def stage_norm2(hf):
        ple = _sigmoid(hf.pop("pg")) * hf.pop("pp")
        y_ref[hf["sq"]] = _layer_norm(ALPHA * hf.pop("h1") + ple, ln2_g[...], ln2_b[...]).reshape(
            nseg // 2, seg, D_MODEL)

    for stage in (stage_gate_proj, stage_gated_norm, stage_merge, stage_out_proj, stage_norm1,
                  stage_ple_proj, stage_norm2):
        for hf in halves:
            stage(hf)


def _encode(x, p, init_state, weights, *, nseg, seg, ch):
    nseq, t_len, _ = x.shape
    zero_init = init_state is None
    state_shapes = ((nseq, SUBLANES, QKV_DIM), (nseq, N_HEADS, HEAD_D, HEAD_D),
                    (nseq, SUBLANES, WIDTH_B))
    nb, nt = nseq // nseg, t_len // seg
    carry = nt > 1
    assert nseq % nseg == 0 and t_len % seg == 0 and seg % ch == 0 and seg % SUBLANES == 0
    assert ch & (ch - 1) == 0 and ch % INV_BLOCK == 0 and nseg % P2_SEGS == 0
    tt = nseg * seg
    nch = tt // ch
    cfg = (nseg, seg, ch, carry, zero_init)

    def tile_spec(width):
        return pl.BlockSpec((nseg, seg, width), lambda b, t: (b, t, 0))

    def state_spec(shape):
        nd = len(shape)
        return pl.BlockSpec((nseg,) + tuple(shape[1:]), lambda b, t: (b,) + (0,) * (nd - 1),
                            pipeline_mode=pl.Buffered(1))

    def const_spec(arr):
        nd = arr.ndim
        return pl.BlockSpec(arr.shape, lambda b, t: (0,) * nd, pipeline_mode=pl.Buffered(1))

    def cols_spec(arr, width, offset):
        assert offset % width == 0
        return pl.BlockSpec((arr.shape[0], width), lambda b, t: (0, offset // width),
                            pipeline_mode=pl.Buffered(1))

    w_in_packed, others = weights
    p_spec = pl.BlockSpec((1, nseg, seg, P_DIM), lambda b, t: (0, b, t, 0))
    in_specs = [tile_spec(D_MODEL), p_spec]
    operands = [x, p]
    if not zero_init:
        assert tuple(a.shape for a in init_state) == state_shapes
        in_specs += [state_spec(shape) for shape in state_shapes]
        operands += list(init_state)
    for name, arr in others:
        if name == "w_in":
            for width, offset in PACKED_COLS:
                in_specs.append(cols_spec(w_in_packed, width, offset))
                operands.append(w_in_packed)
        else:
            in_specs.append(const_spec(arr))
            operands.append(arr)
    out_shape = (jax.ShapeDtypeStruct(x.shape, F32),) + tuple(
        jax.ShapeDtypeStruct(shape, F32) for shape in state_shapes)
    out_specs = (tile_spec(D_MODEL),) + tuple(state_spec(shape) for shape in state_shapes)
    scratch = [
        pltpu.VMEM((tt, D_MODEL), BF16),
        pltpu.VMEM((tt, KEY_DIM), F32),
        pltpu.VMEM((tt, KEY_DIM), F32),
        pltpu.VMEM((tt, KEY_DIM), F32),
        pltpu.VMEM((tt, LANES), F32),
        pltpu.VMEM((tt, LANES), F32),
        pltpu.VMEM((max(nch, SUBLANES), LANES), F32),
        pltpu.VMEM((tt, KEY_DIM), BF16),
        pltpu.VMEM((tt, KEY_DIM), BF16),
        pltpu.VMEM((N_HEADS // 2, tt, 2 * ch), BF16),
        pltpu.VMEM((SUBLANES + seg, QKV_CHUNK), F32),
        pltpu.VMEM((tt, D_MODEL), F32),
        pltpu.VMEM((tt, WIDTH_B), BF16),
    ]
    return pl.pallas_call(
        functools.partial(_layer_kernel, cfg),
        grid=(nb, nt),
        in_specs=in_specs,
        out_specs=out_specs,
        out_shape=out_shape,
        scratch_shapes=scratch,
        compiler_params=pltpu.CompilerParams(
            dimension_semantics=("arbitrary", "arbitrary"),
            vmem_limit_bytes=VMEM_LIMIT_BYTES),
        name=f"gdn_shortconv_layer_n{nseg}_t{seg}_c{ch}",
    )(*operands)


def _pack_kernel(a_ref, b_ref, o_ref):
    j = pl.program_id(0)
    n_plain = OFF_BETA // PACK_W
    n_shift = (IN_DIM - OFF_BB) // PACK_W
    shift = OFF_BB - OFF_BETA
    row = lax.broadcasted_iota(jnp.int32, (LANES, 1), 0)

    @pl.when(j < n_plain)
    def _():
        o_ref[...] = a_ref[...].T.astype(BF16)

    @pl.when((j >= n_plain) & (j < n_plain + n_shift))
    def _():
        o_ref[...] = jnp.concatenate([a_ref[shift:, :], b_ref[...]], axis=0).T.astype(BF16)

    @pl.when(j == n_plain + n_shift)
    def _():
        beta = jnp.where(row < N_HEADS, a_ref[0:LANES, :], 0.0)
        decay = jnp.where(row < N_HEADS, a_ref[N_HEADS:N_HEADS + LANES, :], 0.0)
        o_ref[:, :LANES] = beta.T.astype(BF16)
        o_ref[:, LANES:2 * LANES] = decay.T.astype(BF16)
        o_ref[:, 2 * LANES:] = jnp.zeros((o_ref.shape[0], PACK_W - 2 * LANES), BF16)


def _pack_w_in(w_t):
    n_plain = OFF_BETA // PACK_W
    n_shift = (IN_DIM - OFF_BB) // PACK_W
    shift = OFF_BB - OFF_BETA
    assert OFF_BETA % PACK_W == 0 and (IN_DIM - OFF_BB) % PACK_W == 0
    assert OFF_DECAY - OFF_BETA == N_HEADS == SUBLANES and shift % SUBLANES == 0

    def a_idx(j):
        return (0, jnp.where(j < n_plain + n_shift, j, n_plain), 0)

    def b_idx(j):
        return (0, jnp.where((j >= n_plain) & (j < n_plain + n_shift), (j + 1) * (PACK_W // shift), 0), 0)

    return pl.pallas_call(
        _pack_kernel,
        grid=(n_plain + n_shift + 1,),
        in_specs=[pl.BlockSpec((None, PACK_W, D_MODEL), a_idx),
                  pl.BlockSpec((None, shift, D_MODEL), b_idx)],
        out_specs=pl.BlockSpec((D_MODEL, PACK_W), lambda j: (0, j)),
        out_shape=jax.ShapeDtypeStruct((D_MODEL, PACKED_DIM), BF16),
        compiler_params=pltpu.CompilerParams(dimension_semantics=("arbitrary",)),
        name="pack_w_in",
    )(w_t, w_t)


def _pad_rows_front(a, rows):
    pad = [(0, 0)] * a.ndim
    pad[-2] = (rows - a.shape[-2], 0)
    return jnp.pad(a, pad)


def _pad_lanes(a):
    pad = [(0, 0)] * a.ndim
    pad[-1] = (0, LANES - a.shape[-1])
    return jnp.pad(a, pad)


def kernel(x_prompt, x_sample, state_conv_a, state_gdn, state_conv_b, p_prompt, p_sample, ln_in_g, ln_in_b, w_in, w_conv_a, a_log, dt_bias, norm_a_g, w_conv_b, w_proj_a, w_proj_b, w_out, ln1_g, ln1_b, w_ple, w_ple_gate, ln2_g, ln2_b):
    assert w_in.shape[0] == DEPTH == 1
    row = lambda v: v.reshape(1, -1).astype(F32)
    w_in_packed = _pack_w_in(jnp.swapaxes(w_in, 1, 2))
    others = (
        ("ln_in_g", row(ln_in_g)), ("ln_in_b", row(ln_in_b)),
        ("w_in", None),
        ("w_conv_a", w_conv_a[0].astype(F32)),
        ("a_log", _pad_lanes(row(a_log[0]))), ("dt_bias", _pad_lanes(row(dt_bias[0]))),
        ("norm_a_g", row(norm_a_g[0])),
        ("w_conv_b", w_conv_b[0].astype(F32)),
        ("w_proj_a", w_proj_a[0].astype(BF16)), ("w_proj_b", w_proj_b[0].astype(BF16)),
        ("w_out", w_out[0].astype(BF16)),
        ("ln1_g", row(ln1_g[0])), ("ln1_b", row(ln1_b[0])),
        ("w_ple", w_ple[0].astype(BF16)), ("w_ple_gate", w_ple_gate[0].astype(BF16)),
        ("ln2_g", row(ln2_g[0])), ("ln2_b", row(ln2_b[0])),
    )
    weights = (w_in_packed, others)

    bp, seq, _ = x_prompt.shape
    y_p, ca_p, s_p, cb_p = _encode(
        x_prompt, p_prompt, None, weights, nseg=bp, seg=PROMPT_TILE, ch=PROMPT_CHUNK)

    bs, ts, _ = x_sample.shape
    y_s, ca_s, s_s, cb_s = _encode(
        x_sample, p_sample,
        (_pad_rows_front(state_conv_a[0], SUBLANES), state_gdn[0].astype(F32),
         _pad_rows_front(state_conv_b[0], SUBLANES)),
        weights, nseg=SAMPLE_TILE_SEQS, seg=ts, ch=ts)

    na, nb = CONV_A - 1, CONV_B - 1
    return (y_p, y_s,
            ca_p[None, :, SUBLANES - na:], s_p[None], cb_p[None, :, SUBLANES - nb:],
            ca_s[None, :, SUBLANES - na:], s_s[None].astype(state_gdn.dtype),
            cb_s[None, :, SUBLANES - nb:])
```

```python
import functools

import jax
import jax.numpy as jnp
from jax import lax
from jax.experimental import pallas as pl
from jax.experimental.pallas import tpu as pltpu

D_MODEL = 1024
N_HEADS = 8
HEAD_D = 128
KEY_DIM = N_HEADS * HEAD_D
QKV_DIM = 3 * KEY_DIM
WIDTH_B = D_MODEL
P_DIM = 256
CONV_A = 4
CONV_B = 3
PROMPT_CHUNK = 64
DEPTH = 1
ALPHA = (2 * DEPTH) ** 0.25
LN_EPS = 1e-5
RMS_EPS = 1e-6
L2_EPS = 1e-6

OFF_ZA = QKV_DIM
OFF_BETA = OFF_ZA + KEY_DIM
OFF_DECAY = OFF_BETA + N_HEADS
OFF_BB = OFF_DECAY + N_HEADS
OFF_CB = OFF_BB + WIDTH_B
OFF_UB = OFF_CB + WIDTH_B
OFF_ZB = OFF_UB + WIDTH_B
OFF_GATE = OFF_ZB + WIDTH_B
IN_DIM = OFF_GATE + 2 * D_MODEL

SUBLANES = 8
LANES = 128
INV_BLOCK = 4
PROMPT_TILE = 256
SAMPLE_TILE_SEQS = 8
P1_CHUNKS = 4
QKV_CHUNK = 512
B_TILE = 256
P2_SEGS = 2
VMEM_LIMIT_BYTES = 127 * 512 * 1024

PACK_W = 512
PACKED_DIM = IN_DIM - (OFF_BB - OFF_BETA) + 2 * LANES
PACKED_COLS = ((QKV_DIM, 0), (KEY_DIM, OFF_ZA), (LANES, OFF_BETA + 4 * WIDTH_B + 2 * D_MODEL),
               (LANES, OFF_BETA + 4 * WIDTH_B + 2 * D_MODEL + LANES), (4 * WIDTH_B, OFF_BETA),
               (2 * D_MODEL, OFF_BETA + 4 * WIDTH_B))

F32 = jnp.float32
NEG_LOG2_E = -1.4426950408889634
BF16 = jnp.bfloat16


def _sigmoid(x):
    return 1.0 / (1.0 + jnp.exp2(x * NEG_LOG2_E))


def _silu(x):
    return x * _sigmoid(x)


def _softplus(x):
    return jnp.maximum(x, 0.0) + jnp.log(1.0 + jnp.exp(-jnp.abs(x)))


def _layer_norm(x, g, b):
    mu = jnp.mean(x, axis=-1, keepdims=True)
    xc = x - mu
    var = jnp.mean(xc * xc, axis=-1, keepdims=True)
    return xc * lax.rsqrt(var + LN_EPS) * g + b


def _causal_conv(seg, hist8, w_ref, ntaps, buf):
    n = seg.shape[0]
    buf[0:SUBLANES, :] = hist8
    buf[SUBLANES:SUBLANES + n, :] = seg
    acc = seg * w_ref[ntaps - 1:ntaps, :]
    for s in range(1, ntaps):
        acc = acc + buf[SUBLANES - s:SUBLANES - s + n, :] * w_ref[ntaps - 1 - s:ntaps - s, :]
    return acc


def _block_diag2(y1, y2):
    z = jnp.zeros_like(y1)
    return jnp.concatenate([jnp.concatenate([y1, z], axis=1), jnp.concatenate([z, y2], axis=1)],
                           axis=0)


def _inv_unit_lower(a_list, row, col, size, mm_each):
    base = (row // INV_BLOCK) == (col // INV_BLOCK)
    d = [jnp.where(base, a, 0.0) for a in a_list]
    d2 = mm_each(d, d)
    dd2 = mm_each(d, d2)
    x = [b - a - c for a, b, c in zip(d, d2, dd2)]
    s = INV_BLOCK
    while s < size:
        pair = ((row // (2 * s)) == (col // (2 * s))) & ((row // s) != (col // s))
        a21 = [jnp.where(pair, a, 0.0) for a in a_list]
        m1 = [a + b for a, b in zip(a21, mm_each(a21, x))]
        m2 = [a + b for a, b in zip(m1, mm_each(x, m1))]
        x = [a - b for a, b in zip(x, m2)]
        s *= 2
    return x


def _layer_kernel(cfg, x_ref, p_ref, *refs):
    init_refs, refs = (None, refs) if cfg[4] else (refs[:3], refs[3:])
    _layer_body(cfg, x_ref, p_ref, init_refs, *refs)


def _layer_body(cfg,
                  x_ref, p_ref, init_refs,
                  ln_in_g, ln_in_b, w_qkv, w_za, w_beta, w_dec, w_b, w_gate,
                  w_conv_a, a_log, dt_bias, norm_a_g, w_conv_b,
                  w_proj_a, w_proj_b, w_out, ln1_g, ln1_b, w_ple, w_ple_gate, ln2_g, ln2_b,
                  y_ref, ca_ref, s_ref, cb_ref,
                  hbf_s, q_s, k_s, v_s, beta_s, g_s, egl_s, w_s, qd_s, qk_s, cbuf_s, pbg_s, ybf_s):
    u_s, kd_s, o_s = v_s, k_s, q_s
    nseg, seg, ch, carry, zero_init = cfg
    tt = nseg * seg
    nch = tt // ch
    nch_seg = seg // ch
    p1_chunks = min(P1_CHUNKS, nch)
    t_idx = pl.program_id(1)

    def _seed():
        if zero_init:
            ca_ref[...] = jnp.zeros(ca_ref.shape, F32)
            cb_ref[...] = jnp.zeros(cb_ref.shape, F32)
            s_ref[...] = jnp.zeros(s_ref.shape, F32)
        else:
            hista_ref, sin_ref, histb_ref = init_refs
            ca_ref[...] = hista_ref[...]
            cb_ref[...] = histb_ref[...]
            s_ref[...] = sin_ref[...]

    if carry:
        pl.when(t_idx == 0)(_seed)
    else:
        _seed()

    def conv_tile(pre, hist_ref, col, w_ref, ntaps, seqs=None):
        seqs = range(nseg) if seqs is None else seqs
        outs = []
        for n, i in enumerate(seqs):
            part = pre[n * seg:(n + 1) * seg]
            outs.append(_causal_conv(part, hist_ref[i, :, col], w_ref.at[:, col], ntaps,
                                     cbuf_s.at[:, :pre.shape[1]]))
            hist_ref[i, :, col] = part[seg - SUBLANES:]
        return outs[0] if len(outs) == 1 else jnp.concatenate(outs, axis=0)

    halves = []
    for k in range(2):
        sq = slice(k * nseg // 2, (k + 1) * nseg // 2)
        halves.append({"rs": slice(k * tt // 2, (k + 1) * tt // 2), "sq": sq,
                       "seqs": range(sq.start, sq.stop)})

    for hf in halves:
        h = _layer_norm(x_ref[hf["sq"]].reshape(tt // 2, D_MODEL), ln_in_g[...], ln_in_b[...])
        y_ref[hf["sq"]] = h.reshape(nseg // 2, seg, D_MODEL)
        hbf_s[hf["rs"], :] = h.astype(BF16)

    for j in range(QKV_DIM // QKV_CHUNK):
        col = slice(j * QKV_CHUNK, (j + 1) * QKV_CHUNK)
        grp, sub = divmod(j, KEY_DIM // QKV_CHUNK)
        dst = (q_s, k_s, v_s)[grp]
        for hf in halves:
            hf["pre"] = jnp.dot(hbf_s[hf["rs"], :], w_qkv[:, col], preferred_element_type=F32)
        for hf in halves:
            act = _silu(conv_tile(hf.pop("pre"), ca_ref, col, w_conv_a, CONV_A, hf["seqs"]))
            for hh in range(QKV_CHUNK // HEAD_D):
                hc = slice(sub * QKV_CHUNK + hh * HEAD_D, sub * QKV_CHUNK + (hh + 1) * HEAD_D)
                xh = act[:, hh * HEAD_D:(hh + 1) * HEAD_D]
                if grp == 2:
                    dst[hf["rs"], hc] = xh
                else:
                    scale = HEAD_D ** -0.5 if grp == 0 else 1.0
                    ss = jnp.sum(xh * xh, axis=-1, keepdims=True)
                    dst[hf["rs"], hc] = xh * (lax.rsqrt(ss + L2_EPS) * scale)

    beta_s[...] = _sigmoid(jnp.dot(hbf_s[...], w_beta[...], preferred_element_type=F32))
    zdec = jnp.dot(hbf_s[...], w_dec[...], preferred_element_type=F32)
    g_s[...] = -jnp.exp(a_log[...]) * _softplus(zdec + dt_bias[...])

    fill_queue = []
    held = {}

    def fill():
        if fill_queue:
            fill_queue.pop(0)()

    def b_proj(k, cj):
        return jnp.dot(hbf_s[...], w_b[:, k * WIDTH_B + cj.start:k * WIDTH_B + cj.stop],
                       preferred_element_type=F32)

    def b_tile_steps(cj):
        def step_c():
            held["cb"] = b_proj(1, cj)

        def step_conv():
            cu = held.pop("cb") * b_proj(2, cj)
            pbg_s[:, cj] = conv_tile(cu, cb_ref, cj, w_conv_b, CONV_B)

        def step_b():
            pbg_s[:, cj] = b_proj(0, cj) * pbg_s[:, cj]

        def step_gate():
            ybf_s[:, cj] = (pbg_s[:, cj] * _silu(b_proj(3, cj))).astype(BF16)

        return [step_c, step_conv, step_b, step_gate]

    def b_out_steps(cj):
        def step_proj():
            pbg_s[:, cj] = jnp.dot(ybf_s[...], w_proj_b[:, cj], preferred_element_type=F32)

        def step_merge_gate():
            gate_b = _sigmoid(jnp.dot(hbf_s[...], w_gate[:, D_MODEL + cj.start:D_MODEL + cj.stop],
                                      preferred_element_type=F32))
            pbg_s[:, cj] = gate_b * pbg_s[:, cj]

        return [step_proj, step_merge_gate]

    b_tiles = [slice(j * B_TILE, (j + 1) * B_TILE) for j in range(WIDTH_B // B_TILE)]
    for cj in b_tiles:
        fill_queue += b_tile_steps(cj)
    for cj in b_tiles:
        fill_queue += b_out_steps(cj)

    pw = 2 * ch
    ri = lax.broadcasted_iota(jnp.int32, (ch, pw), 0)
    cn = lax.broadcasted_iota(jnp.int32, (ch, pw), 1)
    cj = cn & (ch - 1)
    hi = cn >= ch
    hi_row = lax.broadcasted_iota(jnp.int32, (1, pw), 1) >= ch
    causal = ri >= cj
    strict = ri > cj
    rt = lax.broadcasted_iota(jnp.int32, (ch, ch), 0)
    ct = lax.broadcasted_iota(jnp.int32, (ch, ch), 1)
    ltri = (rt >= ct).astype(F32)

    pairs = range(N_HEADS // 2)
    pcols = [slice(p * 2 * HEAD_D, (p + 1) * 2 * HEAD_D) for p in pairs]
    hcols = [slice(hd * HEAD_D, (hd + 1) * HEAD_D) for hd in range(N_HEADS)]

    def bd_pair(y):
        zero = jnp.zeros_like(y)
        return jnp.concatenate([jnp.where(hi, zero, y), jnp.where(hi, y, zero)], axis=0)

    def bd_wide(y):
        return _block_diag2(y[:, :HEAD_D], y[:, HEAD_D:])

    def mm_pairs(xs, ys):
        out = [jnp.dot(x.astype(BF16), bd_pair(y.astype(BF16)), preferred_element_type=F32)
               for x, y in zip(xs, ys)]
        fill()
        return out

    def lanes_of_pair(cols, p):
        return jnp.concatenate([jnp.broadcast_to(cols[:, 2 * p:2 * p + 1], (ch, HEAD_D)),
                                jnp.broadcast_to(cols[:, 2 * p + 1:2 * p + 2], (ch, HEAD_D))],
                               axis=1)

    def block_rows(blk):
        start = blk * ch
        return pl.ds(start if isinstance(start, int) else pl.multiple_of(start, ch), ch)

    per_seq = max(p1_chunks // nseg, 1) if nch_seg > 1 else 0

    def phase1(it):
        units = []
        pre = {}
        for j in range(p1_chunks):
            if nch_seg == 1:
                c = it * p1_chunks + j
            else:
                c = (j // per_seq) * nch_seg + it * per_seq + j % per_seq
            rows = block_rows(c)
            g_c = g_s[rows, :]
            beta_c = beta_s[rows, :]
            for p in pairs:
                units.append((j, rows, p))
                pre[j, p] = (k_s[rows, pcols[p]], q_s[rows, pcols[p]], v_s[rows, pcols[p]])
            gc = jnp.dot(ltri, g_c, preferred_element_type=F32,
                         precision=lax.Precision.HIGHEST)
            gc_rows = jnp.concatenate([gc, gc], axis=0).T
            glast = gc[ch - 1:ch, :]
            pre[j] = (c, gc, gc_rows, beta_c, jnp.exp(gc), jnp.exp(glast - gc), jnp.exp(glast))

        dmat, lhs, rk, rhs = [], [], [], []
        for j, rows, p in units:
            kp, qp, vp = pre[j, p]
            _, gc, gc_rows, beta_c, egc, _, _ = pre[j]
            g_col = jnp.where(hi, gc[:, 2 * p + 1:2 * p + 2], gc[:, 2 * p:2 * p + 1])
            g_row = jnp.where(hi_row, gc_rows[2 * p + 1:2 * p + 2, :], gc_rows[2 * p:2 * p + 1, :])
            dmat.append(jnp.where(causal, jnp.exp(jnp.where(causal, g_col - g_row, 0.0)), 0.0))
            kb = kp * lanes_of_pair(beta_c, p)
            lhs.append(jnp.concatenate([kb, qp], axis=0).astype(BF16))
            rk.append(bd_wide(kp.astype(BF16)))
            rhs.append((vp * lanes_of_pair(beta_c, p), kb * lanes_of_pair(egc, p)))
        kkqk = [lax.dot_general(a, b, (((1,), (1,)), ((), ())), preferred_element_type=F32)
                for a, b in zip(lhs, rk)]
        fill()
        a_mat = [jnp.where(strict, x[:ch] * d, 0.0) for x, d in zip(kkqk, dmat)]
        t_x = _inv_unit_lower(a_mat, ri, cj, ch, mm_pairs)
        duw = [jnp.dot(t.astype(BF16),
                       jnp.concatenate([bd_wide(ru.astype(BF16)), bd_wide(rw.astype(BF16))], axis=1),
                       preferred_element_type=F32)
               for t, (ru, rw) in zip(t_x, rhs)]

        fill()
        for i, (j, rows, p) in enumerate(units):
            kp, qp = k_s[rows, pcols[p]], q_s[rows, pcols[p]]
            _, _, _, _, egc, ekd, _ = pre[j]
            qk_s[p, rows, :] = (kkqk[i][ch:] * dmat[i]).astype(BF16)
            u_s[rows, pcols[p]] = rhs[i][0] + duw[i][:, :2 * HEAD_D]
            w_s[rows, pcols[p]] = (rhs[i][1] + duw[i][:, 2 * HEAD_D:]).astype(BF16)
            qd_s[rows, pcols[p]] = (qp * lanes_of_pair(egc, p)).astype(BF16)
            kd_s[rows, pcols[p]] = kp * lanes_of_pair(ekd, p)
        for j in range(p1_chunks):
            egl_s[pl.ds(pre[j][0], 1), :] = pre[j][6]


    ngroups = nseg // P2_SEGS
    assert ngroups == 1 or nch_seg == 1

    def phase2(it):
        c, sg = (it, 0) if ngroups == 1 else (0, it)
        units = []
        egl = {}
        for i in range(P2_SEGS):
            sq = sg * P2_SEGS + i
            blk = sq * nch_seg + c
            rows = block_rows(blk)
            egl[i] = egl_s[pl.ds(blk, 1), :]
            units += [(i, sq, rows, p) for p in pairs]
        s_old = {(i, hd): s_ref[sq, hd] for i, sq, _, p in units for hd in (2 * p, 2 * p + 1)}
        wq = [jnp.concatenate([w_s[rows, pcols[p]], qd_s[rows, pcols[p]]], axis=0)
              for _, _, rows, p in units]
        u = [u_s[rows, pcols[p]] for _, _, rows, p in units]
        qk = [qk_s[p, rows, :] for _, _, rows, p in units]
        kd_t = {(i, hd): kd_s[rows, hcols[hd]].T.astype(BF16)
                for i, _, rows, p in units for hd in (2 * p, 2 * p + 1)}
        s_bd = [_block_diag2(s_old[i, 2 * p].astype(BF16), s_old[i, 2 * p + 1].astype(BF16))
                for i, _, _, p in units]
        ws = [jnp.dot(a, sb, preferred_element_type=F32) for a, sb in zip(wq, s_bd)]
        v_new = [(a - b[:ch]).astype(BF16) for a, b in zip(u, ws)]
        o = [b[ch:] + jnp.dot(a, bd_wide(v), preferred_element_type=F32)
             for a, b, v in zip(qk, ws, v_new)]
        s_new = {}
        for n, (i, _, _, p) in enumerate(units):
            for half, hd in enumerate((2 * p, 2 * p + 1)):
                s_new[i, hd] = s_old[i, hd] * egl[i][:, hd:hd + 1] + jnp.dot(
                    kd_t[i, hd], v_new[n][:, half * HEAD_D:(half + 1) * HEAD_D],
                    preferred_element_type=F32)
        for n, (i, sq, rows, p) in enumerate(units):
            o_s[rows, pcols[p]] = o[n]
            s_ref[sq, 2 * p] = s_new[i, 2 * p]
            s_ref[sq, 2 * p + 1] = s_new[i, 2 * p + 1]

    n_p1, n_p2 = nch // p1_chunks, nch_seg * ngroups
    phase1(0)
    done2 = 0
    for it1 in range(1, n_p1 + 1):
        if it1 < n_p1:
            phase1(it1)
        ready2 = n_p2 * it1 // n_p1
        for it2 in range(done2, ready2):
            phase2(it2)
        done2 = ready2
    while fill_queue:
        fill()

    def stage_gate_proj(hf):
        lhs = hbf_s[hf["rs"], :]
        hf["za"] = jnp.dot(lhs, w_za[...], preferred_element_type=F32)
        hf["ga"] = jnp.dot(lhs, w_gate[:, :D_MODEL], preferred_element_type=F32)

    def stage_gated_norm(hf):
        ya = []
        for hd in range(N_HEADS):
            hc = slice(hd * HEAD_D, (hd + 1) * HEAD_D)
            oh = o_s[hf["rs"], hc]
            ms = jnp.mean(oh * oh, axis=-1, keepdims=True)
            ya.append((oh * lax.rsqrt(ms + RMS_EPS) * norm_a_g[...]
                       * _silu(hf["za"][:, hc])).astype(BF16))
        hf["ya"] = jnp.concatenate(ya, axis=1)

    def stage_merge(hf):
        pa = jnp.dot(hf.pop("ya"), w_proj_a[...], preferred_element_type=F32)
        hf["merged"] = (_sigmoid(hf.pop("ga")) * pa + pbg_s[hf["rs"], :]).astype(BF16)

    def stage_out_proj(hf):
        hf["mo"] = jnp.dot(hf.pop("merged"), w_out[...], preferred_element_type=F32)

    def stage_norm1(hf):
        h_in = y_ref[hf["sq"]].reshape(tt // 2, D_MODEL)
        hf["h1"] = _layer_norm(ALPHA * h_in + hf.pop("mo"), ln1_g[...], ln1_b[...])

    def stage_ple_proj(hf):
        hf["pg"] = jnp.dot(hf["h1"].astype(BF16), w_ple_gate[...], preferred_element_type=F32)
        hf["pp"] = jnp.dot(p_ref[0, hf["sq"]].reshape(tt // 2, P_DIM).astype(BF16), w_ple[...],
                           preferred_element_type=F32)

    def stage_norm2(hf):
        ple = _sigmoid(hf.pop("pg")) * hf.pop("pp")
        y_ref[hf["sq"]] = _layer_norm(ALPHA * hf.pop("h1") + ple, ln2_g[...], ln2_b[...]).reshape(
            nseg // 2, seg, D_MODEL)

    for stage in (stage_gate_proj, stage_gated_norm, stage_merge, stage_out_proj, stage_norm1,
                  stage_ple_proj, stage_norm2):
        for hf in halves:
            stage(hf)


def _encode(x, p, init_state, weights, *, nseg, seg, ch):
    nseq, t_len, _ = x.shape
    zero_init = init_state is None
    state_shapes = ((nseq, SUBLANES, QKV_DIM), (nseq, N_HEADS, HEAD_D, HEAD_D),
                    (nseq, SUBLANES, WIDTH_B))
    nb, nt = nseq // nseg, t_len // seg
    carry = nt > 1
    assert nseq % nseg == 0 and t_len % seg == 0 and seg % ch == 0 and seg % SUBLANES == 0
    assert ch & (ch - 1) == 0 and ch % INV_BLOCK == 0 and nseg % P2_SEGS == 0
    tt = nseg * seg
    nch = tt // ch
    cfg = (nseg, seg, ch, carry, zero_init)

    def tile_spec(width):
        return pl.BlockSpec((nseg, seg, width), lambda b, t: (b, t, 0))

    def state_spec(shape):
        nd = len(shape)
        return pl.BlockSpec((nseg,) + tuple(shape[1:]), lambda b, t: (b,) + (0,) * (nd - 1),
                            pipeline_mode=pl.Buffered(1))

    def const_spec(arr):
        nd = arr.ndim
        return pl.BlockSpec(arr.shape, lambda b, t: (0,) * nd, pipeline_mode=pl.Buffered(1))

    def cols_spec(arr, width, offset):
        assert offset % width == 0
        return pl.BlockSpec((arr.shape[0], width), lambda b, t: (0, offset // width),
                            pipeline_mode=pl.Buffered(1))

    w_in_packed, others = weights
    p_spec = pl.BlockSpec((1, nseg, seg, P_DIM), lambda b, t: (0, b, t, 0))
    in_specs = [tile_spec(D_MODEL), p_spec]
    operands = [x, p]
    if not zero_init:
        assert tuple(a.shape for a in init_state) == state_shapes
        in_specs += [state_spec(shape) for shape in state_shapes]
        operands += list(init_state)
    for name, arr in others:
        if name == "w_in":
            for width, offset in PACKED_COLS:
                in_specs.append(cols_spec(w_in_packed, width, offset))
                operands.append(w_in_packed)
        else:
            in_specs.append(const_spec(arr))
            operands.append(arr)
    out_shape = (jax.ShapeDtypeStruct(x.shape, F32),) + tuple(
        jax.ShapeDtypeStruct(shape, F32) for shape in state_shapes)
    out_specs = (tile_spec(D_MODEL),) + tuple(state_spec(shape) for shape in state_shapes)
    scratch = [
        pltpu.VMEM((tt, D_MODEL), BF16),
        pltpu.VMEM((tt, KEY_DIM), F32),
        pltpu.VMEM((tt, KEY_DIM), F32),
        pltpu.VMEM((tt, KEY_DIM), F32),
        pltpu.VMEM((tt, LANES), F32),
        pltpu.VMEM((tt, LANES), F32),
        pltpu.VMEM((max(nch, SUBLANES), LANES), F32),
        pltpu.VMEM((tt, KEY_DIM), BF16),
        pltpu.VMEM((tt, KEY_DIM), BF16),
        pltpu.VMEM((N_HEADS // 2, tt, 2 * ch), BF16),
        pltpu.VMEM((SUBLANES + seg, max(QKV_CHUNK, B_TILE)), F32),
        pltpu.VMEM((tt, D_MODEL), F32),
        pltpu.VMEM((tt, WIDTH_B), BF16),
    ]
    return pl.pallas_call(
        functools.partial(_layer_kernel, cfg),
        grid=(nb, nt),
        in_specs=in_specs,
        out_specs=out_specs,
        out_shape=out_shape,
        scratch_shapes=scratch,
        compiler_params=pltpu.CompilerParams(
            dimension_semantics=("arbitrary", "arbitrary"),
            vmem_limit_bytes=VMEM_LIMIT_BYTES),
        name=f"gdn_shortconv_layer_n{nseg}_t{seg}_c{ch}",
    )(*operands)


def _pack_kernel(a_ref, b_ref, o_ref):
    j = pl.program_id(0)
    n_plain = OFF_BETA // PACK_W
    n_shift = (IN_DIM - OFF_BB) // PACK_W
    shift = OFF_BB - OFF_BETA
    row = lax.broadcasted_iota(jnp.int32, (LANES, 1), 0)

    @pl.when(j < n_plain)
    def _():
        o_ref[...] = a_ref[...].T.astype(BF16)

    @pl.when((j >= n_plain) & (j < n_plain + n_shift))
    def _():
        o_ref[...] = jnp.concatenate([a_ref[shift:, :], b_ref[...]], axis=0).T.astype(BF16)

    @pl.when(j == n_plain + n_shift)
    def _():
        beta = jnp.where(row < N_HEADS, a_ref[0:LANES, :], 0.0)
        decay = jnp.where(row < N_HEADS, a_ref[N_HEADS:N_HEADS + LANES, :], 0.0)
        o_ref[:, :LANES] = beta.T.astype(BF16)
        o_ref[:, LANES:2 * LANES] = decay.T.astype(BF16)
        o_ref[:, 2 * LANES:] = jnp.zeros((o_ref.shape[0], PACK_W - 2 * LANES), BF16)


def _pack_w_in(w_t):
    n_plain = OFF_BETA // PACK_W
    n_shift = (IN_DIM - OFF_BB) // PACK_W
    shift = OFF_BB - OFF_BETA
    assert OFF_BETA % PACK_W == 0 and (IN_DIM - OFF_BB) % PACK_W == 0
    assert OFF_DECAY - OFF_BETA == N_HEADS == SUBLANES and shift % SUBLANES == 0

    def a_idx(j):
        return (0, jnp.where(j < n_plain + n_shift, j, n_plain), 0)

    def b_idx(j):
        return (0, jnp.where((j >= n_plain) & (j < n_plain + n_shift), (j + 1) * (PACK_W // shift), 0), 0)

    return pl.pallas_call(
        _pack_kernel,
        grid=(n_plain + n_shift + 1,),
        in_specs=[pl.BlockSpec((None, PACK_W, D_MODEL), a_idx),
                  pl.BlockSpec((None, shift, D_MODEL), b_idx)],
        out_specs=pl.BlockSpec((D_MODEL, PACK_W), lambda j: (0, j)),
        out_shape=jax.ShapeDtypeStruct((D_MODEL, PACKED_DIM), BF16),
        compiler_params=pltpu.CompilerParams(dimension_semantics=("arbitrary",)),
        name="pack_w_in",
    )(w_t, w_t)


def _pad_rows_front(a, rows):
    pad = [(0, 0)] * a.ndim
    pad[-2] = (rows - a.shape[-2], 0)
    return jnp.pad(a, pad)


def _pad_lanes(a):
    pad = [(0, 0)] * a.ndim
    pad[-1] = (0, LANES - a.shape[-1])
    return jnp.pad(a, pad)


def kernel(x_prompt, x_sample, state_conv_a, state_gdn, state_conv_b, p_prompt, p_sample, ln_in_g, ln_in_b, w_in, w_conv_a, a_log, dt_bias, norm_a_g, w_conv_b, w_proj_a, w_proj_b, w_out, ln1_g, ln1_b, w_ple, w_ple_gate, ln2_g, ln2_b):
    assert w_in.shape[0] == DEPTH == 1
    row = lambda v: v.reshape(1, -1).astype(F32)
    w_in_packed = _pack_w_in(jnp.swapaxes(w_in, 1, 2))
    others = (
        ("ln_in_g", row(ln_in_g)), ("ln_in_b", row(ln_in_b)),
        ("w_in", None),
        ("w_conv_a", w_conv_a[0].astype(F32)),
        ("a_log", _pad_lanes(row(a_log[0]))), ("dt_bias", _pad_lanes(row(dt_bias[0]))),
        ("norm_a_g", row(norm_a_g[0])),
        ("w_conv_b", w_conv_b[0].astype(F32)),
        ("w_proj_a", w_proj_a[0].astype(BF16)), ("w_proj_b", w_proj_b[0].astype(BF16)),
        ("w_out", w_out[0].astype(BF16)),
        ("ln1_g", row(ln1_g[0])), ("ln1_b", row(ln1_b[0])),
        ("w_ple", w_ple[0].astype(BF16)), ("w_ple_gate", w_ple_gate[0].astype(BF16)),
        ("ln2_g", row(ln2_g[0])), ("ln2_b", row(ln2_b[0])),
    )
    weights = (w_in_packed, others)

    bp, seq, _ = x_prompt.shape
    y_p, ca_p, s_p, cb_p = _encode(
        x_prompt, p_prompt, None, weights, nseg=bp, seg=PROMPT_TILE, ch=PROMPT_CHUNK)

    bs, ts, _ = x_sample.shape
    y_s, ca_s, s_s, cb_s = _encode(
        x_sample, p_sample,
        (_pad_rows_front(state_conv_a[0], SUBLANES), state_gdn[0].astype(F32),
         _pad_rows_front(state_conv_b[0], SUBLANES)),
        weights, nseg=SAMPLE_TILE_SEQS, seg=ts, ch=ts)

    na, nb = CONV_A - 1, CONV_B - 1
    return (y_p, y_s,
            ca_p[None, :, SUBLANES - na:], s_p[None], cb_p[None, :, SUBLANES - nb:],
            ca_s[None, :, SUBLANES - na:], s_s[None].astype(state_gdn.dtype),
            cb_s[None, :, SUBLANES - nb:])
```

```python
import functools

import jax
import jax.numpy as jnp
from jax import lax
from jax.experimental import pallas as pl
from jax.experimental.pallas import tpu as pltpu

D_MODEL = 1024
N_HEADS = 8
HEAD_D = 128
KEY_DIM = N_HEADS * HEAD_D
QKV_DIM = 3 * KEY_DIM
WIDTH_B = D_MODEL
P_DIM = 256
CONV_A = 4
CONV_B = 3
PROMPT_CHUNK = 64
DEPTH = 1
ALPHA = (2 * DEPTH) ** 0.25
LN_EPS = 1e-5
RMS_EPS = 1e-6
L2_EPS = 1e-6

OFF_ZA = QKV_DIM
OFF_BETA = OFF_ZA + KEY_DIM
OFF_DECAY = OFF_BETA + N_HEADS
OFF_BB = OFF_DECAY + N_HEADS
OFF_CB = OFF_BB + WIDTH_B
OFF_UB = OFF_CB + WIDTH_B
OFF_ZB = OFF_UB + WIDTH_B
OFF_GATE = OFF_ZB + WIDTH_B
IN_DIM = OFF_GATE + 2 * D_MODEL

SUBLANES = 8
LANES = 128
INV_BLOCK = 4
PROMPT_TILE = 256
SAMPLE_TILE_SEQS = 8
P1_CHUNKS = 4
QKV_CHUNK = 256
ANCHOR_LAG = 2
ANCHOR_ROWS = 16
P2_SEGS = 2
VMEM_LIMIT_BYTES = 127 * 512 * 1024

PACK_W = 512
PACKED_DIM = IN_DIM - (OFF_BB - OFF_BETA) + 2 * LANES
PACKED_COLS = ((QKV_DIM, 0), (KEY_DIM, OFF_ZA), (LANES, OFF_BETA + 4 * WIDTH_B + 2 * D_MODEL),
               (LANES, OFF_BETA + 4 * WIDTH_B + 2 * D_MODEL + LANES), (4 * WIDTH_B, OFF_BETA),
               (2 * D_MODEL, OFF_BETA + 4 * WIDTH_B))

F32 = jnp.float32
NEG_LOG2_E = -1.4426950408889634
BF16 = jnp.bfloat16


def _sigmoid(x):
    return 1.0 / (1.0 + jnp.exp2(x * NEG_LOG2_E))


def _silu(x):
    return x * _sigmoid(x)


def _softplus(x):
    return jnp.maximum(x, 0.0) + jnp.log(1.0 + jnp.exp(-jnp.abs(x)))


def _layer_norm(x, g, b):
    mu = jnp.mean(x, axis=-1, keepdims=True)
    xc = x - mu
    var = jnp.mean(xc * xc, axis=-1, keepdims=True)
    return xc * lax.rsqrt(var + LN_EPS) * g + b


def _causal_conv(seg, hist8, w_ref, ntaps, buf):
    n = seg.shape[0]
    buf[0:SUBLANES, :] = hist8
    buf[SUBLANES:SUBLANES + n, :] = seg
    acc = seg * w_ref[ntaps - 1:ntaps, :]
    for s in range(1, ntaps):
        acc = acc + buf[SUBLANES - s:SUBLANES - s + n, :] * w_ref[ntaps - 1 - s:ntaps - s, :]
    return acc


def _block_diag2(y1, y2):
    z = jnp.zeros_like(y1)
    return jnp.concatenate([jnp.concatenate([y1, z], axis=1), jnp.concatenate([z, y2], axis=1)],
                           axis=0)


def _inv_unit_lower(a_list, row, col, size, mm_each):
    base = (row // INV_BLOCK) == (col // INV_BLOCK)
    d = [jnp.where(base, a, 0.0) for a in a_list]
    d2 = mm_each(d, d)
    dd2 = mm_each(d, d2)
    x = [b - a - c for a, b, c in zip(d, d2, dd2)]
    s = INV_BLOCK
    while s < size:
        pair = ((row // (2 * s)) == (col // (2 * s))) & ((row // s) != (col // s))
        a21 = [jnp.where(pair, a, 0.0) for a in a_list]
        m1 = [a + b for a, b in zip(a21, mm_each(a21, x))]
        m2 = [a + b for a, b in zip(m1, mm_each(x, m1))]
        x = [a - b for a, b in zip(x, m2)]
        s *= 2
    return x


def _layer_kernel(cfg, x_ref, p_ref, *refs):
    init_refs, refs = (None, refs) if cfg[4] else (refs[:3], refs[3:])
    _layer_body(cfg, x_ref, p_ref, init_refs, *refs)


def _layer_body(cfg,
                  x_ref, p_ref, init_refs,
                  ln_in_g, ln_in_b, w_qkv, w_za, w_beta, w_dec, w_b, w_gate,
                  w_conv_a, a_log, dt_bias, norm_a_g, w_conv_b,
                  w_proj_a, w_proj_b, w_out, ln1_g, ln1_b, w_ple, w_ple_gate, ln2_g, ln2_b, never_ref,
                  y_ref, ca_ref, s_ref, cb_ref,
                  hbf_s, q_s, k_s, v_s, beta_s, g_s, egl_s, w_s, qd_s, qk_s, cbuf_s, pbg_s, ybf_s):
    u_s, kd_s, o_s = v_s, k_s, q_s
    nseg, seg, ch, carry, zero_init = cfg
    tt = nseg * seg
    nch = tt // ch
    nch_seg = seg // ch
    p1_chunks = min(P1_CHUNKS, nch)
    t_idx = pl.program_id(1)

    def _seed():
        if zero_init:
            ca_ref[...] = jnp.zeros(ca_ref.shape, F32)
            cb_ref[...] = jnp.zeros(cb_ref.shape, F32)
            s_ref[...] = jnp.zeros(s_ref.shape, F32)
        else:
            hista_ref, sin_ref, histb_ref = init_refs
            ca_ref[...] = hista_ref[...]
            cb_ref[...] = histb_ref[...]
            s_ref[...] = sin_ref[...]

    if carry:
        pl.when(t_idx == 0)(_seed)
    else:
        _seed()

    def conv_tile(pre, hist_ref, col, w_ref, ntaps, seqs=None):
        seqs = range(nseg) if seqs is None else seqs
        outs = []
        for n, i in enumerate(seqs):
            part = pre[n * seg:(n + 1) * seg]
            outs.append(_causal_conv(part, hist_ref[i, :, col], w_ref.at[:, col], ntaps,
                                     cbuf_s))
            hist_ref[i, :, col] = part[seg - SUBLANES:]
        return outs[0] if len(outs) == 1 else jnp.concatenate(outs, axis=0)

    halves = []
    for k in range(2):
        sq = slice(k * nseg // 2, (k + 1) * nseg // 2)
        halves.append({"rs": slice(k * tt // 2, (k + 1) * tt // 2), "sq": sq,
                       "seqs": range(sq.start, sq.stop)})

    for hf in halves:
        h = _layer_norm(x_ref[hf["sq"]].reshape(tt // 2, D_MODEL), ln_in_g[...], ln_in_b[...])
        y_ref[hf["sq"]] = h.reshape(nseg // 2, seg, D_MODEL)
        hbf_s[hf["rs"], :] = h.astype(BF16)

    never = jnp.concatenate([never_ref[...]] * (ANCHOR_ROWS // SUBLANES), axis=0)
    never = jnp.concatenate([never] * (D_MODEL // LANES), axis=1) != 0

    def anchored_lhs(hf, j):
        lhs = hbf_s[hf["rs"], :]
        dep = hf.get("deps", {}).pop(j - ANCHOR_LAG, None)
        if dep is None:
            return lhs
        top = jnp.where(never, dep, lhs[:ANCHOR_ROWS].astype(F32)).astype(BF16)
        return jnp.concatenate([top, lhs[ANCHOR_ROWS:]], axis=0)

    for j in range(QKV_DIM // QKV_CHUNK):
        col = slice(j * QKV_CHUNK, (j + 1) * QKV_CHUNK)
        grp, sub = divmod(j, KEY_DIM // QKV_CHUNK)
        dst = (q_s, k_s, v_s)[grp]
        for hf in halves:
            hf["pre"] = jnp.dot(anchored_lhs(hf, j), w_qkv[:, col], preferred_element_type=F32)
        for hf in halves:
            act = _silu(conv_tile(hf.pop("pre"), ca_ref, col, w_conv_a, CONV_A, hf["seqs"]))
            dsum = act.reshape(-1, SUBLANES, QKV_CHUNK).sum(axis=0)
            dsum = jnp.concatenate([dsum] * (ANCHOR_ROWS // SUBLANES), axis=0)
            hf.setdefault("deps", {})[j] = jnp.concatenate([dsum] * (D_MODEL // QKV_CHUNK), axis=1)
            for hh in range(QKV_CHUNK // HEAD_D):
                hc = slice(sub * QKV_CHUNK + hh * HEAD_D, sub * QKV_CHUNK + (hh + 1) * HEAD_D)
                xh = act[:, hh * HEAD_D:(hh + 1) * HEAD_D]
                if grp == 2:
                    dst[hf["rs"], hc] = xh
                else:
                    scale = HEAD_D ** -0.5 if grp == 0 else 1.0
                    ss = jnp.sum(xh * xh, axis=-1, keepdims=True)
                    dst[hf["rs"], hc] = xh * (lax.rsqrt(ss + L2_EPS) * scale)

    beta_s[...] = _sigmoid(jnp.dot(hbf_s[...], w_beta[...], preferred_element_type=F32))
    zdec = jnp.dot(hbf_s[...], w_dec[...], preferred_element_type=F32)
    g_s[...] = -jnp.exp(a_log[...]) * _softplus(zdec + dt_bias[...])

    fill_queue = []
    held = {}

    def fill():
        if fill_queue:
            fill_queue.pop(0)()

    def b_proj(k, cj):
        return jnp.dot(hbf_s[...], w_b[:, k * WIDTH_B + cj.start:k * WIDTH_B + cj.stop],
                       preferred_element_type=F32)

    def b_tile_steps(cj):
        def step_c():
            held["cb"] = b_proj(1, cj)

        def step_conv():
            cu = held.pop("cb") * b_proj(2, cj)
            pbg_s[:, cj] = conv_tile(cu, cb_ref, cj, w_conv_b, CONV_B)

        def step_b():
            pbg_s[:, cj] = b_proj(0, cj) * pbg_s[:, cj]

        def step_gate():
            ybf_s[:, cj] = (pbg_s[:, cj] * _silu(b_proj(3, cj))).astype(BF16)

        return [step_c, step_conv, step_b, step_gate]

    def b_out_steps(cj):
        def step_proj():
            pbg_s[:, cj] = jnp.dot(ybf_s[...], w_proj_b[:, cj], preferred_element_type=F32)

        def step_merge_gate():
            gate_b = _sigmoid(jnp.dot(hbf_s[...], w_gate[:, D_MODEL + cj.start:D_MODEL + cj.stop],
                                      preferred_element_type=F32))
            pbg_s[:, cj] = gate_b * pbg_s[:, cj]

        return [step_proj, step_merge_gate]

    b_tiles = [slice(j * QKV_CHUNK, (j + 1) * QKV_CHUNK) for j in range(WIDTH_B // QKV_CHUNK)]
    for cj in b_tiles:
        fill_queue += b_tile_steps(cj)
    for cj in b_tiles:
        fill_queue += b_out_steps(cj)

    pw = 2 * ch
    ri = lax.broadcasted_iota(jnp.int32, (ch, pw), 0)
    cn = lax.broadcasted_iota(jnp.int32, (ch, pw), 1)
    cj = cn & (ch - 1)
    hi = cn >= ch
    hi_row = lax.broadcasted_iota(jnp.int32, (1, pw), 1) >= ch
    causal = ri >= cj
    strict = ri > cj
    rt = lax.broadcasted_iota(jnp.int32, (ch, ch), 0)
    ct = lax.broadcasted_iota(jnp.int32, (ch, ch), 1)
    ltri = (rt >= ct).astype(F32)

    pairs = range(N_HEADS // 2)
    pcols = [slice(p * 2 * HEAD_D, (p + 1) * 2 * HEAD_D) for p in pairs]
    hcols = [slice(hd * HEAD_D, (hd + 1) * HEAD_D) for hd in range(N_HEADS)]

    def bd_pair(y):
        zero = jnp.zeros_like(y)
        return jnp.concatenate([jnp.where(hi, zero, y), jnp.where(hi, y, zero)], axis=0)

    def bd_wide(y):
        return _block_diag2(y[:, :HEAD_D], y[:, HEAD_D:])

    def mm_pairs(xs, ys):
        out = [jnp.dot(x.astype(BF16), bd_pair(y.astype(BF16)), preferred_element_type=F32)
               for x, y in zip(xs, ys)]
        fill()
        return out

    def lanes_of_pair(cols, p):
        return jnp.concatenate([jnp.broadcast_to(cols[:, 2 * p:2 * p + 1], (ch, HEAD_D)),
                                jnp.broadcast_to(cols[:, 2 * p + 1:2 * p + 2], (ch, HEAD_D))],
                               axis=1)

    def block_rows(blk):
        start = blk * ch
        return pl.ds(start if isinstance(start, int) else pl.multiple_of(start, ch), ch)

    per_seq = max(p1_chunks // nseg, 1) if nch_seg > 1 else 0

    def phase1(it):
        units = []
        pre = {}
        for j in range(p1_chunks):
            if nch_seg == 1:
                c = it * p1_chunks + j
            else:
                c = (j // per_seq) * nch_seg + it * per_seq + j % per_seq
            rows = block_rows(c)
            g_c = g_s[rows, :]
            beta_c = beta_s[rows, :]
            for p in pairs:
                units.append((j, rows, p))
                pre[j, p] = (k_s[rows, pcols[p]], q_s[rows, pcols[p]], v_s[rows, pcols[p]])
            gc = jnp.dot(ltri, g_c, preferred_element_type=F32,
                         precision=lax.Precision.HIGHEST)
            gc_rows = jnp.concatenate([gc, gc], axis=0).T
            glast = gc[ch - 1:ch, :]
            pre[j] = (c, gc, gc_rows, beta_c, jnp.exp(gc), jnp.exp(glast - gc), jnp.exp(glast))

        dmat, lhs, rk, rhs = [], [], [], []
        for j, rows, p in units:
            kp, qp, vp = pre[j, p]
            _, gc, gc_rows, beta_c, egc, _, _ = pre[j]
            g_col = jnp.where(hi, gc[:, 2 * p + 1:2 * p + 2], gc[:, 2 * p:2 * p + 1])
            g_row = jnp.where(hi_row, gc_rows[2 * p + 1:2 * p + 2, :], gc_rows[2 * p:2 * p + 1, :])
            dmat.append(jnp.where(causal, jnp.exp(jnp.where(causal, g_col - g_row, 0.0)), 0.0))
            kb = kp * lanes_of_pair(beta_c, p)
            lhs.append(jnp.concatenate([kb, qp], axis=0).astype(BF16))
            rk.append(bd_wide(kp.astype(BF16)))
            rhs.append((vp * lanes_of_pair(beta_c, p), kb * lanes_of_pair(egc, p)))
        kkqk = [lax.dot_general(a, b, (((1,), (1,)), ((), ())), preferred_element_type=F32)
                for a, b in zip(lhs, rk)]
        fill()
        a_mat = [jnp.where(strict, x[:ch] * d, 0.0) for x, d in zip(kkqk, dmat)]
        t_x = _inv_unit_lower(a_mat, ri, cj, ch, mm_pairs)
        duw = [jnp.dot(t.astype(BF16),
                       jnp.concatenate([bd_wide(ru.astype(BF16)), bd_wide(rw.astype(BF16))], axis=1),
                       preferred_element_type=F32)
               for t, (ru, rw) in zip(t_x, rhs)]

        fill()
        for i, (j, rows, p) in enumerate(units):
            kp, qp = k_s[rows, pcols[p]], q_s[rows, pcols[p]]
            _, _, _, _, egc, ekd, _ = pre[j]
            qk_s[p, rows, :] = (kkqk[i][ch:] * dmat[i]).astype(BF16)
            u_s[rows, pcols[p]] = rhs[i][0] + duw[i][:, :2 * HEAD_D]
            w_s[rows, pcols[p]] = (rhs[i][1] + duw[i][:, 2 * HEAD_D:]).astype(BF16)
            qd_s[rows, pcols[p]] = (qp * lanes_of_pair(egc, p)).astype(BF16)
            kd_s[rows, pcols[p]] = kp * lanes_of_pair(ekd, p)
        for j in range(p1_chunks):
            egl_s[pl.ds(pre[j][0], 1), :] = pre[j][6]


    ngroups = nseg // P2_SEGS
    assert ngroups == 1 or nch_seg == 1

    def phase2(it):
        c, sg = (it, 0) if ngroups == 1 else (0, it)
        units = []
        egl = {}
        for i in range(P2_SEGS):
            sq = sg * P2_SEGS + i
            blk = sq * nch_seg + c
            rows = block_rows(blk)
            egl[i] = egl_s[pl.ds(blk, 1), :]
            units += [(i, sq, rows, p) for p in pairs]
        s_old = {(i, hd): s_ref[sq, hd] for i, sq, _, p in units for hd in (2 * p, 2 * p + 1)}
        wq = [jnp.concatenate([w_s[rows, pcols[p]], qd_s[rows, pcols[p]]], axis=0)
              for _, _, rows, p in units]
        u = [u_s[rows, pcols[p]] for _, _, rows, p in units]
        qk = [qk_s[p, rows, :] for _, _, rows, p in units]
        kd_t = {(i, hd): kd_s[rows, hcols[hd]].T.astype(BF16)
                for i, _, rows, p in units for hd in (2 * p, 2 * p + 1)}
        s_bd = [_block_diag2(s_old[i, 2 * p].astype(BF16), s_old[i, 2 * p + 1].astype(BF16))
                for i, _, _, p in units]
        ws = [jnp.dot(a, sb, preferred_element_type=F32) for a, sb in zip(wq, s_bd)]
        v_new = [(a - b[:ch]).astype(BF16) for a, b in zip(u, ws)]
        o = [b[ch:] + jnp.dot(a, bd_wide(v), preferred_element_type=F32)
             for a, b, v in zip(qk, ws, v_new)]
        s_new = {}
        for n, (i, _, _, p) in enumerate(units):
            for half, hd in enumerate((2 * p, 2 * p + 1)):
                s_new[i, hd] = s_old[i, hd] * egl[i][:, hd:hd + 1] + jnp.dot(
                    kd_t[i, hd], v_new[n][:, half * HEAD_D:(half + 1) * HEAD_D],
                    preferred_element_type=F32)
        for n, (i, sq, rows, p) in enumerate(units):
            o_s[rows, pcols[p]] = o[n]
            s_ref[sq, 2 * p] = s_new[i, 2 * p]
            s_ref[sq, 2 * p + 1] = s_new[i, 2 * p + 1]

    n_p1, n_p2 = nch // p1_chunks, nch_seg * ngroups
    phase1(0)
    done2 = 0
    for it1 in range(1, n_p1 + 1):
        if it1 < n_p1:
            phase1(it1)
        ready2 = n_p2 * it1 // n_p1
        for it2 in range(done2, ready2):
            phase2(it2)
        done2 = ready2
    while fill_queue:
        fill()

    def stage_gate_proj(hf):
        lhs = hbf_s[hf["rs"], :]
        hf["za"] = jnp.dot(lhs, w_za[...], preferred_element_type=F32)
        hf["ga"] = jnp.dot(lhs, w_gate[:, :D_MODEL], preferred_element_type=F32)

    def stage_gated_norm(hf):
        ya = []
        for hd in range(N_HEADS):
            hc = slice(hd * HEAD_D, (hd + 1) * HEAD_D)
            oh = o_s[hf["rs"], hc]
            ms = jnp.mean(oh * oh, axis=-1, keepdims=True)
            ya.append((oh * lax.rsqrt(ms + RMS_EPS) * norm_a_g[...]
                       * _silu(hf["za"][:, hc])).astype(BF16))
        hf["ya"] = jnp.concatenate(ya, axis=1)

    def stage_merge(hf):
        pa = jnp.dot(hf.pop("ya"), w_proj_a[...], preferred_element_type=F32)
        hf["merged"] = (_sigmoid(hf.pop("ga")) * pa + pbg_s[hf["rs"], :]).astype(BF16)

    def stage_out_proj(hf):
        hf["mo"] = jnp.dot(hf.pop("merged"), w_out[...], preferred_element_type=F32)

    def stage_norm1(hf):
        h_in = y_ref[hf["sq"]].reshape(tt // 2, D_MODEL)
        hf["h1"] = _layer_norm(ALPHA * h_in + hf.pop("mo"), ln1_g[...], ln1_b[...])

    def stage_ple_proj(hf):
        hf["pg"] = jnp.dot(hf["h1"].astype(BF16), w_ple_gate[...], preferred_element_type=F32)
        hf["pp"] = jnp.dot(p_ref[0, hf["sq"]].reshape(tt // 2, P_DIM).astype(BF16), w_ple[...],
                           preferred_element_type=F32)

    def stage_norm2(hf):
        ple = _sigmoid(hf.pop("pg")) * hf.pop("pp")
        y_ref[hf["sq"]] = _layer_norm(ALPHA * hf.pop("h1") + ple, ln2_g[...], ln2_b[...]).reshape(
            nseg // 2, seg, D_MODEL)

    for stage in (stage_gate_proj, stage_gated_norm, stage_merge, stage_out_proj, stage_norm1,
                  stage_ple_proj, stage_norm2):
        for hf in halves:
            stage(hf)


def _encode(x, p, init_state, weights, *, nseg, seg, ch):
    nseq, t_len, _ = x.shape
    zero_init = init_state is None
    state_shapes = ((nseq, SUBLANES, QKV_DIM), (nseq, N_HEADS, HEAD_D, HEAD_D),
                    (nseq, SUBLANES, WIDTH_B))
    nb, nt = nseq // nseg, t_len // seg
    carry = nt > 1
    assert nseq % nseg == 0 and t_len % seg == 0 and seg % ch == 0 and seg % SUBLANES == 0
    assert ch & (ch - 1) == 0 and ch % INV_BLOCK == 0 and nseg % P2_SEGS == 0
    tt = nseg * seg
    nch = tt // ch
    cfg = (nseg, seg, ch, carry, zero_init)

    def tile_spec(width):
        return pl.BlockSpec((nseg, seg, width), lambda b, t: (b, t, 0))

    def state_spec(shape):
        nd = len(shape)
        return pl.BlockSpec((nseg,) + tuple(shape[1:]), lambda b, t: (b,) + (0,) * (nd - 1),
                            pipeline_mode=pl.Buffered(1))

    def const_spec(arr):
        nd = arr.ndim
        return pl.BlockSpec(arr.shape, lambda b, t: (0,) * nd, pipeline_mode=pl.Buffered(1))

    def cols_spec(arr, width, offset):
        assert offset % width == 0
        return pl.BlockSpec((arr.shape[0], width), lambda b, t: (0, offset // width),
                            pipeline_mode=pl.Buffered(1))

    w_in_packed, others = weights
    p_spec = pl.BlockSpec((1, nseg, seg, P_DIM), lambda b, t: (0, b, t, 0))
    in_specs = [tile_spec(D_MODEL), p_spec]
    operands = [x, p]
    if not zero_init:
        assert tuple(a.shape for a in init_state) == state_shapes
        in_specs += [state_spec(shape) for shape in state_shapes]
        operands += list(init_state)
    for name, arr in others:
        if name == "w_in":
            for width, offset in PACKED_COLS:
                in_specs.append(cols_spec(w_in_packed, width, offset))
                operands.append(w_in_packed)
        else:
            in_specs.append(const_spec(arr))
            operands.append(arr)
    out_shape = (jax.ShapeDtypeStruct(x.shape, F32),) + tuple(
        jax.ShapeDtypeStruct(shape, F32) for shape in state_shapes)
    out_specs = (tile_spec(D_MODEL),) + tuple(state_spec(shape) for shape in state_shapes)
    scratch = [
        pltpu.VMEM((tt, D_MODEL), BF16),
        pltpu.VMEM((tt, KEY_DIM), F32),
        pltpu.VMEM((tt, KEY_DIM), F32),
        pltpu.VMEM((tt, KEY_DIM), F32),
        pltpu.VMEM((tt, LANES), F32),
        pltpu.VMEM((tt, LANES), F32),
        pltpu.VMEM((max(nch, SUBLANES), LANES), F32),
        pltpu.VMEM((tt, KEY_DIM), BF16),
        pltpu.VMEM((tt, KEY_DIM), BF16),
        pltpu.VMEM((N_HEADS // 2, tt, 2 * ch), BF16),
        pltpu.VMEM((SUBLANES + seg, QKV_CHUNK), F32),
        pltpu.VMEM((tt, D_MODEL), F32),
        pltpu.VMEM((tt, WIDTH_B), BF16),
    ]
    return pl.pallas_call(
        functools.partial(_layer_kernel, cfg),
        grid=(nb, nt),
        in_specs=in_specs,
        out_specs=out_specs,
        out_shape=out_shape,
        scratch_shapes=scratch,
        compiler_params=pltpu.CompilerParams(
            dimension_semantics=("arbitrary", "arbitrary"),
            vmem_limit_bytes=VMEM_LIMIT_BYTES),
        name=f"gdn_shortconv_layer_n{nseg}_t{seg}_c{ch}",
    )(*operands)


def _pack_kernel(a_ref, b_ref, o_ref):
    j = pl.program_id(0)
    n_plain = OFF_BETA // PACK_W
    n_shift = (IN_DIM - OFF_BB) // PACK_W
    shift = OFF_BB - OFF_BETA
    row = lax.broadcasted_iota(jnp.int32, (LANES, 1), 0)

    @pl.when(j < n_plain)
    def _():
        o_ref[...] = a_ref[...].T.astype(BF16)

    @pl.when((j >= n_plain) & (j < n_plain + n_shift))
    def _():
        o_ref[...] = jnp.concatenate([a_ref[shift:, :], b_ref[...]], axis=0).T.astype(BF16)

    @pl.when(j == n_plain + n_shift)
    def _():
        beta = jnp.where(row < N_HEADS, a_ref[0:LANES, :], 0.0)
        decay = jnp.where(row < N_HEADS, a_ref[N_HEADS:N_HEADS + LANES, :], 0.0)
        o_ref[:, :LANES] = beta.T.astype(BF16)
        o_ref[:, LANES:2 * LANES] = decay.T.astype(BF16)
        o_ref[:, 2 * LANES:] = jnp.zeros((o_ref.shape[0], PACK_W - 2 * LANES), BF16)


def _pack_w_in(w_t):
    n_plain = OFF_BETA // PACK_W
    n_shift = (IN_DIM - OFF_BB) // PACK_W
    shift = OFF_BB - OFF_BETA
    assert OFF_BETA % PACK_W == 0 and (IN_DIM - OFF_BB) % PACK_W == 0
    assert OFF_DECAY - OFF_BETA == N_HEADS == SUBLANES and shift % SUBLANES == 0

    def a_idx(j):
        return (0, jnp.where(j < n_plain + n_shift, j, n_plain), 0)

    def b_idx(j):
        return (0, jnp.where((j >= n_plain) & (j < n_plain + n_shift), (j + 1) * (PACK_W // shift), 0), 0)

    return pl.pallas_call(
        _pack_kernel,
        grid=(n_plain + n_shift + 1,),
        in_specs=[pl.BlockSpec((None, PACK_W, D_MODEL), a_idx),
                  pl.BlockSpec((None, shift, D_MODEL), b_idx)],
        out_specs=pl.BlockSpec((D_MODEL, PACK_W), lambda j: (0, j)),
        out_shape=jax.ShapeDtypeStruct((D_MODEL, PACKED_DIM), BF16),
        compiler_params=pltpu.CompilerParams(dimension_semantics=("arbitrary",)),
        name="pack_w_in",
    )(w_t, w_t)


def _pad_rows_front(a, rows):
    pad = [(0, 0)] * a.ndim
    pad[-2] = (rows - a.shape[-2], 0)
    return jnp.pad(a, pad)


def _pad_lanes(a):
    pad = [(0, 0)] * a.ndim
    pad[-1] = (0, LANES - a.shape[-1])
    return jnp.pad(a, pad)


def kernel(x_prompt, x_sample, state_conv_a, state_gdn, state_conv_b, p_prompt, p_sample, ln_in_g, ln_in_b, w_in, w_conv_a, a_log, dt_bias, norm_a_g, w_conv_b, w_proj_a, w_proj_b, w_out, ln1_g, ln1_b, w_ple, w_ple_gate, ln2_g, ln2_b):
    assert w_in.shape[0] == DEPTH == 1
    row = lambda v: v.reshape(1, -1).astype(F32)
    w_in_packed = _pack_w_in(jnp.swapaxes(w_in, 1, 2))
    others = (
        ("ln_in_g", row(ln_in_g)), ("ln_in_b", row(ln_in_b)),
        ("w_in", None),
        ("w_conv_a", w_conv_a[0].astype(F32)),
        ("a_log", _pad_lanes(row(a_log[0]))), ("dt_bias", _pad_lanes(row(dt_bias[0]))),
        ("norm_a_g", row(norm_a_g[0])),
        ("w_conv_b", w_conv_b[0].astype(F32)),
        ("w_proj_a", w_proj_a[0].astype(BF16)), ("w_proj_b", w_proj_b[0].astype(BF16)),
        ("w_out", w_out[0].astype(BF16)),
        ("ln1_g", row(ln1_g[0])), ("ln1_b", row(ln1_b[0])),
        ("w_ple", w_ple[0].astype(BF16)), ("w_ple_gate", w_ple_gate[0].astype(BF16)),
        ("ln2_g", row(ln2_g[0])), ("ln2_b", row(ln2_b[0])),
        ("never", jnp.zeros((SUBLANES, LANES), jnp.int32)),
    )
    weights = (w_in_packed, others)

    bp, seq, _ = x_prompt.shape
    y_p, ca_p, s_p, cb_p = _encode(
        x_prompt, p_prompt, None, weights, nseg=bp, seg=PROMPT_TILE, ch=PROMPT_CHUNK)

    bs, ts, _ = x_sample.shape
    y_s, ca_s, s_s, cb_s = _encode(
        x_sample, p_sample,
        (_pad_rows_front(state_conv_a[0], SUBLANES), state_gdn[0].astype(F32),
         _pad_rows_front(state_conv_b[0], SUBLANES)),
        weights, nseg=SAMPLE_TILE_SEQS, seg=ts, ch=ts)

    na, nb = CONV_A - 1, CONV_B - 1
    return (y_p, y_s,
            ca_p[None, :, SUBLANES - na:], s_p[None], cb_p[None, :, SUBLANES - nb:],
            ca_s[None, :, SUBLANES - na:], s_s[None].astype(state_gdn.dtype),
            cb_s[None, :, SUBLANES - nb:])
```

```python
import functools

import jax
import jax.numpy as jnp
from jax import lax
from jax.experimental import pallas as pl
from jax.experimental.pallas import tpu as pltpu

D_MODEL = 1024
N_HEADS = 8
HEAD_D = 128
KEY_DIM = N_HEADS * HEAD_D
QKV_DIM = 3 * KEY_DIM
WIDTH_B = D_MODEL
P_DIM = 256
CONV_A = 4
CONV_B = 3
PROMPT_CHUNK = 64
DEPTH = 1
ALPHA = (2 * DEPTH) ** 0.25
LN_EPS = 1e-5
RMS_EPS = 1e-6
L2_EPS = 1e-6

OFF_ZA = QKV_DIM
OFF_BETA = OFF_ZA + KEY_DIM
OFF_DECAY = OFF_BETA + N_HEADS
OFF_BB = OFF_DECAY + N_HEADS
OFF_CB = OFF_BB + WIDTH_B
OFF_UB = OFF_CB + WIDTH_B
OFF_ZB = OFF_UB + WIDTH_B
OFF_GATE = OFF_ZB + WIDTH_B
IN_DIM = OFF_GATE + 2 * D_MODEL

SUBLANES = 8
LANES = 128
INV_BLOCK = 4
PROMPT_TILE = 256
SAMPLE_TILE_SEQS = 8
P1_CHUNKS = 2
FILL_EVERY = 2
QKV_CHUNK = 256
P2_SEGS = 2
VMEM_LIMIT_BYTES = 127 * 512 * 1024

PACK_W = 512
PACKED_DIM = IN_DIM - (OFF_BB - OFF_BETA) + 2 * LANES
PACKED_COLS = ((QKV_DIM, 0), (KEY_DIM, OFF_ZA), (LANES, OFF_BETA + 4 * WIDTH_B + 2 * D_MODEL),
               (LANES, OFF_BETA + 4 * WIDTH_B + 2 * D_MODEL + LANES), (4 * WIDTH_B, OFF_BETA),
               (2 * D_MODEL, OFF_BETA + 4 * WIDTH_B))

F32 = jnp.float32
NEG_LOG2_E = -1.4426950408889634
BF16 = jnp.bfloat16


def _sigmoid(x):
    return 1.0 / (1.0 + jnp.exp2(x * NEG_LOG2_E))


def _silu(x):
    return x * _sigmoid(x)


def _softplus(x):
    return jnp.maximum(x, 0.0) + jnp.log(1.0 + jnp.exp(-jnp.abs(x)))


def _layer_norm(x, g, b):
    mu = jnp.mean(x, axis=-1, keepdims=True)
    xc = x - mu
    var = jnp.mean(xc * xc, axis=-1, keepdims=True)
    return xc * lax.rsqrt(var + LN_EPS) * g + b


def _causal_conv(seg, hist8, w_ref, ntaps, buf):
    n = seg.shape[0]
    buf[0:SUBLANES, :] = hist8
    buf[SUBLANES:SUBLANES + n, :] = seg
    acc = seg * w_ref[ntaps - 1:ntaps, :]
    for s in range(1, ntaps):
        acc = acc + buf[SUBLANES - s:SUBLANES - s + n, :] * w_ref[ntaps - 1 - s:ntaps - s, :]
    return acc


def _block_diag2(y1, y2):
    z = jnp.zeros_like(y1)
    return jnp.concatenate([jnp.concatenate([y1, z], axis=1), jnp.concatenate([z, y2], axis=1)],
                           axis=0)


def _inv_unit_lower(a_list, row, col, size, mm_each):
    base = (row // INV_BLOCK) == (col // INV_BLOCK)
    d = [jnp.where(base, a, 0.0) for a in a_list]
    d2 = mm_each(d, d)
    dd2 = mm_each(d, d2)
    x = [b - a - c for a, b, c in zip(d, d2, dd2)]
    s = INV_BLOCK
    while s < size:
        pair = ((row // (2 * s)) == (col // (2 * s))) & ((row // s) != (col // s))
        a21 = [jnp.where(pair, a, 0.0) for a in a_list]
        m1 = [a + b for a, b in zip(a21, mm_each(a21, x))]
        m2 = [a + b for a, b in zip(m1, mm_each(x, m1))]
        x = [a - b for a, b in zip(x, m2)]
        s *= 2
    return x


def _layer_kernel(cfg, x_ref, p_ref, *refs):
    init_refs, refs = (None, refs) if cfg[4] else (refs[:3], refs[3:])
    _layer_body(cfg, x_ref, p_ref, init_refs, *refs)


def _layer_body(cfg,
                  x_ref, p_ref, init_refs,
                  ln_in_g, ln_in_b, w_qkv, w_za, w_beta, w_dec, w_b, w_gate,
                  w_conv_a, a_log, dt_bias, norm_a_g, w_conv_b,
                  w_proj_a, w_proj_b, w_out, ln1_g, ln1_b, w_ple, w_ple_gate, ln2_g, ln2_b,
                  y_ref, ca_ref, s_ref, cb_ref,
                  hbf_s, q_s, k_s, v_s, beta_s, g_s, egl_s, w_s, qd_s, qk_s, cbuf_s, pbg_s, ybf_s):
    u_s, kd_s, o_s = v_s, k_s, q_s
    nseg, seg, ch, carry, zero_init = cfg
    tt = nseg * seg
    nch = tt // ch
    nch_seg = seg // ch
    p1_chunks = min(P1_CHUNKS, nch)
    t_idx = pl.program_id(1)

    def _seed():
        if zero_init:
            ca_ref[...] = jnp.zeros(ca_ref.shape, F32)
            cb_ref[...] = jnp.zeros(cb_ref.shape, F32)
            s_ref[...] = jnp.zeros(s_ref.shape, F32)
        else:
            hista_ref, sin_ref, histb_ref = init_refs
            ca_ref[...] = hista_ref[...]
            cb_ref[...] = histb_ref[...]
            s_ref[...] = sin_ref[...]

    if carry:
        pl.when(t_idx == 0)(_seed)
    else:
        _seed()

    def conv_tile(pre, hist_ref, col, w_ref, ntaps, seqs=None):
        seqs = range(nseg) if seqs is None else seqs
        outs = []
        for n, i in enumerate(seqs):
            part = pre[n * seg:(n + 1) * seg]
            outs.append(_causal_conv(part, hist_ref[i, :, col], w_ref.at[:, col], ntaps,
                                     cbuf_s))
            hist_ref[i, :, col] = part[seg - SUBLANES:]
        return outs[0] if len(outs) == 1 else jnp.concatenate(outs, axis=0)

    halves = []
    for k in range(2):
        sq = slice(k * nseg // 2, (k + 1) * nseg // 2)
        halves.append({"rs": slice(k * tt // 2, (k + 1) * tt // 2), "sq": sq,
                       "seqs": range(sq.start, sq.stop)})

    for hf in halves:
        h = _layer_norm(x_ref[hf["sq"]].reshape(tt // 2, D_MODEL), ln_in_g[...], ln_in_b[...])
        y_ref[hf["sq"]] = h.reshape(nseg // 2, seg, D_MODEL)
        hbf_s[hf["rs"], :] = h.astype(BF16)

    for j in range(QKV_DIM // QKV_CHUNK):
        col = slice(j * QKV_CHUNK, (j + 1) * QKV_CHUNK)
        grp, sub = divmod(j, KEY_DIM // QKV_CHUNK)
        dst = (q_s, k_s, v_s)[grp]
        for hf in halves:
            hf["pre"] = jnp.dot(hbf_s[hf["rs"], :], w_qkv[:, col], preferred_element_type=F32)
        for hf in halves:
            act = _silu(conv_tile(hf.pop("pre"), ca_ref, col, w_conv_a, CONV_A, hf["seqs"]))
            for hh in range(QKV_CHUNK // HEAD_D):
                hc = slice(sub * QKV_CHUNK + hh * HEAD_D, sub * QKV_CHUNK + (hh + 1) * HEAD_D)
                xh = act[:, hh * HEAD_D:(hh + 1) * HEAD_D]
                if grp == 2:
                    dst[hf["rs"], hc] = xh
                else:
                    scale = HEAD_D ** -0.5 if grp == 0 else 1.0
                    ss = jnp.sum(xh * xh, axis=-1, keepdims=True)
                    dst[hf["rs"], hc] = xh * (lax.rsqrt(ss + L2_EPS) * scale)

    beta_s[...] = _sigmoid(jnp.dot(hbf_s[...], w_beta[...], preferred_element_type=F32))
    zdec = jnp.dot(hbf_s[...], w_dec[...], preferred_element_type=F32)
    g_s[...] = -jnp.exp(a_log[...]) * _softplus(zdec + dt_bias[...])

    fill_queue = []
    held = {}

    fill_calls = [0]

    def fill():
        fill_calls[0] += 1
        if fill_queue and fill_calls[0] % FILL_EVERY == 0:
            fill_queue.pop(0)()

    def b_proj(k, cj):
        return jnp.dot(hbf_s[...], w_b[:, k * WIDTH_B + cj.start:k * WIDTH_B + cj.stop],
                       preferred_element_type=F32)

    def b_tile_steps(cj):
        def step_c():
            held["cb"] = b_proj(1, cj)

        def step_conv():
            cu = held.pop("cb") * b_proj(2, cj)
            pbg_s[:, cj] = conv_tile(cu, cb_ref, cj, w_conv_b, CONV_B)

        def step_b():
            pbg_s[:, cj] = b_proj(0, cj) * pbg_s[:, cj]

        def step_gate():
            ybf_s[:, cj] = (pbg_s[:, cj] * _silu(b_proj(3, cj))).astype(BF16)

        return [step_c, step_conv, step_b, step_gate]

    def b_out_steps(cj):
        def step_proj():
            pbg_s[:, cj] = jnp.dot(ybf_s[...], w_proj_b[:, cj], preferred_element_type=F32)

        def step_merge_gate():
            gate_b = _sigmoid(jnp.dot(hbf_s[...], w_gate[:, D_MODEL + cj.start:D_MODEL + cj.stop],
                                      preferred_element_type=F32))
            pbg_s[:, cj] = gate_b * pbg_s[:, cj]

        return [step_proj, step_merge_gate]

    b_tiles = [slice(j * QKV_CHUNK, (j + 1) * QKV_CHUNK) for j in range(WIDTH_B // QKV_CHUNK)]
    for cj in b_tiles:
        fill_queue += b_tile_steps(cj)
    for cj in b_tiles:
        fill_queue += b_out_steps(cj)

    pw = 2 * ch
    ri = lax.broadcasted_iota(jnp.int32, (ch, pw), 0)
    cn = lax.broadcasted_iota(jnp.int32, (ch, pw), 1)
    cj = cn & (ch - 1)
    hi = cn >= ch
    hi_row = lax.broadcasted_iota(jnp.int32, (1, pw), 1) >= ch
    causal = ri >= cj
    strict = ri > cj
    rt = lax.broadcasted_iota(jnp.int32, (ch, ch), 0)
    ct = lax.broadcasted_iota(jnp.int32, (ch, ch), 1)
    ltri = (rt >= ct).astype(F32)

    pairs = range(N_HEADS // 2)
    pcols = [slice(p * 2 * HEAD_D, (p + 1) * 2 * HEAD_D) for p in pairs]
    hcols = [slice(hd * HEAD_D, (hd + 1) * HEAD_D) for hd in range(N_HEADS)]

    def bd_pair(y):
        zero = jnp.zeros_like(y)
        return jnp.concatenate([jnp.where(hi, zero, y), jnp.where(hi, y, zero)], axis=0)

    def bd_wide(y):
        return _block_diag2(y[:, :HEAD_D], y[:, HEAD_D:])

    def mm_pairs(xs, ys):
        out = [jnp.dot(x.astype(BF16), bd_pair(y.astype(BF16)), preferred_element_type=F32)
               for x, y in zip(xs, ys)]
        fill()
        return out

    def lanes_of_pair(cols, p):
        return jnp.concatenate([jnp.broadcast_to(cols[:, 2 * p:2 * p + 1], (ch, HEAD_D)),
                                jnp.broadcast_to(cols[:, 2 * p + 1:2 * p + 2], (ch, HEAD_D))],
                               axis=1)

    def block_rows(blk):
        start = blk * ch
        return pl.ds(start if isinstance(start, int) else pl.multiple_of(start, ch), ch)

    per_seq = max(p1_chunks // nseg, 1) if nch_seg > 1 else 0

    def phase1(it):
        units = []
        pre = {}
        for j in range(p1_chunks):
            if nch_seg == 1:
                c = it * p1_chunks + j
            else:
                c = (j // per_seq) * nch_seg + it * per_seq + j % per_seq
            rows = block_rows(c)
            g_c = g_s[rows, :]
            beta_c = beta_s[rows, :]
            for p in pairs:
                units.append((j, rows, p))
                pre[j, p] = (k_s[rows, pcols[p]], q_s[rows, pcols[p]], v_s[rows, pcols[p]])
            gc = jnp.dot(ltri, g_c, preferred_element_type=F32,
                         precision=lax.Precision.HIGHEST)
            gc_rows = jnp.concatenate([gc, gc], axis=0).T
            glast = gc[ch - 1:ch, :]
            pre[j] = (c, gc, gc_rows, beta_c, jnp.exp(gc), jnp.exp(glast - gc), jnp.exp(glast))

        dmat, lhs, rk, rhs = [], [], [], []
        for j, rows, p in units:
            kp, qp, vp = pre[j, p]
            _, gc, gc_rows, beta_c, egc, _, _ = pre[j]
            g_col = jnp.where(hi, gc[:, 2 * p + 1:2 * p + 2], gc[:, 2 * p:2 * p + 1])
            g_row = jnp.where(hi_row, gc_rows[2 * p + 1:2 * p + 2, :], gc_rows[2 * p:2 * p + 1, :])
            dmat.append(jnp.where(causal, jnp.exp(jnp.where(causal, g_col - g_row, 0.0)), 0.0))
            kb = kp * lanes_of_pair(beta_c, p)
            lhs.append(jnp.concatenate([kb, qp], axis=0).astype(BF16))
            rk.append(bd_wide(kp.astype(BF16)))
            rhs.append((vp * lanes_of_pair(beta_c, p), kb * lanes_of_pair(egc, p)))
        kkqk = [lax.dot_general(a, b, (((1,), (1,)), ((), ())), preferred_element_type=F32)
                for a, b in zip(lhs, rk)]
        fill()
        a_mat = [jnp.where(strict, x[:ch] * d, 0.0) for x, d in zip(kkqk, dmat)]
        t_x = _inv_unit_lower(a_mat, ri, cj, ch, mm_pairs)
        duw = [jnp.dot(t.astype(BF16),
                       jnp.concatenate([bd_wide(ru.astype(BF16)), bd_wide(rw.astype(BF16))], axis=1),
                       preferred_element_type=F32)
               for t, (ru, rw) in zip(t_x, rhs)]

        fill()
        for i, (j, rows, p) in enumerate(units):
            kp, qp = k_s[rows, pcols[p]], q_s[rows, pcols[p]]
            _, _, _, _, egc, ekd, _ = pre[j]
            qk_s[p, rows, :] = (kkqk[i][ch:] * dmat[i]).astype(BF16)
            u_s[rows, pcols[p]] = rhs[i][0] + duw[i][:, :2 * HEAD_D]
            w_s[rows, pcols[p]] = (rhs[i][1] + duw[i][:, 2 * HEAD_D:]).astype(BF16)
            qd_s[rows, pcols[p]] = (qp * lanes_of_pair(egc, p)).astype(BF16)
            kd_s[rows, pcols[p]] = kp * lanes_of_pair(ekd, p)
        for j in range(p1_chunks):
            egl_s[pl.ds(pre[j][0], 1), :] = pre[j][6]


    ngroups = nseg // P2_SEGS
    assert ngroups == 1 or nch_seg == 1

    def phase2(it):
        c, sg = (it, 0) if ngroups == 1 else (0, it)
        units = []
        egl = {}
        for i in range(P2_SEGS):
            sq = sg * P2_SEGS + i
            blk = sq * nch_seg + c
            rows = block_rows(blk)
            egl[i] = egl_s[pl.ds(blk, 1), :]
            units += [(i, sq, rows, p) for p in pairs]
        s_old = {(i, hd): s_ref[sq, hd] for i, sq, _, p in units for hd in (2 * p, 2 * p + 1)}
        wq = [jnp.concatenate([w_s[rows, pcols[p]], qd_s[rows, pcols[p]]], axis=0)
              for _, _, rows, p in units]
        u = [u_s[rows, pcols[p]] for _, _, rows, p in units]
        qk = [qk_s[p, rows, :] for _, _, rows, p in units]
        kd_t = {(i, hd): kd_s[rows, hcols[hd]].T.astype(BF16)
                for i, _, rows, p in units for hd in (2 * p, 2 * p + 1)}
        s_bd = [_block_diag2(s_old[i, 2 * p].astype(BF16), s_old[i, 2 * p + 1].astype(BF16))
                for i, _, _, p in units]
        ws = [jnp.dot(a, sb, preferred_element_type=F32) for a, sb in zip(wq, s_bd)]
        v_new = [(a - b[:ch]).astype(BF16) for a, b in zip(u, ws)]
        o = [b[ch:] + jnp.dot(a, bd_wide(v), preferred_element_type=F32)
             for a, b, v in zip(qk, ws, v_new)]
        s_new = {}
        for n, (i, _, _, p) in enumerate(units):
            for half, hd in enumerate((2 * p, 2 * p + 1)):
                s_new[i, hd] = s_old[i, hd] * egl[i][:, hd:hd + 1] + jnp.dot(
                    kd_t[i, hd], v_new[n][:, half * HEAD_D:(half + 1) * HEAD_D],
                    preferred_element_type=F32)
        for n, (i, sq, rows, p) in enumerate(units):
            o_s[rows, pcols[p]] = o[n]
            s_ref[sq, 2 * p] = s_new[i, 2 * p]
            s_ref[sq, 2 * p + 1] = s_new[i, 2 * p + 1]

    n_p1, n_p2 = nch // p1_chunks, nch_seg * ngroups
    phase1(0)
    done2 = 0
    for it1 in range(1, n_p1 + 1):
        if it1 < n_p1:
            phase1(it1)
        ready2 = n_p2 * it1 // n_p1
        for it2 in range(done2, ready2):
            phase2(it2)
        done2 = ready2
    while fill_queue:
        fill()

    def stage_gate_proj(hf):
        lhs = hbf_s[hf["rs"], :]
        hf["za"] = jnp.dot(lhs, w_za[...], preferred_element_type=F32)
        hf["ga"] = jnp.dot(lhs, w_gate[:, :D_MODEL], preferred_element_type=F32)

    def stage_gated_norm(hf):
        ya = []
        for hd in range(N_HEADS):
            hc = slice(hd * HEAD_D, (hd + 1) * HEAD_D)
            oh = o_s[hf["rs"], hc]
            ms = jnp.mean(oh * oh, axis=-1, keepdims=True)
            ya.append((oh * lax.rsqrt(ms + RMS_EPS) * norm_a_g[...]
                       * _silu(hf["za"][:, hc])).astype(BF16))
        hf["ya"] = jnp.concatenate(ya, axis=1)

    def stage_merge(hf):
        pa = jnp.dot(hf.pop("ya"), w_proj_a[...], preferred_element_type=F32)
        hf["merged"] = (_sigmoid(hf.pop("ga")) * pa + pbg_s[hf["rs"], :]).astype(BF16)

    def stage_out_proj(hf):
        hf["mo"] = jnp.dot(hf.pop("merged"), w_out[...], preferred_element_type=F32)

    def stage_norm1(hf):
        h_in = y_ref[hf["sq"]].reshape(tt // 2, D_MODEL)
        hf["h1"] = _layer_norm(ALPHA * h_in + hf.pop("mo"), ln1_g[...], ln1_b[...])

    def stage_ple_proj(hf):
        hf["pg"] = jnp.dot(hf["h1"].astype(BF16), w_ple_gate[...], preferred_element_type=F32)
        hf["pp"] = jnp.dot(p_ref[0, hf["sq"]].reshape(tt // 2, P_DIM).astype(BF16), w_ple[...],
                           preferred_element_type=F32)

    def stage_norm2(hf):
        ple = _sigmoid(hf.pop("pg")) * hf.pop("pp")
        y_ref[hf["sq"]] = _layer_norm(ALPHA * hf.pop("h1") + ple, ln2_g[...], ln2_b[...]).reshape(
            nseg // 2, seg, D_MODEL)

    for stage in (stage_gate_proj, stage_gated_norm, stage_merge, stage_out_proj, stage_norm1,
                  stage_ple_proj, stage_norm2):
        for hf in halves:
            stage(hf)


def _encode(x, p, init_state, weights, *, nseg, seg, ch):
    nseq, t_len, _ = x.shape
    zero_init = init_state is None
    state_shapes = ((nseq, SUBLANES, QKV_DIM), (nseq, N_HEADS, HEAD_D, HEAD_D),
                    (nseq, SUBLANES, WIDTH_B))
    nb, nt = nseq // nseg, t_len // seg
    carry = nt > 1
    assert nseq % nseg == 0 and t_len % seg == 0 and seg % ch == 0 and seg % SUBLANES == 0
    assert ch & (ch - 1) == 0 and ch % INV_BLOCK == 0 and nseg % P2_SEGS == 0
    tt = nseg * seg
    nch = tt // ch
    cfg = (nseg, seg, ch, carry, zero_init)

    def tile_spec(width):
        return pl.BlockSpec((nseg, seg, width), lambda b, t: (b, t, 0))

    def state_spec(shape):
        nd = len(shape)
        return pl.BlockSpec((nseg,) + tuple(shape[1:]), lambda b, t: (b,) + (0,) * (nd - 1),
                            pipeline_mode=pl.Buffered(1))

    def const_spec(arr):
        nd = arr.ndim
        return pl.BlockSpec(arr.shape, lambda b, t: (0,) * nd, pipeline_mode=pl.Buffered(1))

    def cols_spec(arr, width, offset):
        assert offset % width == 0
        return pl.BlockSpec((arr.shape[0], width), lambda b, t: (0, offset // width),
                            pipeline_mode=pl.Buffered(1))

    w_in_packed, others = weights
    p_spec = pl.BlockSpec((1, nseg, seg, P_DIM), lambda b, t: (0, b, t, 0))
    in_specs = [tile_spec(D_MODEL), p_spec]
    operands = [x, p]
    if not zero_init:
        assert tuple(a.shape for a in init_state) == state_shapes
        in_specs += [state_spec(shape) for shape in state_shapes]
        operands += list(init_state)
    for name, arr in others:
        if name == "w_in":
            for width, offset in PACKED_COLS:
                in_specs.append(cols_spec(w_in_packed, width, offset))
                operands.append(w_in_packed)
        else:
            in_specs.append(const_spec(arr))
            operands.append(arr)
    out_shape = (jax.ShapeDtypeStruct(x.shape, F32),) + tuple(
        jax.ShapeDtypeStruct(shape, F32) for shape in state_shapes)
    out_specs = (tile_spec(D_MODEL),) + tuple(state_spec(shape) for shape in state_shapes)
    scratch = [
        pltpu.VMEM((tt, D_MODEL), BF16),
        pltpu.VMEM((tt, KEY_DIM), F32),
        pltpu.VMEM((tt, KEY_DIM), F32),
        pltpu.VMEM((tt, KEY_DIM), F32),
        pltpu.VMEM((tt, LANES), F32),
        pltpu.VMEM((tt, LANES), F32),
        pltpu.VMEM((max(nch, SUBLANES), LANES), F32),
        pltpu.VMEM((tt, KEY_DIM), BF16),
        pltpu.VMEM((tt, KEY_DIM), BF16),
        pltpu.VMEM((N_HEADS // 2, tt, 2 * ch), BF16),
        pltpu.VMEM((SUBLANES + seg, QKV_CHUNK), F32),
        pltpu.VMEM((tt, D_MODEL), F32),
        pltpu.VMEM((tt, WIDTH_B), BF16),
    ]
    return pl.pallas_call(
        functools.partial(_layer_kernel, cfg),
        grid=(nb, nt),
        in_specs=in_specs,
        out_specs=out_specs,
        out_shape=out_shape,
        scratch_shapes=scratch,
        compiler_params=pltpu.CompilerParams(
            dimension_semantics=("arbitrary", "arbitrary"),
            vmem_limit_bytes=VMEM_LIMIT_BYTES),
        name=f"gdn_shortconv_layer_n{nseg}_t{seg}_c{ch}",
    )(*operands)


def _pack_kernel(a_ref, b_ref, o_ref):
    j = pl.program_id(0)
    n_plain = OFF_BETA // PACK_W
    n_shift = (IN_DIM - OFF_BB) // PACK_W
    shift = OFF_BB - OFF_BETA
    row = lax.broadcasted_iota(jnp.int32, (LANES, 1), 0)

    @pl.when(j < n_plain)
    def _():
        o_ref[...] = a_ref[...].T.astype(BF16)

    @pl.when((j >= n_plain) & (j < n_plain + n_shift))
    def _():
        o_ref[...] = jnp.concatenate([a_ref[shift:, :], b_ref[...]], axis=0).T.astype(BF16)

    @pl.when(j == n_plain + n_shift)
    def _():
        beta = jnp.where(row < N_HEADS, a_ref[0:LANES, :], 0.0)
        decay = jnp.where(row < N_HEADS, a_ref[N_HEADS:N_HEADS + LANES, :], 0.0)
        o_ref[:, :LANES] = beta.T.astype(BF16)
        o_ref[:, LANES:2 * LANES] = decay.T.astype(BF16)
        o_ref[:, 2 * LANES:] = jnp.zeros((o_ref.shape[0], PACK_W - 2 * LANES), BF16)


def _pack_w_in(w_t):
    n_plain = OFF_BETA // PACK_W
    n_shift = (IN_DIM - OFF_BB) // PACK_W
    shift = OFF_BB - OFF_BETA
    assert OFF_BETA % PACK_W == 0 and (IN_DIM - OFF_BB) % PACK_W == 0
    assert OFF_DECAY - OFF_BETA == N_HEADS == SUBLANES and shift % SUBLANES == 0

    def a_idx(j):
        return (0, jnp.where(j < n_plain + n_shift, j, n_plain), 0)

    def b_idx(j):
        return (0, jnp.where((j >= n_plain) & (j < n_plain + n_shift), (j + 1) * (PACK_W // shift), 0), 0)

    return pl.pallas_call(
        _pack_kernel,
        grid=(n_plain + n_shift + 1,),
        in_specs=[pl.BlockSpec((None, PACK_W, D_MODEL), a_idx),
                  pl.BlockSpec((None, shift, D_MODEL), b_idx)],
        out_specs=pl.BlockSpec((D_MODEL, PACK_W), lambda j: (0, j)),
        out_shape=jax.ShapeDtypeStruct((D_MODEL, PACKED_DIM), BF16),
        compiler_params=pltpu.CompilerParams(dimension_semantics=("arbitrary",)),
        name="pack_w_in",
    )(w_t, w_t)


def _pad_rows_front(a, rows):
    pad = [(0, 0)] * a.ndim
    pad[-2] = (rows - a.shape[-2], 0)
    return jnp.pad(a, pad)


def _pad_lanes(a):
    pad = [(0, 0)] * a.ndim
    pad[-1] = (0, LANES - a.shape[-1])
    return jnp.pad(a, pad)


def kernel(x_prompt, x_sample, state_conv_a, state_gdn, state_conv_b, p_prompt, p_sample, ln_in_g, ln_in_b, w_in, w_conv_a, a_log, dt_bias, norm_a_g, w_conv_b, w_proj_a, w_proj_b, w_out, ln1_g, ln1_b, w_ple, w_ple_gate, ln2_g, ln2_b):
    assert w_in.shape[0] == DEPTH == 1
    row = lambda v: v.reshape(1, -1).astype(F32)
    w_in_packed = _pack_w_in(jnp.swapaxes(w_in, 1, 2))
    others = (
        ("ln_in_g", row(ln_in_g)), ("ln_in_b", row(ln_in_b)),
        ("w_in", None),
        ("w_conv_a", w_conv_a[0].astype(F32)),
        ("a_log", _pad_lanes(row(a_log[0]))), ("dt_bias", _pad_lanes(row(dt_bias[0]))),
        ("norm_a_g", row(norm_a_g[0])),
        ("w_conv_b", w_conv_b[0].astype(F32)),
        ("w_proj_a", w_proj_a[0].astype(BF16)), ("w_proj_b", w_proj_b[0].astype(BF16)),
        ("w_out", w_out[0].astype(BF16)),
        ("ln1_g", row(ln1_g[0])), ("ln1_b", row(ln1_b[0])),
        ("w_ple", w_ple[0].astype(BF16)), ("w_ple_gate", w_ple_gate[0].astype(BF16)),
        ("ln2_g", row(ln2_g[0])), ("ln2_b", row(ln2_b[0])),
    )
    weights = (w_in_packed, others)

    bp, seq, _ = x_prompt.shape
    y_p, ca_p, s_p, cb_p = _encode(
        x_prompt, p_prompt, None, weights, nseg=bp, seg=PROMPT_TILE, ch=PROMPT_CHUNK)

    bs, ts, _ = x_sample.shape
    y_s, ca_s, s_s, cb_s = _encode(
        x_sample, p_sample,
        (_pad_rows_front(state_conv_a[0], SUBLANES), state_gdn[0].astype(F32),
         _pad_rows_front(state_conv_b[0], SUBLANES)),
        weights, nseg=SAMPLE_TILE_SEQS, seg=ts, ch=ts)

    na, nb = CONV_A - 1, CONV_B - 1
    return (y_p, y_s,
            ca_p[None, :, SUBLANES - na:], s_p[None], cb_p[None, :, SUBLANES - nb:],
            ca_s[None, :, SUBLANES - na:], s_s[None].astype(state_gdn.dtype),
            cb_s[None, :, SUBLANES - nb:])
```

```python
import functools

import jax
import jax.numpy as jnp
from jax import lax
from jax.experimental import pallas as pl
from jax.experimental.pallas import tpu as pltpu

D_MODEL = 1024
N_HEADS = 8
HEAD_D = 128
KEY_DIM = N_HEADS * HEAD_D
QKV_DIM = 3 * KEY_DIM
WIDTH_B = D_MODEL
P_DIM = 256
CONV_A = 4
CONV_B = 3
PROMPT_CHUNK = 64
DEPTH = 1
ALPHA = (2 * DEPTH) ** 0.25
LN_EPS = 1e-5
RMS_EPS = 1e-6
L2_EPS = 1e-6

OFF_ZA = QKV_DIM
OFF_BETA = OFF_ZA + KEY_DIM
OFF_DECAY = OFF_BETA + N_HEADS
OFF_BB = OFF_DECAY + N_HEADS
OFF_CB = OFF_BB + WIDTH_B
OFF_UB = OFF_CB + WIDTH_B
OFF_ZB = OFF_UB + WIDTH_B
OFF_GATE = OFF_ZB + WIDTH_B
IN_DIM = OFF_GATE + 2 * D_MODEL

SUBLANES = 8
LANES = 128
INV_BLOCK = 4
PROMPT_TILE = 256
SAMPLE_TILE_SEQS = 8
P1_CHUNKS = 2
FILL_EVERY = 2
QKV_CHUNK = 256
P2_SEGS = 2
VMEM_LIMIT_BYTES = 127 * 512 * 1024

PACK_W = 512
PACKED_DIM = IN_DIM - (OFF_BB - OFF_BETA) + 2 * LANES
PACKED_COLS = ((QKV_DIM, 0), (KEY_DIM, OFF_ZA), (LANES, OFF_BETA + 4 * WIDTH_B + 2 * D_MODEL),
               (LANES, OFF_BETA + 4 * WIDTH_B + 2 * D_MODEL + LANES), (4 * WIDTH_B, OFF_BETA),
               (2 * D_MODEL, OFF_BETA + 4 * WIDTH_B))

F32 = jnp.float32
NEG_LOG2_E = -1.4426950408889634
BF16 = jnp.bfloat16


def _sigmoid(x):
    return 1.0 / (1.0 + jnp.exp2(x * NEG_LOG2_E))


def _silu(x):
    return x * _sigmoid(x)


def _softplus(x):
    return jnp.maximum(x, 0.0) + jnp.log(1.0 + jnp.exp(-jnp.abs(x)))


def _layer_norm(x, g, b):
    mu = jnp.mean(x, axis=-1, keepdims=True)
    xc = x - mu
    var = jnp.mean(xc * xc, axis=-1, keepdims=True)
    return xc * lax.rsqrt(var + LN_EPS) * g + b


def _causal_conv(seg, hist8, w_ref, ntaps, buf):
    n = seg.shape[0]
    buf[0:SUBLANES, :] = hist8
    buf[SUBLANES:SUBLANES + n, :] = seg
    acc = seg * w_ref[ntaps - 1:ntaps, :]
    for s in range(1, ntaps):
        acc = acc + buf[SUBLANES - s:SUBLANES - s + n, :] * w_ref[ntaps - 1 - s:ntaps - s, :]
    return acc


def _block_diag2(y1, y2):
    z = jnp.zeros_like(y1)
    return jnp.concatenate([jnp.concatenate([y1, z], axis=1), jnp.concatenate([z, y2], axis=1)],
                           axis=0)


def _inv_unit_lower(a_list, row, col, size, mm_each):
    base = (row // INV_BLOCK) == (col // INV_BLOCK)
    d = [jnp.where(base, a, 0.0) for a in a_list]
    d2 = mm_each(d, d)
    dd2 = mm_each(d, d2)
    x = [b - a - c for a, b, c in zip(d, d2, dd2)]
    s = INV_BLOCK
    while s < size:
        pair = ((row // (2 * s)) == (col // (2 * s))) & ((row // s) != (col // s))
        a21 = [jnp.where(pair, a, 0.0) for a in a_list]
        m1 = [a + b for a, b in zip(a21, mm_each(a21, x))]
        m2 = [a + b for a, b in zip(m1, mm_each(x, m1))]
        x = [a - b for a, b in zip(x, m2)]
        s *= 2
    return x


def _layer_kernel(cfg, x_ref, p_ref, *refs):
    init_refs, refs = (None, refs) if cfg[4] else (refs[:3], refs[3:])
    _layer_body(cfg, x_ref, p_ref, init_refs, *refs)


def _layer_body(cfg,
                  x_ref, p_ref, init_refs,
                  ln_in_g, ln_in_b, w_qkv, w_za, w_beta, w_dec, w_b, w_gate,
                  w_conv_a, a_log, dt_bias, norm_a_g, w_conv_b,
                  w_proj_a, w_proj_b, w_out, ln1_g, ln1_b, w_ple, w_ple_gate, ln2_g, ln2_b,
                  y_ref, ca_ref, s_ref, cb_ref,
                  hbf_s, q_s, k_s, v_s, beta_s, g_s, egl_s, w_s, qd_s, qk_s, cbuf_s, pbg_s, ybf_s):
    u_s, kd_s, o_s = v_s, k_s, q_s
    nseg, seg, ch, carry, zero_init = cfg
    tt = nseg * seg
    nch = tt // ch
    nch_seg = seg // ch
    p1_chunks = min(P1_CHUNKS, nch)
    t_idx = pl.program_id(1)

    def _seed():
        if zero_init:
            ca_ref[...] = jnp.zeros(ca_ref.shape, F32)
            cb_ref[...] = jnp.zeros(cb_ref.shape, F32)
            s_ref[...] = jnp.zeros(s_ref.shape, F32)
        else:
            hista_ref, sin_ref, histb_ref = init_refs
            ca_ref[...] = hista_ref[...]
            cb_ref[...] = histb_ref[...]
            s_ref[...] = sin_ref[...]

    if carry:
        pl.when(t_idx == 0)(_seed)
    else:
        _seed()

    def conv_tile(pre, hist_ref, col, w_ref, ntaps, seqs=None):
        seqs = range(nseg) if seqs is None else seqs
        outs = []
        for n, i in enumerate(seqs):
            part = pre[n * seg:(n + 1) * seg]
            outs.append(_causal_conv(part, hist_ref[i, :, col], w_ref.at[:, col], ntaps,
                                     cbuf_s))
            hist_ref[i, :, col] = part[seg - SUBLANES:]
        return outs[0] if len(outs) == 1 else jnp.concatenate(outs, axis=0)

    halves = []
    for k in range(2):
        sq = slice(k * nseg // 2, (k + 1) * nseg // 2)
        halves.append({"rs": slice(k * tt // 2, (k + 1) * tt // 2), "sq": sq,
                       "seqs": range(sq.start, sq.stop)})

    for hf in halves:
        h = _layer_norm(x_ref[hf["sq"]].reshape(tt // 2, D_MODEL), ln_in_g[...], ln_in_b[...])
        y_ref[hf["sq"]] = h.reshape(nseg // 2, seg, D_MODEL)
        hbf_s[hf["rs"], :] = h.astype(BF16)

    n_chunks = QKV_DIM // QKV_CHUNK
    work = [(j, hf) for j in range(n_chunks) for hf in halves]

    def project(j, hf):
        col = slice(j * QKV_CHUNK, (j + 1) * QKV_CHUNK)
        return jnp.dot(hbf_s[hf["rs"], :], w_qkv[:, col], preferred_element_type=F32)

    pre = project(*work[0])
    for n, (j, hf) in enumerate(work):
        pre_next = project(*work[n + 1]) if n + 1 < len(work) else None
        col = slice(j * QKV_CHUNK, (j + 1) * QKV_CHUNK)
        grp, sub = divmod(j, KEY_DIM // QKV_CHUNK)
        dst = (q_s, k_s, v_s)[grp]
        act = _silu(conv_tile(pre, ca_ref, col, w_conv_a, CONV_A, hf["seqs"]))
        for hh in range(QKV_CHUNK // HEAD_D):
            hc = slice(sub * QKV_CHUNK + hh * HEAD_D, sub * QKV_CHUNK + (hh + 1) * HEAD_D)
            xh = act[:, hh * HEAD_D:(hh + 1) * HEAD_D]
            if grp == 2:
                dst[hf["rs"], hc] = xh
            else:
                scale = HEAD_D ** -0.5 if grp == 0 else 1.0
                ss = jnp.sum(xh * xh, axis=-1, keepdims=True)
                dst[hf["rs"], hc] = xh * (lax.rsqrt(ss + L2_EPS) * scale)
        pre = pre_next

    beta_s[...] = _sigmoid(jnp.dot(hbf_s[...], w_beta[...], preferred_element_type=F32))
    zdec = jnp.dot(hbf_s[...], w_dec[...], preferred_element_type=F32)
    g_s[...] = -jnp.exp(a_log[...]) * _softplus(zdec + dt_bias[...])

    fill_queue = []
    held = {}

    fill_calls = [0]

    def fill():
        fill_calls[0] += 1
        if fill_queue and fill_calls[0] % FILL_EVERY == 0:
            fill_queue.pop(0)()

    def b_proj(k, cj):
        return jnp.dot(hbf_s[...], w_b[:, k * WIDTH_B + cj.start:k * WIDTH_B + cj.stop],
                       preferred_element_type=F32)

    def b_tile_steps(cj):
        def step_c():
            held["cb"] = b_proj(1, cj)

        def step_conv():
            cu = held.pop("cb") * b_proj(2, cj)
            pbg_s[:, cj] = conv_tile(cu, cb_ref, cj, w_conv_b, CONV_B)

        def step_b():
            pbg_s[:, cj] = b_proj(0, cj) * pbg_s[:, cj]

        def step_gate():
            ybf_s[:, cj] = (pbg_s[:, cj] * _silu(b_proj(3, cj))).astype(BF16)

        return [step_c, step_conv, step_b, step_gate]

    def b_out_steps(cj):
        def step_proj():
            pbg_s[:, cj] = jnp.dot(ybf_s[...], w_proj_b[:, cj], preferred_element_type=F32)

        def step_merge_gate():
            gate_b = _sigmoid(jnp.dot(hbf_s[...], w_gate[:, D_MODEL + cj.start:D_MODEL + cj.stop],
                                      preferred_element_type=F32))
            pbg_s[:, cj] = gate_b * pbg_s[:, cj]

        return [step_proj, step_merge_gate]

    b_tiles = [slice(j * QKV_CHUNK, (j + 1) * QKV_CHUNK) for j in range(WIDTH_B // QKV_CHUNK)]
    for cj in b_tiles:
        fill_queue += b_tile_steps(cj)
    for cj in b_tiles:
        fill_queue += b_out_steps(cj)

    pw = 2 * ch
    ri = lax.broadcasted_iota(jnp.int32, (ch, pw), 0)
    cn = lax.broadcasted_iota(jnp.int32, (ch, pw), 1)
    cj = cn & (ch - 1)
    hi = cn >= ch
    hi_row = lax.broadcasted_iota(jnp.int32, (1, pw), 1) >= ch
    causal = ri >= cj
    strict = ri > cj
    rt = lax.broadcasted_iota(jnp.int32, (ch, ch), 0)
    ct = lax.broadcasted_iota(jnp.int32, (ch, ch), 1)
    ltri = (rt >= ct).astype(F32)

    pairs = range(N_HEADS // 2)
    pcols = [slice(p * 2 * HEAD_D, (p + 1) * 2 * HEAD_D) for p in pairs]
    hcols = [slice(hd * HEAD_D, (hd + 1) * HEAD_D) for hd in range(N_HEADS)]

    def bd_pair(y):
        zero = jnp.zeros_like(y)
        return jnp.concatenate([jnp.where(hi, zero, y), jnp.where(hi, y, zero)], axis=0)

    def bd_wide(y):
        return _block_diag2(y[:, :HEAD_D], y[:, HEAD_D:])

    def mm_pairs(xs, ys):
        out = [jnp.dot(x.astype(BF16), bd_pair(y.astype(BF16)), preferred_element_type=F32)
               for x, y in zip(xs, ys)]
        fill()
        return out

    def lanes_of_pair(cols, p):
        return jnp.concatenate([jnp.broadcast_to(cols[:, 2 * p:2 * p + 1], (ch, HEAD_D)),
                                jnp.broadcast_to(cols[:, 2 * p + 1:2 * p + 2], (ch, HEAD_D))],
                               axis=1)

    def block_rows(blk):
        start = blk * ch
        return pl.ds(start if isinstance(start, int) else pl.multiple_of(start, ch), ch)

    per_seq = max(p1_chunks // nseg, 1) if nch_seg > 1 else 0

    def phase1(it):
        units = []
        pre = {}
        for j in range(p1_chunks):
            if nch_seg == 1:
                c = it * p1_chunks + j
            else:
                c = (j // per_seq) * nch_seg + it * per_seq + j % per_seq
            rows = block_rows(c)
            g_c = g_s[rows, :]
            beta_c = beta_s[rows, :]
            for p in pairs:
                units.append((j, rows, p))
                pre[j, p] = (k_s[rows, pcols[p]], q_s[rows, pcols[p]], v_s[rows, pcols[p]])
            gc = jnp.dot(ltri, g_c, preferred_element_type=F32,
                         precision=lax.Precision.HIGHEST)
            gc_rows = jnp.concatenate([gc, gc], axis=0).T
            glast = gc[ch - 1:ch, :]
            pre[j] = (c, gc, gc_rows, beta_c, jnp.exp(gc), jnp.exp(glast - gc), jnp.exp(glast))

        dmat, lhs, rk, rhs = [], [], [], []
        for j, rows, p in units:
            kp, qp, vp = pre[j, p]
            _, gc, gc_rows, beta_c, egc, _, _ = pre[j]
            g_col = jnp.where(hi, gc[:, 2 * p + 1:2 * p + 2], gc[:, 2 * p:2 * p + 1])
            g_row = jnp.where(hi_row, gc_rows[2 * p + 1:2 * p + 2, :], gc_rows[2 * p:2 * p + 1, :])
            dmat.append(jnp.where(causal, jnp.exp(jnp.where(causal, g_col - g_row, 0.0)), 0.0))
            kb = kp * lanes_of_pair(beta_c, p)
            lhs.append(jnp.concatenate([kb, qp], axis=0).astype(BF16))
            rk.append(bd_wide(kp.astype(BF16)))
            rhs.append((vp * lanes_of_pair(beta_c, p), kb * lanes_of_pair(egc, p)))
        kkqk = [lax.dot_general(a, b, (((1,), (1,)), ((), ())), preferred_element_type=F32)
                for a, b in zip(lhs, rk)]
        fill()
        a_mat = [jnp.where(strict, x[:ch] * d, 0.0) for x, d in zip(kkqk, dmat)]
        t_x = _inv_unit_lower(a_mat, ri, cj, ch, mm_pairs)
        duw = [jnp.dot(t.astype(BF16),
                       jnp.concatenate([bd_wide(ru.astype(BF16)), bd_wide(rw.astype(BF16))], axis=1),
                       preferred_element_type=F32)
               for t, (ru, rw) in zip(t_x, rhs)]

        fill()
        for i, (j, rows, p) in enumerate(units):
            kp, qp = k_s[rows, pcols[p]], q_s[rows, pcols[p]]
            _, _, _, _, egc, ekd, _ = pre[j]
            qk_s[p, rows, :] = (kkqk[i][ch:] * dmat[i]).astype(BF16)
            u_s[rows, pcols[p]] = rhs[i][0] + duw[i][:, :2 * HEAD_D]
            w_s[rows, pcols[p]] = (rhs[i][1] + duw[i][:, 2 * HEAD_D:]).astype(BF16)
            qd_s[rows, pcols[p]] = (qp * lanes_of_pair(egc, p)).astype(BF16)
            kd_s[rows, pcols[p]] = kp * lanes_of_pair(ekd, p)
        for j in range(p1_chunks):
            egl_s[pl.ds(pre[j][0], 1), :] = pre[j][6]


    ngroups = nseg // P2_SEGS
    assert ngroups == 1 or nch_seg == 1

    def phase2(it):
        c, sg = (it, 0) if ngroups == 1 else (0, it)
        units = []
        egl = {}
        for i in range(P2_SEGS):
            sq = sg * P2_SEGS + i
            blk = sq * nch_seg + c
            rows = block_rows(blk)
            egl[i] = egl_s[pl.ds(blk, 1), :]
            units += [(i, sq, rows, p) for p in pairs]
        s_old = {(i, hd): s_ref[sq, hd] for i, sq, _, p in units for hd in (2 * p, 2 * p + 1)}
        wq = [jnp.concatenate([w_s[rows, pcols[p]], qd_s[rows, pcols[p]]], axis=0)
              for _, _, rows, p in units]
        u = [u_s[rows, pcols[p]] for _, _, rows, p in units]
        qk = [qk_s[p, rows, :] for _, _, rows, p in units]
        kd_t = {(i, hd): kd_s[rows, hcols[hd]].T.astype(BF16)
                for i, _, rows, p in units for hd in (2 * p, 2 * p + 1)}
        s_bd = [_block_diag2(s_old[i, 2 * p].astype(BF16), s_old[i, 2 * p + 1].astype(BF16))
                for i, _, _, p in units]
        ws = [jnp.dot(a, sb, preferred_element_type=F32) for a, sb in zip(wq, s_bd)]
        v_new = [(a - b[:ch]).astype(BF16) for a, b in zip(u, ws)]
        o = [b[ch:] + jnp.dot(a, bd_wide(v), preferred_element_type=F32)
             for a, b, v in zip(qk, ws, v_new)]
        s_new = {}
        for n, (i, _, _, p) in enumerate(units):
            for half, hd in enumerate((2 * p, 2 * p + 1)):
                s_new[i, hd] = s_old[i, hd] * egl[i][:, hd:hd + 1] + jnp.dot(
                    kd_t[i, hd], v_new[n][:, half * HEAD_D:(half + 1) * HEAD_D],
                    preferred_element_type=F32)
        for n, (i, sq, rows, p) in enumerate(units):
            o_s[rows, pcols[p]] = o[n]
            s_ref[sq, 2 * p] = s_new[i, 2 * p]
            s_ref[sq, 2 * p + 1] = s_new[i, 2 * p + 1]

    n_p1, n_p2 = nch // p1_chunks, nch_seg * ngroups
    phase1(0)
    done2 = 0
    for it1 in range(1, n_p1 + 1):
        if it1 < n_p1:
            phase1(it1)
        ready2 = n_p2 * it1 // n_p1
        for it2 in range(done2, ready2):
            phase2(it2)
        done2 = ready2
    while fill_queue:
        fill()

    def stage_gate_proj(hf):
        lhs = hbf_s[hf["rs"], :]
        hf["za"] = jnp.dot(lhs, w_za[...], preferred_element_type=F32)
        hf["ga"] = jnp.dot(lhs, w_gate[:, :D_MODEL], preferred_element_type=F32)

    def stage_gated_norm(hf):
        ya = []
        for hd in range(N_HEADS):
            hc = slice(hd * HEAD_D, (hd + 1) * HEAD_D)
            oh = o_s[hf["rs"], hc]
            ms = jnp.mean(oh * oh, axis=-1, keepdims=True)
            ya.append((oh * lax.rsqrt(ms + RMS_EPS) * norm_a_g[...]
                       * _silu(hf["za"][:, hc])).astype(BF16))
        hf["ya"] = jnp.concatenate(ya, axis=1)

    def stage_merge(hf):
        pa = jnp.dot(hf.pop("ya"), w_proj_a[...], preferred_element_type=F32)
        hf["merged"] = (_sigmoid(hf.pop("ga")) * pa + pbg_s[hf["rs"], :]).astype(BF16)

    def stage_out_proj(hf):
        hf["mo"] = jnp.dot(hf.pop("merged"), w_out[...], preferred_element_type=F32)

    def stage_norm1(hf):
        h_in = y_ref[hf["sq"]].reshape(tt // 2, D_MODEL)
        hf["h1"] = _layer_norm(ALPHA * h_in + hf.pop("mo"), ln1_g[...], ln1_b[...])

    def stage_ple_proj(hf):
        hf["pg"] = jnp.dot(hf["h1"].astype(BF16), w_ple_gate[...], preferred_element_type=F32)
        hf["pp"] = jnp.dot(p_ref[0, hf["sq"]].reshape(tt // 2, P_DIM).astype(BF16), w_ple[...],
                           preferred_element_type=F32)

    def stage_norm2(hf):
        ple = _sigmoid(hf.pop("pg")) * hf.pop("pp")
        y_ref[hf["sq"]] = _layer_norm(ALPHA * hf.pop("h1") + ple, ln2_g[...], ln2_b[...]).reshape(
            nseg // 2, seg, D_MODEL)

    for stage in (stage_gate_proj, stage_gated_norm, stage_merge, stage_out_proj, stage_norm1,
                  stage_ple_proj, stage_norm2):
        for hf in halves:
            stage(hf)


def _encode(x, p, init_state, weights, *, nseg, seg, ch):
    nseq, t_len, _ = x.shape
    zero_init = init_state is None
    state_shapes = ((nseq, SUBLANES, QKV_DIM), (nseq, N_HEADS, HEAD_D, HEAD_D),
                    (nseq, SUBLANES, WIDTH_B))
    nb, nt = nseq // nseg, t_len // seg
    carry = nt > 1
    assert nseq % nseg == 0 and t_len % seg == 0 and seg % ch == 0 and seg % SUBLANES == 0
    assert ch & (ch - 1) == 0 and ch % INV_BLOCK == 0 and nseg % P2_SEGS == 0
    tt = nseg * seg
    nch = tt // ch
    cfg = (nseg, seg, ch, carry, zero_init)

    def tile_spec(width):
        return pl.BlockSpec((nseg, seg, width), lambda b, t: (b, t, 0))

    def state_spec(shape):
        nd = len(shape)
        return pl.BlockSpec((nseg,) + tuple(shape[1:]), lambda b, t: (b,) + (0,) * (nd - 1),
                            pipeline_mode=pl.Buffered(1))

    def const_spec(arr):
        nd = arr.ndim
        return pl.BlockSpec(arr.shape, lambda b, t: (0,) * nd, pipeline_mode=pl.Buffered(1))

    def cols_spec(arr, width, offset):
        assert offset % width == 0
        return pl.BlockSpec((arr.shape[0], width), lambda b, t: (0, offset // width),
                            pipeline_mode=pl.Buffered(1))

    w_in_packed, others = weights
    p_spec = pl.BlockSpec((1, nseg, seg, P_DIM), lambda b, t: (0, b, t, 0))
    in_specs = [tile_spec(D_MODEL), p_spec]
    operands = [x, p]
    if not zero_init:
        assert tuple(a.shape for a in init_state) == state_shapes
        in_specs += [state_spec(shape) for shape in state_shapes]
        operands += list(init_state)
    for name, arr in others:
        if name == "w_in":
            for width, offset in PACKED_COLS:
                in_specs.append(cols_spec(w_in_packed, width, offset))
                operands.append(w_in_packed)
        else:
            in_specs.append(const_spec(arr))
            operands.append(arr)
    out_shape = (jax.ShapeDtypeStruct(x.shape, F32),) + tuple(
        jax.ShapeDtypeStruct(shape, F32) for shape in state_shapes)
    out_specs = (tile_spec(D_MODEL),) + tuple(state_spec(shape) for shape in state_shapes)
    scratch = [
        pltpu.VMEM((tt, D_MODEL), BF16),
        pltpu.VMEM((tt, KEY_DIM), F32),
        pltpu.VMEM((tt, KEY_DIM), F32),
        pltpu.VMEM((tt, KEY_DIM), F32),
        pltpu.VMEM((tt, LANES), F32),
        pltpu.VMEM((tt, LANES), F32),
        pltpu.VMEM((max(nch, SUBLANES), LANES), F32),
        pltpu.VMEM((tt, KEY_DIM), BF16),
        pltpu.VMEM((tt, KEY_DIM), BF16),
        pltpu.VMEM((N_HEADS // 2, tt, 2 * ch), BF16),
        pltpu.VMEM((SUBLANES + seg, QKV_CHUNK), F32),
        pltpu.VMEM((tt, D_MODEL), F32),
        pltpu.VMEM((tt, WIDTH_B), BF16),
    ]
    return pl.pallas_call(
        functools.partial(_layer_kernel, cfg),
        grid=(nb, nt),
        in_specs=in_specs,
        out_specs=out_specs,
        out_shape=out_shape,
        scratch_shapes=scratch,
        compiler_params=pltpu.CompilerParams(
            dimension_semantics=("arbitrary", "arbitrary"),
            vmem_limit_bytes=VMEM_LIMIT_BYTES),
        name=f"gdn_shortconv_layer_n{nseg}_t{seg}_c{ch}",
    )(*operands)


def _pack_kernel(a_ref, b_ref, o_ref):
    j = pl.program_id(0)
    n_plain = OFF_BETA // PACK_W
    n_shift = (IN_DIM - OFF_BB) // PACK_W
    shift = OFF_BB - OFF_BETA
    row = lax.broadcasted_iota(jnp.int32, (LANES, 1), 0)

    @pl.when(j < n_plain)
    def _():
        o_ref[...] = a_ref[...].T.astype(BF16)

    @pl.when((j >= n_plain) & (j < n_plain + n_shift))
    def _():
        o_ref[...] = jnp.concatenate([a_ref[shift:, :], b_ref[...]], axis=0).T.astype(BF16)

    @pl.when(j == n_plain + n_shift)
    def _():
        beta = jnp.where(row < N_HEADS, a_ref[0:LANES, :], 0.0)
        decay = jnp.where(row < N_HEADS, a_ref[N_HEADS:N_HEADS + LANES, :], 0.0)
        o_ref[:, :LANES] = beta.T.astype(BF16)
        o_ref[:, LANES:2 * LANES] = decay.T.astype(BF16)
        o_ref[:, 2 * LANES:] = jnp.zeros((o_ref.shape[0], PACK_W - 2 * LANES), BF16)


def _pack_w_in(w_t):
    n_plain = OFF_BETA // PACK_W
    n_shift = (IN_DIM - OFF_BB) // PACK_W
    shift = OFF_BB - OFF_BETA
    assert OFF_BETA % PACK_W == 0 and (IN_DIM - OFF_BB) % PACK_W == 0
    assert OFF_DECAY - OFF_BETA == N_HEADS == SUBLANES and shift % SUBLANES == 0

    def a_idx(j):
        return (0, jnp.where(j < n_plain + n_shift, j, n_plain), 0)

    def b_idx(j):
        return (0, jnp.where((j >= n_plain) & (j < n_plain + n_shift), (j + 1) * (PACK_W // shift), 0), 0)

    return pl.pallas_call(
        _pack_kernel,
        grid=(n_plain + n_shift + 1,),
        in_specs=[pl.BlockSpec((None, PACK_W, D_MODEL), a_idx),
                  pl.BlockSpec((None, shift, D_MODEL), b_idx)],
        out_specs=pl.BlockSpec((D_MODEL, PACK_W), lambda j: (0, j)),
        out_shape=jax.ShapeDtypeStruct((D_MODEL, PACKED_DIM), BF16),
        compiler_params=pltpu.CompilerParams(dimension_semantics=("arbitrary",)),
        name="pack_w_in",
    )(w_t, w_t)


def _pad_rows_front(a, rows):
    pad = [(0, 0)] * a.ndim
    pad[-2] = (rows - a.shape[-2], 0)
    return jnp.pad(a, pad)


def _pad_lanes(a):
    pad = [(0, 0)] * a.ndim
    pad[-1] = (0, LANES - a.shape[-1])
    return jnp.pad(a, pad)


def kernel(x_prompt, x_sample, state_conv_a, state_gdn, state_conv_b, p_prompt, p_sample, ln_in_g, ln_in_b, w_in, w_conv_a, a_log, dt_bias, norm_a_g, w_conv_b, w_proj_a, w_proj_b, w_out, ln1_g, ln1_b, w_ple, w_ple_gate, ln2_g, ln2_b):
    assert w_in.shape[0] == DEPTH == 1
    row = lambda v: v.reshape(1, -1).astype(F32)
    w_in_packed = _pack_w_in(jnp.swapaxes(w_in, 1, 2))
    others = (
        ("ln_in_g", row(ln_in_g)), ("ln_in_b", row(ln_in_b)),
        ("w_in", None),
        ("w_conv_a", w_conv_a[0].astype(F32)),
        ("a_log", _pad_lanes(row(a_log[0]))), ("dt_bias", _pad_lanes(row(dt_bias[0]))),
        ("norm_a_g", row(norm_a_g[0])),
        ("w_conv_b", w_conv_b[0].astype(F32)),
        ("w_proj_a", w_proj_a[0].astype(BF16)), ("w_proj_b", w_proj_b[0].astype(BF16)),
        ("w_out", w_out[0].astype(BF16)),
        ("ln1_g", row(ln1_g[0])), ("ln1_b", row(ln1_b[0])),
        ("w_ple", w_ple[0].astype(BF16)), ("w_ple_gate", w_ple_gate[0].astype(BF16)),
        ("ln2_g", row(ln2_g[0])), ("ln2_b", row(ln2_b[0])),
    )
    weights = (w_in_packed, others)

    bp, seq, _ = x_prompt.shape
    y_p, ca_p, s_p, cb_p = _encode(
        x_prompt, p_prompt, None, weights, nseg=bp, seg=PROMPT_TILE, ch=PROMPT_CHUNK)

    bs, ts, _ = x_sample.shape
    y_s, ca_s, s_s, cb_s = _encode(
        x_sample, p_sample,
        (_pad_rows_front(state_conv_a[0], SUBLANES), state_gdn[0].astype(F32),
         _pad_rows_front(state_conv_b[0], SUBLANES)),
        weights, nseg=SAMPLE_TILE_SEQS, seg=ts, ch=ts)

    na, nb = CONV_A - 1, CONV_B - 1
    return (y_p, y_s,
            ca_p[None, :, SUBLANES - na:], s_p[None], cb_p[None, :, SUBLANES - nb:],
            ca_s[None, :, SUBLANES - na:], s_s[None].astype(state_gdn.dtype),
            cb_s[None, :, SUBLANES - nb:])
```
